```python
import math, functools
import jax, jax.numpy as jnp
from jax import lax
import numpy as np

D_MODEL = 2048
BATCH = 8
SEQ = 2048
DEPTH = 2

GRID_W = 64
CTX_LEN = 256
Q_BLOCK = 128
ROPE_THETA = 10000.0
HEAD_DIM = 128
DIFF_HEADS = D_MODEL // (2 * HEAD_DIM)
DIFF_QK_DIM = HEAD_DIM // 2
DIFF_V_DIM = HEAD_DIM
GQA_HEADS = D_MODEL // (2 * HEAD_DIM)
GQA_KV_HEADS = GQA_HEADS // 4
AB_IN_WIDTH = 2 * DIFF_HEADS * 2 * DIFF_QK_DIM + DIFF_HEADS * DIFF_V_DIM + (GQA_HEADS + 2 * GQA_KV_HEADS) * HEAD_DIM
MIX_WIDTH = DIFF_HEADS * DIFF_V_DIM + GQA_HEADS * HEAD_DIM
MLA_HEADS = D_MODEL // HEAD_DIM
MLA_Q_RANK = 512
MLA_KV_RANK = 512
MLA_NOPE_DIM = 128
MLA_ROPE_DIM = 64
MLA_V_DIM = 128
MLA_DOWN_WIDTH = MLA_Q_RANK + MLA_KV_RANK + MLA_ROPE_DIM
N_EXPERTS = 64
EXPERT_DIM = 512
TOP_K = 8
N_GROUPS = 8
TOPK_GROUPS = 4
ROUTE_SCALE = 2.5
EXPERT_BLOCK = 128
ALPHA = (2 * DEPTH) ** 0.25
BETA = (8 * DEPTH) ** -0.25
N_EVEN = (DEPTH + 1) // 2
N_ODD = DEPTH // 2

kernel_name = 'hybrid_diffgqa_mla_moe_dit'


def rms_norm(x, g, eps=1e-6):
    xf = x.astype(jnp.float32)
    y = xf * lax.rsqrt(jnp.mean(xf * xf, axis=-1, keepdims=True) + eps)
    return (y * g.astype(jnp.float32)).astype(x.dtype)


def layer_norm(x, g, b, eps=1e-5):
    xf = x.astype(jnp.float32)
    xc = xf - jnp.mean(xf, axis=-1, keepdims=True)
    var = jnp.mean(xc * xc, axis=-1, keepdims=True)
    return (xc * lax.rsqrt(var + eps) * g.astype(jnp.float32) + b.astype(jnp.float32)).astype(x.dtype)


def axial_angles(rows, rot_dim):
    t = jnp.arange(rows * GRID_W)
    row = (t // GRID_W).astype(jnp.float32)
    col = (t % GRID_W).astype(jnp.float32)
    axis_dim = rot_dim // 2
    inv_freq = 1.0 / (ROPE_THETA ** (jnp.arange(0, axis_dim, 2, dtype=jnp.float32) / axis_dim))
    return row[:, None] * inv_freq, col[:, None] * inv_freq


def rope_1d(x, ang):
    a = ang.reshape((ang.shape[0],) + (1,) * (x.ndim - 3) + (ang.shape[1],))
    cos = jnp.cos(a).astype(x.dtype)
    sin = jnp.sin(a).astype(x.dtype)
    x1, x2 = jnp.split(x, 2, axis=-1)
    return jnp.concatenate([x1 * cos - x2 * sin, x2 * cos + x1 * sin], axis=-1)


def axial_rope(x, angs):
    xr, xc = jnp.split(x, 2, axis=-1)
    return jnp.concatenate([rope_1d(xr, angs[0]), rope_1d(xc, angs[1])], axis=-1)


def sweep_query_blocks(attend, q):
    B, L = q.shape[0], q.shape[1]
    nb = L // Q_BLOCK
    qb = jnp.moveaxis(q.reshape((B, nb, Q_BLOCK) + q.shape[2:]), 1, 0)
    out = lax.map(attend, qb)
    return jnp.moveaxis(out, 0, 1).reshape((B, L) + out.shape[3:])


def diff_attend(q, k, v, lam):
    logits = jnp.einsum('bqhid,bkhid->bhiqk', q, k).astype(jnp.float32) * (DIFF_QK_DIM ** -0.5)
    p = jax.nn.softmax(logits, axis=-1)
    a = (p[:, :, 0] - lam * p[:, :, 1]).astype(v.dtype)
    return jnp.einsum('bhqk,bkhd->bqhd', a, v)


def gqa_attend(q, k, v, scale):
    B, Lq, Hq, d = q.shape
    Hk = k.shape[2]
    qg = q.reshape(B, Lq, Hk, Hq // Hk, d)
    logits = jnp.einsum('bqhgd,bkhd->bhgqk', qg, k).astype(jnp.float32) * scale
    p = jax.nn.softmax(logits, axis=-1).astype(v.dtype)
    o = jnp.einsum('bhgqk,bkhd->bqhgd', p, v)
    return o.reshape(B, Lq, Hq, v.shape[-1])


def split_ab(proj):
    B, L = proj.shape[0], proj.shape[1]
    sizes = (DIFF_HEADS * 2 * DIFF_QK_DIM, DIFF_HEADS * 2 * DIFF_QK_DIM, DIFF_HEADS * DIFF_V_DIM,
             GQA_HEADS * HEAD_DIM, GQA_KV_HEADS * HEAD_DIM, GQA_KV_HEADS * HEAD_DIM)
    qa, ka, va, qb, kb, vb = jnp.split(proj, np.cumsum(sizes)[:-1].tolist(), axis=-1)
    return (qa.reshape(B, L, DIFF_HEADS, 2, DIFF_QK_DIM), ka.reshape(B, L, DIFF_HEADS, 2, DIFF_QK_DIM),
            va.reshape(B, L, DIFF_HEADS, DIFF_V_DIM), qb.reshape(B, L, GQA_HEADS, HEAD_DIM),
            kb.reshape(B, L, GQA_KV_HEADS, HEAD_DIM), vb.reshape(B, L, GQA_KV_HEADS, HEAD_DIM))


def mixer_diff_gqa(h_lat, h_ctx, w_in, w_out, lam_params, subln_g, q_norm_g, k_norm_g,
                   lam_init, angs64, angs128, need_ctx):
    qa_l, ka_l, va_l, qb_l, kb_l, vb_l = split_ab(h_lat @ w_in)
    qa_c, ka_c, va_c, qb_c, kb_c, vb_c = split_ab(h_ctx @ w_in)
    qb_l, qb_c = rms_norm(qb_l, q_norm_g), rms_norm(qb_c, q_norm_g)
    kb_l, kb_c = rms_norm(kb_l, k_norm_g), rms_norm(kb_c, k_norm_g)
    qa_l, ka_l = axial_rope(qa_l, angs64), axial_rope(ka_l, angs64)
    qb_l, kb_l = axial_rope(qb_l, angs128), axial_rope(kb_l, angs128)
    lp = lam_params.astype(jnp.float32)
    lam = jnp.exp(jnp.sum(lp[0] * lp[1])) - jnp.exp(jnp.sum(lp[2] * lp[3])) + lam_init

    def heads_out(qa, ka, va, qb, kb, vb):
        B, L = qa.shape[0], qa.shape[1]
        oa = sweep_query_blocks(lambda qblk: diff_attend(qblk, ka, va, lam), qa)
        oa = rms_norm(oa, subln_g, 1e-5) * (1.0 - lam_init)
        ob = sweep_query_blocks(lambda qblk: gqa_attend(qblk, kb, vb, HEAD_DIM ** -0.5), qb)
        merged = jnp.concatenate([oa.reshape(B, L, -1), ob.reshape(B, L, -1)], axis=-1)
        return merged @ w_out

    y_lat = heads_out(qa_l, jnp.concatenate([ka_c, ka_l], 1), jnp.concatenate([va_c, va_l], 1),
                      qb_l, jnp.concatenate([kb_c, kb_l], 1), jnp.concatenate([vb_c, vb_l], 1))
    y_ctx = heads_out(qa_c, ka_c, va_c, qb_c, kb_c, vb_c) if need_ctx else None
    return y_lat, y_ctx


def mixer_mla(h_lat, h_ctx, w_down, q_norm_g, w_uq, kv_norm_g, w_ukv, w_o, angs64, need_ctx):
    def project(h):
        B, L = h.shape[0], h.shape[1]
        cq, ckv, k_pe = jnp.split(h @ w_down, [MLA_Q_RANK, MLA_Q_RANK + MLA_KV_RANK], axis=-1)
        q = (rms_norm(cq, q_norm_g) @ w_uq).reshape(B, L, MLA_HEADS, MLA_NOPE_DIM + MLA_ROPE_DIM)
        kv = (rms_norm(ckv, kv_norm_g) @ w_ukv).reshape(B, L, MLA_HEADS, MLA_NOPE_DIM + MLA_V_DIM)
        q_nope, q_pe = jnp.split(q, [MLA_NOPE_DIM], axis=-1)
        k_nope, v = jnp.split(kv, [MLA_NOPE_DIM], axis=-1)
        return q_nope, q_pe, k_nope, k_pe[:, :, None, :], v

    def assemble(nope, pe):
        return jnp.concatenate([nope, jnp.broadcast_to(pe, nope.shape[:3] + (MLA_ROPE_DIM,))], axis=-1)

    ql_n, ql_pe, kl_n, kl_pe, v_l = project(h_lat)
    ql_pe, kl_pe = axial_rope(ql_pe, angs64), axial_rope(kl_pe, angs64)
    qc_n, qc_pe, kc_n, kc_pe, v_c = project(h_ctx)
    q_l, k_l = assemble(ql_n, ql_pe), assemble(kl_n, kl_pe)
    q_c, k_c = assemble(qc_n, qc_pe), assemble(kc_n, kc_pe)
    scale = (MLA_NOPE_DIM + MLA_ROPE_DIM) ** -0.5

    def heads_out(q, k, v):
        B, L = q.shape[0], q.shape[1]
        o = sweep_query_blocks(lambda qblk: gqa_attend(qblk, k, v, scale), q)
        return o.reshape(B, L, -1) @ w_o

    y_lat = heads_out(q_l, jnp.concatenate([k_c, k_l], 1), jnp.concatenate([v_c, v_l], 1))
    y_ctx = heads_out(q_c, k_c, v_c) if need_ctx else None
    return y_lat, y_ctx


def swiglu(h, w_in, w_out):
    g, u = jnp.split(h @ w_in, 2, axis=-1)
    return (jax.nn.silu(g) * u) @ w_out


def moe_ffn(h, router_w, router_bias, w_in, w_out, sh_in, sh_out):
    T, D = h.shape
    scores = jax.nn.sigmoid((h @ router_w).astype(jnp.float32))
    biased = scores + router_bias.astype(jnp.float32)
    grouped = biased.reshape(T, N_GROUPS, N_EXPERTS // N_GROUPS)
    group_score = lax.top_k(grouped, 2)[0].sum(axis=-1)
    _, top_groups = lax.top_k(group_score, TOPK_GROUPS)
    group_ok = (top_groups[..., None] == jnp.arange(N_GROUPS)).any(axis=1)
    expert_ok = jnp.repeat(group_ok, N_EXPERTS // N_GROUPS, axis=-1)
    _, top_e = lax.top_k(jnp.where(expert_ok, biased, -jnp.inf), TOP_K)
    gate = jnp.take_along_axis(scores, top_e, axis=-1)
    gate = gate / jnp.sum(gate, axis=-1, keepdims=True) * ROUTE_SCALE

    TK = T * TOP_K
    flat_e = top_e.reshape(-1).astype(jnp.int32)
    order = jnp.argsort(flat_e, stable=True)
    sorted_e = flat_e[order]
    counts = jnp.bincount(flat_e, length=N_EXPERTS)
    starts = jnp.cumsum(counts) - counts
    padded = (counts + EXPERT_BLOCK - 1) // EXPERT_BLOCK * EXPERT_BLOCK
    pad_ends = jnp.cumsum(padded)
    pad_starts = pad_ends - padded
    dest = pad_starts[sorted_e] + (jnp.arange(TK, dtype=jnp.int32) - starts[sorted_e])
    n_blocks = -(-TK // EXPERT_BLOCK) + N_EXPERTS
    P = n_blocks * EXPERT_BLOCK
    slot_tok = jnp.full((P,), T, dtype=jnp.int32).at[dest].set((order // TOP_K).astype(jnp.int32))
    slot_w = jnp.zeros((P,), h.dtype).at[dest].set(gate.reshape(-1)[order].astype(h.dtype))
    block_e = jnp.minimum(jnp.searchsorted(pad_ends, jnp.arange(n_blocks) * EXPERT_BLOCK, side='right'),
                          N_EXPERTS - 1)
    h_pad = jnp.concatenate([h, jnp.zeros((1, D), h.dtype)], axis=0)

    def run_block(args):
        tok, wts, e = args
        return swiglu(h_pad[tok], w_in[e], w_out[e]) * wts[:, None]

    out = lax.map(run_block, (slot_tok.reshape(n_blocks, EXPERT_BLOCK),
                              slot_w.reshape(n_blocks, EXPERT_BLOCK), block_e))
    routed = jax.ops.segment_sum(out.reshape(P, D), slot_tok, num_segments=T + 1)[:T]
    return routed + swiglu(h, sh_in, sh_out)


def setup_inputs(seed: int = 0) -> dict:
    key = jax.random.key(seed)
    ks = iter(jax.random.split(key, 32))

    def normal(shape, scale):
        return jax.random.normal(next(ks), shape, jnp.float32) * scale

    def gain(shape):
        return 1.0 + normal(shape, 0.05)

    D, F = D_MODEL, EXPERT_DIM
    return {
        'x': normal((BATCH, SEQ, D), 1.0),
        'c': normal((BATCH, D), 1.0),
        'ctx': normal((BATCH, CTX_LEN, D), 1.0),
        'c_ctx': normal((D,), 1.0),
        'ada_w': normal((DEPTH, D, 6 * D), 0.015),
        'ada_b': normal((DEPTH, 6 * D), 0.02),
        'ln_mix_g': gain((DEPTH, D)),
        'ln_mix_b': normal((DEPTH, D), 0.02),
        'ln_ffn_g': gain((DEPTH, D)),
        'ln_ffn_b': normal((DEPTH, D), 0.02),
        'ab_w_in': normal((N_EVEN, D, AB_IN_WIDTH), D ** -0.5),
        'ab_w_out': normal((N_EVEN, MIX_WIDTH, D), MIX_WIDTH ** -0.5 * BETA),
        'diff_lambda': normal((N_EVEN, 4, DIFF_QK_DIM), 0.1),
        'diff_subln_g': gain((N_EVEN, DIFF_V_DIM)),
        'gqa_q_norm_g': gain((N_EVEN, HEAD_DIM)),
        'gqa_k_norm_g': gain((N_EVEN, HEAD_DIM)),
        'mla_w_down': normal((N_ODD, D, MLA_DOWN_WIDTH), D ** -0.5),
        'mla_q_norm_g': gain((N_ODD, MLA_Q_RANK)),
        'mla_w_uq': normal((N_ODD, MLA_Q_RANK, MLA_HEADS * (MLA_NOPE_DIM + MLA_ROPE_DIM)), MLA_Q_RANK ** -0.5),
        'mla_kv_norm_g': gain((N_ODD, MLA_KV_RANK)),
        'mla_w_ukv': normal((N_ODD, MLA_KV_RANK, MLA_HEADS * (MLA_NOPE_DIM + MLA_V_DIM)), MLA_KV_RANK ** -0.5),
        'mla_w_o': normal((N_ODD, MLA_HEADS * MLA_V_DIM, D), (MLA_HEADS * MLA_V_DIM) ** -0.5 * BETA),
        'router_w': normal((DEPTH, D, N_EXPERTS), D ** -0.5),
        'router_bias': normal((DEPTH, N_EXPERTS), 0.01),
        'expert_w_in': normal((DEPTH, N_EXPERTS, D, 2 * F), D ** -0.5),
        'expert_w_out': normal((DEPTH, N_EXPERTS, F, D), F ** -0.5 * BETA),
        'shared_w_in': normal((DEPTH, D, 2 * F), D ** -0.5),
        'shared_w_out': normal((DEPTH, F, D), F ** -0.5 * BETA),
    }


def reference(x, c, ctx, c_ctx, ada_w, ada_b, ln_mix_g, ln_mix_b, ln_ffn_g, ln_ffn_b,
              ab_w_in, ab_w_out, diff_lambda, diff_subln_g, gqa_q_norm_g, gqa_k_norm_g,
              mla_w_down, mla_q_norm_g, mla_w_uq, mla_kv_norm_g, mla_w_ukv, mla_w_o,
              router_w, router_bias, expert_w_in, expert_w_out, shared_w_in, shared_w_out):
    B, S, D = x.shape
    C = ctx.shape[1]
    rows = S // GRID_W
    angs64 = axial_angles(rows, 64)
    angs128 = axial_angles(rows, HEAD_DIM)
    c_act = jax.nn.silu(c)
    cc_act = jax.nn.silu(c_ctx)
    for layer in range(DEPTH):
        need_ctx = layer < DEPTH - 1
        i = layer // 2
        mod_lat = c_act @ ada_w[layer] + ada_b[layer]
        mod_ctx = cc_act @ ada_w[layer] + ada_b[layer]
        sh1, sc1, g1, sh2, sc2, g2 = jnp.split(mod_lat[:, None, :], 6, axis=-1)
        csh1, csc1, cg1, csh2, csc2, cg2 = jnp.split(mod_ctx, 6, axis=-1)

        h_lat = x * (1.0 + sc1) + sh1
        h_ctx = ctx * (1.0 + csc1) + csh1
        if layer % 2 == 0:
            y_lat, y_ctx = mixer_diff_gqa(h_lat, h_ctx, ab_w_in[i], ab_w_out[i], diff_lambda[i], diff_subln_g[i],
                                          gqa_q_norm_g[i], gqa_k_norm_g[i], 0.8 - 0.6 * math.exp(-0.3 * layer),
                                          angs64, angs128, need_ctx)
        else:
            y_lat, y_ctx = mixer_mla(h_lat, h_ctx, mla_w_down[i], mla_q_norm_g[i], mla_w_uq[i], mla_kv_norm_g[i],
                                     mla_w_ukv[i], mla_w_o[i], angs64, need_ctx)
        x = layer_norm(ALPHA * x + g1 * y_lat, ln_mix_g[layer], ln_mix_b[layer])
        if need_ctx:
            ctx = layer_norm(ALPHA * ctx + cg1 * y_ctx, ln_mix_g[layer], ln_mix_b[layer])

        h_lat = (x * (1.0 + sc2) + sh2).reshape(B * S, D)
        if need_ctx:
            h_ctx = (ctx * (1.0 + csc2) + csh2).reshape(B * C, D)
            tokens = jnp.concatenate([h_ctx, h_lat], axis=0)
        else:
            tokens = h_lat
        y = moe_ffn(tokens, router_w[layer], router_bias[layer], expert_w_in[layer], expert_w_out[layer],
                    shared_w_in[layer], shared_w_out[layer])
        x = layer_norm(ALPHA * x + g2 * y[tokens.shape[0] - B * S:].reshape(B, S, D),
                       ln_ffn_g[layer], ln_ffn_b[layer])
        if need_ctx:
            ctx = layer_norm(ALPHA * ctx + cg2 * y[:B * C].reshape(B, C, D), ln_ffn_g[layer], ln_ffn_b[layer])
    return x
```

```python
import functools
import math

import numpy as np
import jax
import jax.numpy as jnp
from jax import lax
from jax.experimental import pallas as pl
from jax.experimental.pallas import tpu as pltpu

F32 = jnp.float32
BF16 = jnp.bfloat16

GRID_W = 64
ROPE_THETA = 10000.0
HEAD_DIM = 128
DIFF_QK_DIM = 64
N_EXPERTS = 64
EXPERT_DIM = 512
TOP_K = 8
N_GROUPS = 8
TOPK_GROUPS = 4
ROUTE_SCALE = 2.5
MLA_Q_RANK = 512
MLA_KV_RANK = 512
MLA_NOPE_DIM = 128
MLA_ROPE_DIM = 64
MLA_V_DIM = 128
LANES = 128

MOD_ROWS = 16
VMEM_LIMIT = 56 * 1024 * 1024
MOE_BLOCK = 256


def _cparams(sem):
    return pltpu.CompilerParams(dimension_semantics=sem, vmem_limit_bytes=VMEM_LIMIT)


def _dot(a, b):
    return jnp.dot(a, b, preferred_element_type=F32)


def _dot_nt(a, b):
    return lax.dot_general(a, b, (((1,), (1,)), ((), ())), preferred_element_type=F32)


def _dot_hi(a, b):
    return lax.dot_general(a, b, (((1,), (0,)), ((), ())), precision=lax.Precision.HIGHEST,
                           preferred_element_type=F32)


def _sigmoid(x):
    return 1.0 / (1.0 + jnp.exp(-x))


def _rope(x, cos, sin_up, sin_dn, shift):
    return (x * cos + pltpu.roll(x, LANES - shift, 1) * sin_up + pltpu.roll(x, shift, 1) * sin_dn)


def _rope_tables(rows, rot_dim):
    t = jnp.arange(rows * GRID_W)
    row = (t // GRID_W).astype(F32)
    col = (t % GRID_W).astype(F32)
    axis_dim = rot_dim // 2
    quarter = rot_dim // 4
    inv_freq = 1.0 / (ROPE_THETA ** (jnp.arange(0, axis_dim, 2, dtype=F32) / axis_dim))
    lane = np.arange(LANES) % rot_dim
    is_col = lane >= axis_dim
    within = lane % axis_dim
    first = within < quarter
    freq = within % quarter
    ang = jnp.where(jnp.asarray(is_col)[None, :], col[:, None], row[:, None]) * inv_freq[freq][None, :]
    cos = jnp.cos(ang).astype(F32)
    sin = jnp.sin(ang).astype(F32)
    first = jnp.asarray(first)[None, :]
    sin_up = jnp.where(first, -sin, 0.0)
    sin_dn = jnp.where(first, 0.0, sin)
    return cos, sin_up, sin_dn, quarter


def _softmax_rows(s):
    m = jnp.max(s, axis=-1, keepdims=True)
    e = jnp.exp(s - m)
    return e * (1.0 / jnp.sum(e, axis=-1, keepdims=True))


def _rms(x, g, eps):
    return x * lax.rsqrt(jnp.mean(x * x, axis=-1, keepdims=True) + eps) * g


def _ada_kernel(c_ref, w_ref, b_ref, o_ref):
    c = c_ref[...]
    act = c * _sigmoid(c)
    o_ref[0] = _dot_hi(act, w_ref[0]) + b_ref[0]


def _ada_mod(cvec, ada_w, ada_b):
    depth, d, n = ada_w.shape
    tn = 1536
    return pl.pallas_call(
        _ada_kernel,
        grid=(depth, n // tn),
        in_specs=[pl.BlockSpec((MOD_ROWS, d), lambda l, j: (0, 0)),
                  pl.BlockSpec((1, d, tn), lambda l, j: (l, 0, j)),
                  pl.BlockSpec((1, 1, tn), lambda l, j: (l, 0, j))],
        out_specs=pl.BlockSpec((1, MOD_ROWS, tn), lambda l, j: (l, 0, j)),
        out_shape=jax.ShapeDtypeStruct((depth, MOD_ROWS, n), F32),
        compiler_params=_cparams(("arbitrary", "arbitrary")),
        name="ada_mod",
    )(cvec, ada_w, ada_b.reshape(depth, 1, n))


def _mod_row(i, tm, n_ctx_rows, seq, ctx_row):
    r0 = i * tm
    return jnp.where(r0 < n_ctx_rows, ctx_row, (r0 - n_ctx_rows) // seq)


def _modproj_kernel(x_ref, sc_ref, sh_ref, w_ref, o_ref, h_ref):
    @pl.when(pl.program_id(1) == 0)
    def _():
        h_ref[...] = (x_ref[...] * (1.0 + sc_ref[0]) + sh_ref[0]).astype(h_ref.dtype)

    o_ref[...] = _dot(h_ref[...], w_ref[...]).astype(o_ref.dtype)


def _modproj(x, mod, sc_idx, sh_idx, w, out_dtype, geom, tm=1024, tn=512):
    t, d = x.shape
    n = w.shape[1]
    tn = tn if n % tn == 0 else n
    row = functools.partial(_mod_row, tm=tm, **geom)
    return pl.pallas_call(
        _modproj_kernel,
        grid=(t // tm, n // tn),
        in_specs=[pl.BlockSpec((tm, d), lambda i, j: (i, 0)),
                  pl.BlockSpec((1, 1, d), lambda i, j: (row(i), 0, sc_idx)),
                  pl.BlockSpec((1, 1, d), lambda i, j: (row(i), 0, sh_idx)),
                  pl.BlockSpec((d, tn), lambda i, j: (0, j))],
        out_specs=pl.BlockSpec((tm, tn), lambda i, j: (i, j)),
        out_shape=jax.ShapeDtypeStruct((t, n), out_dtype),
        scratch_shapes=[pltpu.VMEM((tm, d), BF16)],
        compiler_params=_cparams(("arbitrary", "arbitrary")),
        name="modproj",
    )(x, mod, mod, w)


def _normproj_kernel(x_ref, g_ref, w_ref, o_ref, h_ref):
    @pl.when(pl.program_id(1) == 0)
    def _():
        h_ref[...] = _rms(x_ref[...], g_ref[...], 1e-6).astype(h_ref.dtype)

    o_ref[...] = _dot(h_ref[...], w_ref[...]).astype(o_ref.dtype)


def _normproj(x, col_block, g, w, row_off_blocks, n_rows, tm=1024, tn=1024):
    k = w.shape[0]
    n = w.shape[1]
    return pl.pallas_call(
        _normproj_kernel,
        grid=(n_rows // tm, n // tn),
        in_specs=[pl.BlockSpec((tm, k), lambda i, j: (i + row_off_blocks, col_block)),
                  pl.BlockSpec((1, k), lambda i, j: (0, 0)),
                  pl.BlockSpec((k, tn), lambda i, j: (0, j))],
        out_specs=pl.BlockSpec((tm, tn), lambda i, j: (i, j)),
        out_shape=jax.ShapeDtypeStruct((n_rows, n), BF16),
        scratch_shapes=[pltpu.VMEM((tm, k), BF16)],
        compiler_params=_cparams(("arbitrary", "arbitrary")),
        name="normproj",
    )(x, g.reshape(1, k), w)


def _diff_lambda(lam_ref, lam_init):
    lp = lam_ref[...]
    s01 = jnp.sum(lp[0:1] * lp[1:2], axis=-1, keepdims=True)
    s23 = jnp.sum(lp[2:3] * lp[3:4], axis=-1, keepdims=True)
    return jnp.exp(s01) - jnp.exp(s23) + lam_init


def _diff_attend(q, k, v, lam, g, lam_init):
    lane = lax.broadcasted_iota(jnp.int32, q.shape, 1)
    qs = q * (DIFF_QK_DIM ** -0.5)
    q1 = jnp.where(lane < DIFF_QK_DIM, qs, 0.0).astype(BF16)
    q2 = jnp.where(lane >= DIFF_QK_DIM, qs, 0.0).astype(BF16)
    p1 = _softmax_rows(_dot_nt(q1, k))
    p2 = _softmax_rows(_dot_nt(q2, k))
    a = (p1 - lam * p2).astype(BF16)
    o = _dot(a, v)
    return _rms(o, g, 1e-5) * (1.0 - lam_init)


def _diff_attn_kernel(q_ref, kc_ref, kl_ref, vc_ref, vl_ref, cos_ref, su_ref, sd_ref, lam_ref, g_ref,
                      o_ref, k_scr, v_scr, *, n_ctx, tq, shift, lam_init):
    i = pl.program_id(2)

    @pl.when(i == 0)
    def _prep():
        k_scr[0:n_ctx, :] = kc_ref[...]
        kl = kl_ref[...].astype(F32)
        k_scr[n_ctx:, :] = _rope(kl, cos_ref[...], su_ref[...], sd_ref[...], shift).astype(BF16)
        v_scr[0:n_ctx, :] = vc_ref[...]
        v_scr[n_ctx:, :] = vl_ref[...]

    lam = _diff_lambda(lam_ref, lam_init)
    g = g_ref[...]

    @pl.when(i == 0)
    def _ctx_queries():
        q = q_ref[...].astype(F32)
        o_ref[...] = _diff_attend(q, k_scr[0:n_ctx, :], v_scr[0:n_ctx, :], lam, g, lam_init).astype(o_ref.dtype)

    @pl.when(i > 0)
    def _lat_queries():
        r0 = pl.multiple_of((i - 1) * tq, tq)
        q = _rope(q_ref[...].astype(F32), cos_ref[pl.ds(r0, tq), :], su_ref[pl.ds(r0, tq), :],
                  sd_ref[pl.ds(r0, tq), :], shift)
        o_ref[...] = _diff_attend(q, k_scr[...], v_scr[...], lam, g, lam_init).astype(o_ref.dtype)


def _q_block(b, i, bsz, n_ctx, seq, tq):
    return jnp.where(i == 0, b * (n_ctx // tq), (bsz * n_ctx + b * seq) // tq + i - 1)


def _diff_attn(proj, tables, lam_params, subln_g, lam_init, bsz, n_ctx, seq, n_heads, tq=256):
    t = proj.shape[0]
    cos, su, sd, shift = tables
    qb = functools.partial(_q_block, bsz=bsz, n_ctx=n_ctx, seq=seq, tq=tq)
    lat0 = bsz * n_ctx // seq
    kern = functools.partial(_diff_attn_kernel, n_ctx=n_ctx, tq=tq, shift=shift, lam_init=lam_init)
    tab = pl.BlockSpec((seq, LANES), lambda b, h, i: (0, 0))
    return pl.pallas_call(
        kern,
        grid=(bsz, n_heads, 1 + seq // tq),
        in_specs=[pl.BlockSpec((tq, LANES), lambda b, h, i: (qb(b, i), h)),
                  pl.BlockSpec((n_ctx, LANES), lambda b, h, i: (b, n_heads + h)),
                  pl.BlockSpec((seq, LANES), lambda b, h, i: (lat0 + b, n_heads + h)),
                  pl.BlockSpec((n_ctx, LANES), lambda b, h, i: (b, 2 * n_heads + h)),
                  pl.BlockSpec((seq, LANES), lambda b, h, i: (lat0 + b, 2 * n_heads + h)),
                  tab, tab, tab,
                  pl.BlockSpec((4, DIFF_QK_DIM), lambda b, h, i: (0, 0)),
                  pl.BlockSpec((1, LANES), lambda b, h, i: (0, 0))],
        out_specs=pl.BlockSpec((tq, LANES), lambda b, h, i: (qb(b, i), h)),
        out_shape=jax.ShapeDtypeStruct((t, n_heads * LANES), BF16),
        scratch_shapes=[pltpu.VMEM((n_ctx + seq, LANES), BF16), pltpu.VMEM((n_ctx + seq, LANES), BF16)],
        compiler_params=_cparams(("arbitrary", "arbitrary", "arbitrary")),
        name="diff_attn",
    )(proj, proj, proj, proj, proj, cos, su, sd, lam_params, subln_g.reshape(1, LANES))


def _softmax_pv(q, k, v):
    p = _softmax_rows(_dot_nt(q, k)).astype(BF16)
    return _dot(p, v)


def _gqa_attn_kernel(q_ref, kc_ref, kl_ref, vc_ref, vl_ref, cos_ref, su_ref, sd_ref, qg_ref, kg_ref,
                     o_ref, k_scr, v_scr, *, n_ctx, tq, shift):
    g = pl.program_id(2)
    i = pl.program_id(3)

    @pl.when((g == 0) & (i == 0))
    def _prep():
        kg = kg_ref[...]
        k_scr[0:n_ctx, :] = _rms(kc_ref[...].astype(F32), kg, 1e-6).astype(BF16)
        kl = _rms(kl_ref[...].astype(F32), kg, 1e-6)
        k_scr[n_ctx:, :] = _rope(kl, cos_ref[...], su_ref[...], sd_ref[...], shift).astype(BF16)
        v_scr[0:n_ctx, :] = vc_ref[...]
        v_scr[n_ctx:, :] = vl_ref[...]

    scale = HEAD_DIM ** -0.5
    qn = _rms(q_ref[...].astype(F32), qg_ref[...], 1e-6)

    @pl.when(i == 0)
    def _ctx_queries():
        q = (qn * scale).astype(BF16)
        o_ref[...] = _softmax_pv(q, k_scr[0:n_ctx, :], v_scr[0:n_ctx, :]).astype(o_ref.dtype)

    @pl.when(i > 0)
    def _lat_queries():
        r0 = pl.multiple_of((i - 1) * tq, tq)
        q = _rope(qn, cos_ref[pl.ds(r0, tq), :], su_ref[pl.ds(r0, tq), :], sd_ref[pl.ds(r0, tq), :], shift)
        q = (q * scale).astype(BF16)
        o_ref[...] = _softmax_pv(q, k_scr[...], v_scr[...]).astype(o_ref.dtype)


def _gqa_attn(proj, tables, q_norm_g, k_norm_g, col0, bsz, n_ctx, seq, n_heads, n_kv, tq=256):
    t = proj.shape[0]
    cos, su, sd, shift = tables
    grp = n_heads // n_kv
    qb = functools.partial(_q_block, bsz=bsz, n_ctx=n_ctx, seq=seq, tq=tq)
    lat0 = bsz * n_ctx // seq
    kcol = col0 + n_heads
    vcol = kcol + n_kv
    kern = functools.partial(_gqa_attn_kernel, n_ctx=n_ctx, tq=tq, shift=shift)
    tab = pl.BlockSpec((seq, LANES), lambda b, h, g, i: (0, 0))
    vec = pl.BlockSpec((1, LANES), lambda b, h, g, i: (0, 0))
    return pl.pallas_call(
        kern,
        grid=(bsz, n_kv, grp, 1 + seq // tq),
        in_specs=[pl.BlockSpec((tq, LANES), lambda b, h, g, i: (qb(b, i), col0 + h * grp + g)),
                  pl.BlockSpec((n_ctx, LANES), lambda b, h, g, i: (b, kcol + h)),
                  pl.BlockSpec((seq, LANES), lambda b, h, g, i: (lat0 + b, kcol + h)),
                  pl.BlockSpec((n_ctx, LANES), lambda b, h, g, i: (b, vcol + h)),
                  pl.BlockSpec((seq, LANES), lambda b, h, g, i: (lat0 + b, vcol + h)),
                  tab, tab, tab, vec, vec],
        out_specs=pl.BlockSpec((tq, LANES), lambda b, h, g, i: (qb(b, i), h * grp + g)),
        out_shape=jax.ShapeDtypeStruct((t, n_heads * LANES), BF16),
        scratch_shapes=[pltpu.VMEM((n_ctx + seq, LANES), BF16), pltpu.VMEM((n_ctx + seq, LANES), BF16)],
        compiler_params=_cparams(("arbitrary", "arbitrary", "arbitrary", "arbitrary")),
        name="gqa_attn",
    )(proj, proj, proj, proj, proj, cos, su, sd, q_norm_g.reshape(1, LANES), k_norm_g.reshape(1, LANES))


def _mla_attn_kernel(qn_ref, qp_ref, knc_ref, knl_ref, kpc_ref, kpl_ref, vc_ref, vl_ref,
                     cos_ref, su_ref, sd_ref, o_ref, k_scr, v_scr, *, n_ctx, tq, shift):
    i = pl.program_id(2)

    @pl.when(i == 0)
    def _prep():
        k_scr[0:n_ctx, 0:LANES] = knc_ref[...]
        k_scr[n_ctx:, 0:LANES] = knl_ref[...]
        k_scr[0:n_ctx, LANES:] = kpc_ref[...].astype(BF16)
        k_scr[n_ctx:, LANES:] = _rope(kpl_ref[...], cos_ref[...], su_ref[...], sd_ref[...], shift).astype(BF16)
        v_scr[0:n_ctx, :] = vc_ref[...]
        v_scr[n_ctx:, :] = vl_ref[...]

    scale = (MLA_NOPE_DIM + MLA_ROPE_DIM) ** -0.5
    r0 = pl.multiple_of(i * tq, tq)
    qp = _rope(qp_ref[...].astype(F32), cos_ref[pl.ds(r0, tq), :], su_ref[pl.ds(r0, tq), :],
               sd_ref[pl.ds(r0, tq), :], shift)
    q = jnp.concatenate([(qn_ref[...].astype(F32) * scale).astype(BF16), (qp * scale).astype(BF16)], axis=-1)
    o_ref[...] = _softmax_pv(q, k_scr[...], v_scr[...]).astype(o_ref.dtype)


def _mla_attn(q, kv, down, tables, bsz, n_ctx, seq, n_heads, kpe_col, tq=256):
    cos, su, sd, shift = tables
    lat0 = bsz * n_ctx // seq
    nq = seq // tq
    kern = functools.partial(_mla_attn_kernel, n_ctx=n_ctx, tq=tq, shift=shift)
    tab = pl.BlockSpec((seq, LANES), lambda b, h, i: (0, 0))
    return pl.pallas_call(
        kern,
        grid=(bsz, n_heads, nq),
        in_specs=[pl.BlockSpec((tq, LANES), lambda b, h, i: (b * nq + i, h)),
                  pl.BlockSpec((tq, LANES), lambda b, h, i: (b * nq + i, n_heads + h)),
                  pl.BlockSpec((n_ctx, LANES), lambda b, h, i: (b, 2 * h)),
                  pl.BlockSpec((seq, LANES), lambda b, h, i: (lat0 + b, 2 * h)),
                  pl.BlockSpec((n_ctx, LANES), lambda b, h, i: (b, kpe_col)),
                  pl.BlockSpec((seq, LANES), lambda b, h, i: (lat0 + b, kpe_col)),
                  pl.BlockSpec((n_ctx, LANES), lambda b, h, i: (b, 2 * h + 1)),
                  pl.BlockSpec((seq, LANES), lambda b, h, i: (lat0 + b, 2 * h + 1)),
                  tab, tab, tab],
        out_specs=pl.BlockSpec((tq, LANES), lambda b, h, i: (b * nq + i, h)),
        out_shape=jax.ShapeDtypeStruct((bsz * seq, n_heads * LANES), BF16),
        scratch_shapes=[pltpu.VMEM((n_ctx + seq, 2 * LANES), BF16), pltpu.VMEM((n_ctx + seq, LANES), BF16)],
        compiler_params=_cparams(("arbitrary", "arbitrary", "arbitrary")),
        name="mla_attn",
    )(q, q, kv, kv, down, down, kv, kv, cos, su, sd)


def _layer_norm(z, g, b):
    zc = z - jnp.mean(z, axis=-1, keepdims=True)
    var = jnp.mean(zc * zc, axis=-1, keepdims=True)
    return zc * lax.rsqrt(var + 1e-5) * g + b


def _resln_kernel(*refs, n_act, alpha):
    x_ref, gate_ref = refs[0], refs[1]
    acts = refs[2:2 + n_act]
    ws = refs[2 + n_act:2 + 2 * n_act]
    lng_ref, lnb_ref, o_ref = refs[2 + 2 * n_act:]
    y = _dot(acts[0][...], ws[0][...])
    for a_ref, w_ref in zip(acts[1:], ws[1:]):
        y = y + _dot(a_ref[...], w_ref[...])
    z = alpha * x_ref[...] + gate_ref[0] * y
    o_ref[...] = _layer_norm(z, lng_ref[...], lnb_ref[...])


def _resln(x, mod, gate_idx, acts, ws, ln_g, ln_b, alpha, geom, row_off_blocks, n_rows, act_off_blocks, tm=512):
    d = x.shape[1]
    row = functools.partial(_mod_row, tm=tm, **geom)
    n_act = len(acts)
    in_specs = [pl.BlockSpec((tm, d), lambda i: (i + row_off_blocks, 0)),
                pl.BlockSpec((1, 1, d), lambda i: (row(i + row_off_blocks), 0, gate_idx))]
    in_specs += [pl.BlockSpec((tm, a.shape[1]), lambda i: (i + act_off_blocks, 0)) for a in acts]
    in_specs += [pl.BlockSpec(w.shape, lambda i: (0, 0)) for w in ws]
    in_specs += [pl.BlockSpec((1, d), lambda i: (0, 0))] * 2
    return pl.pallas_call(
        functools.partial(_resln_kernel, n_act=n_act, alpha=alpha),
        grid=(n_rows // tm,),
        in_specs=in_specs,
        out_specs=pl.BlockSpec((tm, d), lambda i: (i, 0)),
        out_shape=jax.ShapeDtypeStruct((n_rows, d), F32),
        compiler_params=_cparams(("arbitrary",)),
        name="proj_resln",
    )(x, mod, *acts, *ws, ln_g.reshape(1, d), ln_b.reshape(1, d))


def _route_kernel(x_ref, sc_ref, sh_ref, rw_ref, h_ref, s_ref):
    h = x_ref[...] * (1.0 + sc_ref[0]) + sh_ref[0]
    h_ref[...] = h.astype(h_ref.dtype)
    s_ref[...] = _sigmoid(_dot_hi(h, rw_ref[...]))


def _route(x, mod, sc_idx, sh_idx, router_w, geom, tm=512):
    t, d = x.shape
    e = router_w.shape[1]
    row = functools.partial(_mod_row, tm=tm, **geom)
    return pl.pallas_call(
        _route_kernel,
        grid=(t // tm,),
        in_specs=[pl.BlockSpec((tm, d), lambda i: (i, 0)),
                  pl.BlockSpec((1, 1, d), lambda i: (row(i), 0, sc_idx)),
                  pl.BlockSpec((1, 1, d), lambda i: (row(i), 0, sh_idx)),
                  pl.BlockSpec((d, e), lambda i: (0, 0))],
        out_specs=[pl.BlockSpec((tm, d), lambda i: (i, 0)),
                   pl.BlockSpec((tm, e), lambda i: (i, 0))],
        out_shape=[jax.ShapeDtypeStruct((t, d), BF16), jax.ShapeDtypeStruct((t, e), F32)],
        compiler_params=_cparams(("arbitrary",)),
        name="moe_route",
    )(x, mod, mod, router_w)


def _select_experts(scores, router_bias):
    t = scores.shape[0]
    biased = scores + router_bias.astype(F32)
    grouped = biased.reshape(t, N_GROUPS, N_EXPERTS // N_GROUPS)
    group_score = lax.top_k(grouped, 2)[0].sum(axis=-1)
    _, top_groups = lax.top_k(group_score, TOPK_GROUPS)
    group_ok = (top_groups[..., None] == jnp.arange(N_GROUPS)).any(axis=1)
    expert_ok = jnp.repeat(group_ok, N_EXPERTS // N_GROUPS, axis=-1)
    _, top_e = lax.top_k(jnp.where(expert_ok, biased, -jnp.inf), TOP_K)
    gate = jnp.take_along_axis(scores, top_e, axis=-1)
    gate = gate / jnp.sum(gate, axis=-1, keepdims=True) * ROUTE_SCALE
    return top_e.astype(jnp.int32), gate


def _dispatch_plan(top_e, gate):
    t = top_e.shape[0]
    n_blocks = t * TOP_K // MOE_BLOCK + N_EXPERTS
    p = n_blocks * MOE_BLOCK
    onehot = (top_e[:, :, None] == jnp.arange(N_EXPERTS, dtype=jnp.int32)).any(axis=1)
    running = jnp.cumsum(onehot.astype(jnp.int32), axis=0)
    counts = running[-1]
    padded = (counts + MOE_BLOCK - 1) // MOE_BLOCK * MOE_BLOCK
    pad_ends = jnp.cumsum(padded)
    pad_starts = pad_ends - padded
    rank = jnp.take_along_axis(running, top_e, axis=1) - 1
    dest = pad_starts[top_e] + rank
    tok = jnp.broadcast_to(jnp.arange(t, dtype=jnp.int32)[:, None], dest.shape)
    slot_tok = jnp.zeros((p,), jnp.int32).at[dest.reshape(-1)].set(tok.reshape(-1))
    slot_w = jnp.zeros((p,), F32).at[dest.reshape(-1)].set(gate.reshape(-1))
    block_e = jnp.minimum(jnp.searchsorted(pad_ends, jnp.arange(n_blocks) * MOE_BLOCK, side='right'),
                          N_EXPERTS - 1).astype(jnp.int32)
    n_used = (pad_ends[-1] // MOE_BLOCK).astype(jnp.int32).reshape(1)
    return dest, slot_tok, slot_w, block_e, n_used


def _experts_kernel(be_ref, nu_ref, x_ref, w_ref, wg_ref, wu_ref, wo_ref, o_ref):
    blk = pl.program_id(0)

    @pl.when(blk < nu_ref[0])
    def _():
        x = x_ref[...]
        gate = _dot(x, wg_ref[0])
        up = _dot(x, wu_ref[0])
        a = (gate * _sigmoid(gate) * up).astype(BF16)
        o_ref[...] = (_dot(a, wo_ref[0]) * w_ref[...]).astype(o_ref.dtype)

    @pl.when(blk >= nu_ref[0])
    def _():
        o_ref[...] = jnp.zeros_like(o_ref)


def _experts(xs, slot_w, block_e, n_used, w_in, w_out):
    p, d = xs.shape
    f = w_out.shape[1]
    n_blocks = p // MOE_BLOCK
    grid_spec = pltpu.PrefetchScalarGridSpec(
        num_scalar_prefetch=2,
        grid=(n_blocks,),
        in_specs=[pl.BlockSpec((MOE_BLOCK, d), lambda i, be, nu: (i, 0)),
                  pl.BlockSpec((MOE_BLOCK, 1), lambda i, be, nu: (i, 0)),
                  pl.BlockSpec((1, d, f), lambda i, be, nu: (be[i], 0, 0)),
                  pl.BlockSpec((1, d, f), lambda i, be, nu: (be[i], 0, 1)),
                  pl.BlockSpec((1, f, d), lambda i, be, nu: (be[i], 0, 0))],
        out_specs=pl.BlockSpec((MOE_BLOCK, d), lambda i, be, nu: (i, 0)),
    )
    return pl.pallas_call(
        _experts_kernel,
        grid_spec=grid_spec,
        out_shape=jax.ShapeDtypeStruct((p, d), BF16),
        compiler_params=_cparams(("arbitrary",)),
        name="moe_experts",
    )(block_e, n_used, xs, slot_w.reshape(p, 1), w_in, w_in, w_out)


def _ffn_out_kernel(x_ref, gate_ref, h_ref, r_ref, wg_ref, wu_ref, wo_ref, lng_ref, lnb_ref, o_ref, *, alpha):
    h = h_ref[...]
    gate = _dot(h, wg_ref[...])
    up = _dot(h, wu_ref[...])
    a = (gate * _sigmoid(gate) * up).astype(BF16)
    y = r_ref[...] + _dot(a, wo_ref[...])
    z = alpha * x_ref[...] + gate_ref[0] * y
    o_ref[...] = _layer_norm(z, lng_ref[...], lnb_ref[...])


def _ffn_out(x, mod, gate_idx, h, routed, sh_in, sh_out, ln_g, ln_b, alpha, geom, tm=512):
    t, d = x.shape
    f = sh_out.shape[0]
    row = functools.partial(_mod_row, tm=tm, **geom)
    return pl.pallas_call(
        functools.partial(_ffn_out_kernel, alpha=alpha),
        grid=(t // tm,),
        in_specs=[pl.BlockSpec((tm, d), lambda i: (i, 0)),
                  pl.BlockSpec((1, 1, d), lambda i: (row(i), 0, gate_idx)),
                  pl.BlockSpec((tm, d), lambda i: (i, 0)),
                  pl.BlockSpec((tm, d), lambda i: (i, 0)),
                  pl.BlockSpec((d, f), lambda i: (0, 0)),
                  pl.BlockSpec((d, f), lambda i: (0, 1)),
                  pl.BlockSpec((f, d), lambda i: (0, 0)),
                  pl.BlockSpec((1, d), lambda i: (0, 0)),
                  pl.BlockSpec((1, d), lambda i: (0, 0))],
        out_specs=pl.BlockSpec((tm, d), lambda i: (i, 0)),
        out_shape=jax.ShapeDtypeStruct((t, d), F32),
        compiler_params=_cparams(("arbitrary",)),
        name="ffn_out",
    )(x, mod, h, routed, sh_in, sh_in, sh_out, ln_g.reshape(1, d), ln_b.reshape(1, d))


def _moe_sublayer(x, mod, geom, router_w, router_bias, w_in, w_out, sh_in, sh_out, ln_g, ln_b, alpha):
    h, scores = _route(x, mod, 4, 3, router_w, geom)
    top_e, gate = _select_experts(scores, router_bias)
    dest, slot_tok, slot_w, block_e, n_used = _dispatch_plan(top_e, gate)
    xs = jnp.take(h, slot_tok, axis=0, mode="clip")
    ys = _experts(xs, slot_w, block_e, n_used, w_in.astype(BF16), w_out.astype(BF16))
    routed = jnp.sum(jnp.take(ys, dest, axis=0, mode="clip").astype(F32), axis=1)
    return _ffn_out(x, mod, 5, h, routed, sh_in.astype(BF16), sh_out.astype(BF16), ln_g, ln_b, alpha, geom)


def kernel(x, c, ctx, c_ctx, ada_w, ada_b, ln_mix_g, ln_mix_b, ln_ffn_g, ln_ffn_b, ab_w_in, ab_w_out, diff_lambda, diff_subln_g, gqa_q_norm_g, gqa_k_norm_g, mla_w_down, mla_q_norm_g, mla_w_uq, mla_kv_norm_g, mla_w_ukv, mla_w_o, router_w, router_bias, expert_w_in, expert_w_out, shared_w_in, shared_w_out):
    bsz, seq, d = x.shape
    n_ctx = ctx.shape[1]
    depth = ada_w.shape[0]
    assert depth == 2 and seq % GRID_W == 0 and bsz + 1 <= MOD_ROWS
    alpha = (2 * depth) ** 0.25
    n_ctx_rows = bsz * n_ctx
    n_lat_rows = bsz * seq
    geom_all = dict(n_ctx_rows=n_ctx_rows, seq=seq, ctx_row=bsz)
    geom_lat = dict(n_ctx_rows=0, seq=seq, ctx_row=bsz)

    tables64 = _rope_tables(seq // GRID_W, 64)
    tables128 = _rope_tables(seq // GRID_W, HEAD_DIM)

    cvec = jnp.zeros((MOD_ROWS, d), F32).at[:bsz].set(c).at[bsz].set(c_ctx)
    mods = _ada_mod(cvec, ada_w, ada_b)
    mods = mods.reshape(depth, MOD_ROWS, 1, 6 * d)

    xt = jnp.concatenate([ctx.reshape(n_ctx_rows, d), x.reshape(n_lat_rows, d)], axis=0)

    mod = mods[0]
    n_diff = d // (2 * HEAD_DIM)
    n_gqa = d // (2 * HEAD_DIM)
    n_gqa_kv = n_gqa // 4
    proj = _modproj(xt, mod, 1, 0, ab_w_in[0].astype(BF16), BF16, geom_all)
    lam_init = 0.8 - 0.6 * math.exp(-0.3 * 0)
    oa = _diff_attn(proj, tables64, diff_lambda[0], diff_subln_g[0], lam_init, bsz, n_ctx, seq, n_diff)
    ob = _gqa_attn(proj, tables128, gqa_q_norm_g[0], gqa_k_norm_g[0], 3 * n_diff, bsz, n_ctx, seq, n_gqa, n_gqa_kv)
    w_out = ab_w_out[0].astype(BF16)
    wa, wb = w_out[:n_diff * HEAD_DIM], w_out[n_diff * HEAD_DIM:]
    t_all = n_ctx_rows + n_lat_rows
    xt = _resln(xt, mod, 2, [oa, ob], [wa, wb], ln_mix_g[0], ln_mix_b[0], alpha, geom_all, 0, t_all, 0)
    xt = _moe_sublayer(xt, mod, geom_all, router_w[0], router_bias[0], expert_w_in[0], expert_w_out[0],
                       shared_w_in[0], shared_w_out[0], ln_ffn_g[0], ln_ffn_b[0], alpha)

    mod = mods[1]
    n_mla = d // HEAD_DIM
    qk = MLA_NOPE_DIM + MLA_ROPE_DIM
    w_down = jnp.pad(mla_w_down[0], ((0, 0), (0, LANES - MLA_ROPE_DIM))).astype(BF16)
    down = _modproj(xt, mod, 1, 0, w_down, F32, geom_all)
    w_uq = mla_w_uq[0].reshape(MLA_Q_RANK, n_mla, qk)
    w_uq_rope = jnp.pad(w_uq[:, :, MLA_NOPE_DIM:], ((0, 0), (0, 0), (0, LANES - MLA_ROPE_DIM)))
    w_uq = jnp.concatenate([w_uq[:, :, :MLA_NOPE_DIM].reshape(MLA_Q_RANK, -1),
                            w_uq_rope.reshape(MLA_Q_RANK, -1)], axis=1).astype(BF16)
    lat_blocks = n_ctx_rows // 1024
    q = _normproj(down, 0, mla_q_norm_g[0], w_uq, lat_blocks, n_lat_rows)
    kv = _normproj(down, 1, mla_kv_norm_g[0], mla_w_ukv[0].astype(BF16), 0, t_all)
    kpe_col = (MLA_Q_RANK + MLA_KV_RANK) // LANES
    o = _mla_attn(q, kv, down, tables64, bsz, n_ctx, seq, n_mla, kpe_col)
    xl = _resln(xt, mod, 2, [o], [mla_w_o[0].astype(BF16)], ln_mix_g[1], ln_mix_b[1], alpha, geom_all,
                n_ctx_rows // 512, n_lat_rows, 0)
    xl = _moe_sublayer(xl, mod, geom_lat, router_w[1], router_bias[1], expert_w_in[1], expert_w_out[1],
                       shared_w_in[1], shared_w_out[1], ln_ffn_g[1], ln_ffn_b[1], alpha)
    return xl.reshape(bsz, seq, d)
```

```python
import functools
import math

import numpy as np
import jax
import jax.numpy as jnp
from jax import lax
from jax.experimental import pallas as pl
from jax.experimental.pallas import tpu as pltpu

F32 = jnp.float32
BF16 = jnp.bfloat16

GRID_W = 64
ROPE_THETA = 10000.0
HEAD_DIM = 128
DIFF_QK_DIM = 64
N_EXPERTS = 64
EXPERT_DIM = 512
TOP_K = 8
N_GROUPS = 8
TOPK_GROUPS = 4
ROUTE_SCALE = 2.5
MLA_Q_RANK = 512
MLA_KV_RANK = 512
MLA_NOPE_DIM = 128
MLA_ROPE_DIM = 64
MLA_V_DIM = 128
LANES = 128

MOD_ROWS = 16
VMEM_LIMIT = 56 * 1024 * 1024
MOE_BLOCK = 256


def _cparams(sem):
    return pltpu.CompilerParams(dimension_semantics=sem, vmem_limit_bytes=VMEM_LIMIT)


def _dot(a, b):
    return jnp.dot(a, b, preferred_element_type=F32)


def _dot_nt(a, b):
    return lax.dot_general(a, b, (((1,), (1,)), ((), ())), preferred_element_type=F32)


def _dot_hi(a, b):
    return lax.dot_general(a, b, (((1,), (0,)), ((), ())), precision=lax.Precision.HIGHEST,
                           preferred_element_type=F32)


def _sigmoid(x):
    return 1.0 / (1.0 + jnp.exp(-x))


def _rope(x, cos, sin_up, sin_dn, shift):
    return (x * cos + pltpu.roll(x, LANES - shift, 1) * sin_up + pltpu.roll(x, shift, 1) * sin_dn)


def _rope_tables(rows, rot_dim):
    t = jnp.arange(rows * GRID_W)
    row = (t // GRID_W).astype(F32)
    col = (t % GRID_W).astype(F32)
    axis_dim = rot_dim // 2
    quarter = rot_dim // 4
    inv_freq = 1.0 / (ROPE_THETA ** (jnp.arange(0, axis_dim, 2, dtype=F32) / axis_dim))
    lane = np.arange(LANES) % rot_dim
    is_col = lane >= axis_dim
    within = lane % axis_dim
    first = within < quarter
    freq = within % quarter
    ang = jnp.where(jnp.asarray(is_col)[None, :], col[:, None], row[:, None]) * inv_freq[freq][None, :]
    cos = jnp.cos(ang).astype(F32)
    sin = jnp.sin(ang).astype(F32)
    first = jnp.asarray(first)[None, :]
    sin_up = jnp.where(first, -sin, 0.0)
    sin_dn = jnp.where(first, 0.0, sin)
    return cos, sin_up, sin_dn, quarter


def _softmax_rows(s):
    m = jnp.max(s, axis=-1, keepdims=True)
    e = jnp.exp(s - m)
    return e * (1.0 / jnp.sum(e, axis=-1, keepdims=True))


def _rms(x, g, eps):
    return x * lax.rsqrt(jnp.mean(x * x, axis=-1, keepdims=True) + eps) * g


def _ada_kernel(c_ref, w_ref, b_ref, o_ref):
    c = c_ref[...]
    act = c * _sigmoid(c)
    o_ref[0] = _dot_hi(act, w_ref[0]) + b_ref[0]


def _ada_mod(cvec, ada_w, ada_b):
    depth, d, n = ada_w.shape
    tn = 1536
    return pl.pallas_call(
        _ada_kernel,
        grid=(depth, n // tn),
        in_specs=[pl.BlockSpec((MOD_ROWS, d), lambda l, j: (0, 0)),
                  pl.BlockSpec((1, d, tn), lambda l, j: (l, 0, j)),
                  pl.BlockSpec((1, 1, tn), lambda l, j: (l, 0, j))],
        out_specs=pl.BlockSpec((1, MOD_ROWS, tn), lambda l, j: (l, 0, j)),
        out_shape=jax.ShapeDtypeStruct((depth, MOD_ROWS, n), F32),
        compiler_params=_cparams(("arbitrary", "arbitrary")),
        name="ada_mod",
    )(cvec, ada_w, ada_b.reshape(depth, 1, n))


def _mod_row(i, tm, n_ctx_rows, seq, ctx_row):
    r0 = i * tm
    return jnp.where(r0 < n_ctx_rows, ctx_row, (r0 - n_ctx_rows) // seq)


def _modproj_kernel(x_ref, sc_ref, sh_ref, w_ref, o_ref, h_ref):
    @pl.when(pl.program_id(1) == 0)
    def _():
        h_ref[...] = (x_ref[...] * (1.0 + sc_ref[0]) + sh_ref[0]).astype(h_ref.dtype)

    o_ref[...] = _dot(h_ref[...], w_ref[...]).astype(o_ref.dtype)


def _modproj(x, mod, sc_idx, sh_idx, w, out_dtype, geom, tm=1024, tn=512):
    t, d = x.shape
    n = w.shape[1]
    tn = tn if n % tn == 0 else n
    row = functools.partial(_mod_row, tm=tm, **geom)
    return pl.pallas_call(
        _modproj_kernel,
        grid=(t // tm, n // tn),
        in_specs=[pl.BlockSpec((tm, d), lambda i, j: (i, 0)),
                  pl.BlockSpec((1, 1, d), lambda i, j: (row(i), 0, sc_idx)),
                  pl.BlockSpec((1, 1, d), lambda i, j: (row(i), 0, sh_idx)),
                  pl.BlockSpec((d, tn), lambda i, j: (0, j))],
        out_specs=pl.BlockSpec((tm, tn), lambda i, j: (i, j)),
        out_shape=jax.ShapeDtypeStruct((t, n), out_dtype),
        scratch_shapes=[pltpu.VMEM((tm, d), BF16)],
        compiler_params=_cparams(("arbitrary", "arbitrary")),
        name="modproj",
    )(x, mod, mod, w)


def _normproj_kernel(x_ref, g_ref, w_ref, o_ref, h_ref):
    @pl.when(pl.program_id(1) == 0)
    def _():
        h_ref[...] = _rms(x_ref[...], g_ref[...], 1e-6).astype(h_ref.dtype)

    o_ref[...] = _dot(h_ref[...], w_ref[...]).astype(o_ref.dtype)


def _normproj(x, col_block, g, w, row_off_blocks, n_rows, tm=1024, tn=1024):
    k = w.shape[0]
    n = w.shape[1]
    return pl.pallas_call(
        _normproj_kernel,
        grid=(n_rows // tm, n // tn),
        in_specs=[pl.BlockSpec((tm, k), lambda i, j: (i + row_off_blocks, col_block)),
                  pl.BlockSpec((1, k), lambda i, j: (0, 0)),
                  pl.BlockSpec((k, tn), lambda i, j: (0, j))],
        out_specs=pl.BlockSpec((tm, tn), lambda i, j: (i, j)),
        out_shape=jax.ShapeDtypeStruct((n_rows, n), BF16),
        scratch_shapes=[pltpu.VMEM((tm, k), BF16)],
        compiler_params=_cparams(("arbitrary", "arbitrary")),
        name="normproj",
    )(x, g.reshape(1, k), w)


def _diff_lambda(lam_ref, lam_init):
    lp = lam_ref[...]
    s01 = jnp.sum(lp[0:1] * lp[1:2], axis=-1, keepdims=True)
    s23 = jnp.sum(lp[2:3] * lp[3:4], axis=-1, keepdims=True)
    return jnp.exp(s01) - jnp.exp(s23) + lam_init


def _diff_attend(q, k, v, lam, g, lam_init):
    lane = lax.broadcasted_iota(jnp.int32, q.shape, 1)
    qs = q * (DIFF_QK_DIM ** -0.5)
    q1 = jnp.where(lane < DIFF_QK_DIM, qs, 0.0).astype(BF16)
    q2 = jnp.where(lane >= DIFF_QK_DIM, qs, 0.0).astype(BF16)
    p1 = _softmax_rows(_dot_nt(q1, k))
    p2 = _softmax_rows(_dot_nt(q2, k))
    a = (p1 - lam * p2).astype(BF16)
    o = _dot(a, v)
    return _rms(o, g, 1e-5) * (1.0 - lam_init)


def _diff_attn_kernel(q_ref, kc_ref, kl_ref, vc_ref, vl_ref, cos_ref, su_ref, sd_ref, lam_ref, g_ref,
                      o_ref, k_scr, v_scr, *, n_ctx, tq, shift, lam_init):
    i = pl.program_id(2)

    @pl.when(i == 0)
    def _prep():
        k_scr[0:n_ctx, :] = kc_ref[...]
        kl = kl_ref[...].astype(F32)
        k_scr[n_ctx:, :] = _rope(kl, cos_ref[...], su_ref[...], sd_ref[...], shift).astype(BF16)
        v_scr[0:n_ctx, :] = vc_ref[...]
        v_scr[n_ctx:, :] = vl_ref[...]

    lam = _diff_lambda(lam_ref, lam_init)
    g = g_ref[...]

    @pl.when(i == 0)
    def _ctx_queries():
        q = q_ref[...].astype(F32)
        o_ref[...] = _diff_attend(q, k_scr[0:n_ctx, :], v_scr[0:n_ctx, :], lam, g, lam_init).astype(o_ref.dtype)

    @pl.when(i > 0)
    def _lat_queries():
        r0 = pl.multiple_of((i - 1) * tq, tq)
        q = _rope(q_ref[...].astype(F32), cos_ref[pl.ds(r0, tq), :], su_ref[pl.ds(r0, tq), :],
                  sd_ref[pl.ds(r0, tq), :], shift)
        o_ref[...] = _diff_attend(q, k_scr[...], v_scr[...], lam, g, lam_init).astype(o_ref.dtype)


def _q_block(b, i, bsz, n_ctx, seq, tq):
    return jnp.where(i == 0, b * (n_ctx // tq), (bsz * n_ctx + b * seq) // tq + i - 1)


def _diff_attn(proj, tables, lam_params, subln_g, lam_init, bsz, n_ctx, seq, n_heads, tq=256):
    t = proj.shape[0]
    cos, su, sd, shift = tables
    qb = functools.partial(_q_block, bsz=bsz, n_ctx=n_ctx, seq=seq, tq=tq)
    lat0 = bsz * n_ctx // seq
    kern = functools.partial(_diff_attn_kernel, n_ctx=n_ctx, tq=tq, shift=shift, lam_init=lam_init)
    tab = pl.BlockSpec((seq, LANES), lambda b, h, i: (0, 0))
    return pl.pallas_call(
        kern,
        grid=(bsz, n_heads, 1 + seq // tq),
        in_specs=[pl.BlockSpec((tq, LANES), lambda b, h, i: (qb(b, i), h)),
                  pl.BlockSpec((n_ctx, LANES), lambda b, h, i: (b, n_heads + h)),
                  pl.BlockSpec((seq, LANES), lambda b, h, i: (lat0 + b, n_heads + h)),
                  pl.BlockSpec((n_ctx, LANES), lambda b, h, i: (b, 2 * n_heads + h)),
                  pl.BlockSpec((seq, LANES), lambda b, h, i: (lat0 + b, 2 * n_heads + h)),
                  tab, tab, tab,
                  pl.BlockSpec((4, DIFF_QK_DIM), lambda b, h, i: (0, 0)),
                  pl.BlockSpec((1, LANES), lambda b, h, i: (0, 0))],
        out_specs=pl.BlockSpec((tq, LANES), lambda b, h, i: (qb(b, i), h)),
        out_shape=jax.ShapeDtypeStruct((t, n_heads * LANES), BF16),
        scratch_shapes=[pltpu.VMEM((n_ctx + seq, LANES), BF16), pltpu.VMEM((n_ctx + seq, LANES), BF16)],
        compiler_params=_cparams(("arbitrary", "arbitrary", "arbitrary")),
        name="diff_attn",
    )(proj, proj, proj, proj, proj, cos, su, sd, lam_params, subln_g.reshape(1, LANES))


def _softmax_pv(q, k, v):
    p = _softmax_rows(_dot_nt(q, k)).astype(BF16)
    return _dot(p, v)


def _gqa_attn_kernel(q_ref, kc_ref, kl_ref, vc_ref, vl_ref, cos_ref, su_ref, sd_ref, qg_ref, kg_ref,
                     o_ref, k_scr, v_scr, *, n_ctx, tq, shift):
    g = pl.program_id(2)
    i = pl.program_id(3)

    @pl.when((g == 0) & (i == 0))
    def _prep():
        kg = kg_ref[...]
        k_scr[0:n_ctx, :] = _rms(kc_ref[...].astype(F32), kg, 1e-6).astype(BF16)
        kl = _rms(kl_ref[...].astype(F32), kg, 1e-6)
        k_scr[n_ctx:, :] = _rope(kl, cos_ref[...], su_ref[...], sd_ref[...], shift).astype(BF16)
        v_scr[0:n_ctx, :] = vc_ref[...]
        v_scr[n_ctx:, :] = vl_ref[...]

    scale = HEAD_DIM ** -0.5
    qn = _rms(q_ref[...].astype(F32), qg_ref[...], 1e-6)

    @pl.when(i == 0)
    def _ctx_queries():
        q = (qn * scale).astype(BF16)
        o_ref[...] = _softmax_pv(q, k_scr[0:n_ctx, :], v_scr[0:n_ctx, :]).astype(o_ref.dtype)

    @pl.when(i > 0)
    def _lat_queries():
        r0 = pl.multiple_of((i - 1) * tq, tq)
        q = _rope(qn, cos_ref[pl.ds(r0, tq), :], su_ref[pl.ds(r0, tq), :], sd_ref[pl.ds(r0, tq), :], shift)
        q = (q * scale).astype(BF16)
        o_ref[...] = _softmax_pv(q, k_scr[...], v_scr[...]).astype(o_ref.dtype)


def _gqa_attn(proj, tables, q_norm_g, k_norm_g, col0, bsz, n_ctx, seq, n_heads, n_kv, tq=256):
    t = proj.shape[0]
    cos, su, sd, shift = tables
    grp = n_heads // n_kv
    qb = functools.partial(_q_block, bsz=bsz, n_ctx=n_ctx, seq=seq, tq=tq)
    lat0 = bsz * n_ctx // seq
    kcol = col0 + n_heads
    vcol = kcol + n_kv
    kern = functools.partial(_gqa_attn_kernel, n_ctx=n_ctx, tq=tq, shift=shift)
    tab = pl.BlockSpec((seq, LANES), lambda b, h, g, i: (0, 0))
    vec = pl.BlockSpec((1, LANES), lambda b, h, g, i: (0, 0))
    return pl.pallas_call(
        kern,
        grid=(bsz, n_kv, grp, 1 + seq // tq),
        in_specs=[pl.BlockSpec((tq, LANES), lambda b, h, g, i: (qb(b, i), col0 + h * grp + g)),
                  pl.BlockSpec((n_ctx, LANES), lambda b, h, g, i: (b, kcol + h)),
                  pl.BlockSpec((seq, LANES), lambda b, h, g, i: (lat0 + b, kcol + h)),
                  pl.BlockSpec((n_ctx, LANES), lambda b, h, g, i: (b, vcol + h)),
                  pl.BlockSpec((seq, LANES), lambda b, h, g, i: (lat0 + b, vcol + h)),
                  tab, tab, tab, vec, vec],
        out_specs=pl.BlockSpec((tq, LANES), lambda b, h, g, i: (qb(b, i), h * grp + g)),
        out_shape=jax.ShapeDtypeStruct((t, n_heads * LANES), BF16),
        scratch_shapes=[pltpu.VMEM((n_ctx + seq, LANES), BF16), pltpu.VMEM((n_ctx + seq, LANES), BF16)],
        compiler_params=_cparams(("arbitrary", "arbitrary", "arbitrary", "arbitrary")),
        name="gqa_attn",
    )(proj, proj, proj, proj, proj, cos, su, sd, q_norm_g.reshape(1, LANES), k_norm_g.reshape(1, LANES))


def _mla_attn_kernel(qn_ref, qp_ref, knc_ref, knl_ref, kpc_ref, kpl_ref, vc_ref, vl_ref,
                     cos_ref, su_ref, sd_ref, o_ref, k_scr, v_scr, *, n_ctx, tq, shift):
    i = pl.program_id(2)

    @pl.when(i == 0)
    def _prep():
        k_scr[0:n_ctx, 0:LANES] = knc_ref[...]
        k_scr[n_ctx:, 0:LANES] = knl_ref[...]
        k_scr[0:n_ctx, LANES:] = kpc_ref[...].astype(BF16)
        k_scr[n_ctx:, LANES:] = _rope(kpl_ref[...], cos_ref[...], su_ref[...], sd_ref[...], shift).astype(BF16)
        v_scr[0:n_ctx, :] = vc_ref[...]
        v_scr[n_ctx:, :] = vl_ref[...]

    scale = (MLA_NOPE_DIM + MLA_ROPE_DIM) ** -0.5
    r0 = pl.multiple_of(i * tq, tq)
    qp = _rope(qp_ref[...].astype(F32), cos_ref[pl.ds(r0, tq), :], su_ref[pl.ds(r0, tq), :],
               sd_ref[pl.ds(r0, tq), :], shift)
    q = jnp.concatenate([(qn_ref[...].astype(F32) * scale).astype(BF16), (qp * scale).astype(BF16)], axis=-1)
    o_ref[...] = _softmax_pv(q, k_scr[...], v_scr[...]).astype(o_ref.dtype)


def _mla_attn(q, kv, down, tables, bsz, n_ctx, seq, n_heads, kpe_col, tq=256):
    cos, su, sd, shift = tables
    lat0 = bsz * n_ctx // seq
    nq = seq // tq
    kern = functools.partial(_mla_attn_kernel, n_ctx=n_ctx, tq=tq, shift=shift)
    tab = pl.BlockSpec((seq, LANES), lambda b, h, i: (0, 0))
    return pl.pallas_call(
        kern,
        grid=(bsz, n_heads, nq),
        in_specs=[pl.BlockSpec((tq, LANES), lambda b, h, i: (b * nq + i, h)),
                  pl.BlockSpec((tq, LANES), lambda b, h, i: (b * nq + i, n_heads + h)),
                  pl.BlockSpec((n_ctx, LANES), lambda b, h, i: (b, 2 * h)),
                  pl.BlockSpec((seq, LANES), lambda b, h, i: (lat0 + b, 2 * h)),
                  pl.BlockSpec((n_ctx, LANES), lambda b, h, i: (b, kpe_col)),
                  pl.BlockSpec((seq, LANES), lambda b, h, i: (lat0 + b, kpe_col)),
                  pl.BlockSpec((n_ctx, LANES), lambda b, h, i: (b, 2 * h + 1)),
                  pl.BlockSpec((seq, LANES), lambda b, h, i: (lat0 + b, 2 * h + 1)),
                  tab, tab, tab],
        out_specs=pl.BlockSpec((tq, LANES), lambda b, h, i: (b * nq + i, h)),
        out_shape=jax.ShapeDtypeStruct((bsz * seq, n_heads * LANES), BF16),
        scratch_shapes=[pltpu.VMEM((n_ctx + seq, 2 * LANES), BF16), pltpu.VMEM((n_ctx + seq, LANES), BF16)],
        compiler_params=_cparams(("arbitrary", "arbitrary", "arbitrary")),
        name="mla_attn",
    )(q, q, kv, kv, down, down, kv, kv, cos, su, sd)


def _layer_norm(z, g, b):
    zc = z - jnp.mean(z, axis=-1, keepdims=True)
    var = jnp.mean(zc * zc, axis=-1, keepdims=True)
    return zc * lax.rsqrt(var + 1e-5) * g + b


def _resln_kernel(*refs, n_act, alpha):
    x_ref, gate_ref = refs[0], refs[1]
    acts = refs[2:2 + n_act]
    ws = refs[2 + n_act:2 + 2 * n_act]
    lng_ref, lnb_ref, o_ref = refs[2 + 2 * n_act:]
    y = _dot(acts[0][...], ws[0][...])
    for a_ref, w_ref in zip(acts[1:], ws[1:]):
        y = y + _dot(a_ref[...], w_ref[...])
    z = alpha * x_ref[...] + gate_ref[0] * y
    o_ref[...] = _layer_norm(z, lng_ref[...], lnb_ref[...])


def _resln(x, mod, gate_idx, acts, ws, ln_g, ln_b, alpha, geom, row_off_blocks, n_rows, act_off_blocks, tm=512):
    d = x.shape[1]
    row = functools.partial(_mod_row, tm=tm, **geom)
    n_act = len(acts)
    in_specs = [pl.BlockSpec((tm, d), lambda i: (i + row_off_blocks, 0)),
                pl.BlockSpec((1, 1, d), lambda i: (row(i + row_off_blocks), 0, gate_idx))]
    in_specs += [pl.BlockSpec((tm, a.shape[1]), lambda i: (i + act_off_blocks, 0)) for a in acts]
    in_specs += [pl.BlockSpec(w.shape, lambda i: (0, 0)) for w in ws]
    in_specs += [pl.BlockSpec((1, d), lambda i: (0, 0))] * 2
    return pl.pallas_call(
        functools.partial(_resln_kernel, n_act=n_act, alpha=alpha),
        grid=(n_rows // tm,),
        in_specs=in_specs,
        out_specs=pl.BlockSpec((tm, d), lambda i: (i, 0)),
        out_shape=jax.ShapeDtypeStruct((n_rows, d), F32),
        compiler_params=_cparams(("arbitrary",)),
        name="proj_resln",
    )(x, mod, *acts, *ws, ln_g.reshape(1, d), ln_b.reshape(1, d))


def _pick_first_max(cur, idx, axes, sentinel):
    m = cur
    for ax in axes:
        m = jnp.max(m, axis=ax, keepdims=True)
    first = jnp.where(cur == m, idx, sentinel)
    for ax in axes:
        first = jnp.min(first, axis=ax, keepdims=True)
    return m, first


def _route_kernel(x_ref, sc_ref, sh_ref, rwt_ref, bias_ref, tri_ref, h_ref, e_ref, g_ref, r_ref, cnt_ref, run_ref):
    tm = x_ref.shape[0]
    n_members = N_EXPERTS // N_GROUPS

    @pl.when(pl.program_id(0) == 0)
    def _():
        run_ref[...] = jnp.zeros_like(run_ref)

    h = x_ref[...] * (1.0 + sc_ref[0]) + sh_ref[0]
    h_ref[...] = h.astype(h_ref.dtype)
    logits = lax.dot_general(rwt_ref[...], h, (((1,), (1,)), ((), ())), precision=lax.Precision.HIGHEST,
                             preferred_element_type=F32)
    scores = _sigmoid(logits).reshape(N_GROUPS, n_members, tm)
    biased = scores + bias_ref[...]
    neg = -jnp.inf
    member = lax.broadcasted_iota(jnp.int32, biased.shape, 1).astype(F32)
    group = lax.broadcasted_iota(jnp.int32, biased.shape, 0).astype(F32)
    expert = group * n_members + member

    m1, first = _pick_first_max(biased, member, (1,), float(n_members))
    m2 = jnp.max(jnp.where(member == first, neg, biased), axis=1, keepdims=True)
    gscore = m1 + m2
    gidx = lax.broadcasted_iota(jnp.int32, gscore.shape, 0).astype(F32)
    group_ok = jnp.zeros(gscore.shape, jnp.bool_)
    for _ in range(TOPK_GROUPS):
        _, first = _pick_first_max(gscore, gidx, (0,), float(N_GROUPS))
        pick = gidx == first
        group_ok = group_ok | pick
        gscore = jnp.where(pick, neg, gscore)

    cur = jnp.where(group_ok, biased, neg)
    chosen = jnp.zeros(biased.shape, jnp.bool_)
    top_e, gates = [], []
    for _ in range(TOP_K):
        _, first = _pick_first_max(cur, expert, (0, 1), float(N_EXPERTS))
        pick = expert == first
        chosen = chosen | pick
        cur = jnp.where(pick, neg, cur)
        top_e.append(first)
        gates.append(jnp.sum(jnp.sum(jnp.where(pick, scores, 0.0), axis=0, keepdims=True), axis=1, keepdims=True))
    gsum = gates[0]
    for gk in gates[1:]:
        gsum = gsum + gk
    norm = ROUTE_SCALE / gsum

    chosen2d = jnp.where(chosen, 1.0, 0.0).reshape(N_EXPERTS, tm)
    before = _dot(chosen2d.astype(BF16), tri_ref[...])
    rank = (run_ref[...] + before).reshape(N_GROUPS, n_members, tm)
    run_ref[...] = run_ref[...] + jnp.sum(chosen2d, axis=1, keepdims=True)
    cnt_ref[...] = run_ref[...].astype(jnp.int32)
    for k in range(TOP_K):
        pick = expert == top_e[k]
        rk = jnp.sum(jnp.sum(jnp.where(pick, rank, 0.0), axis=0, keepdims=True), axis=1, keepdims=True)
        e_ref[k:k + 1, :] = top_e[k].reshape(1, tm).astype(jnp.int32)
        g_ref[k:k + 1, :] = (gates[k] * norm).reshape(1, tm)
        r_ref[k:k + 1, :] = rk.reshape(1, tm).astype(jnp.int32)


def _route(x, mod, sc_idx, sh_idx, router_w, router_bias, geom, tm=512):
    t, d = x.shape
    row = functools.partial(_mod_row, tm=tm, **geom)
    tri = (np.arange(tm)[:, None] < np.arange(tm)[None, :]).astype(np.float32)
    kt = pl.BlockSpec((TOP_K, tm), lambda i: (0, i))
    return pl.pallas_call(
        _route_kernel,
        grid=(t // tm,),
        in_specs=[pl.BlockSpec((tm, d), lambda i: (i, 0)),
                  pl.BlockSpec((1, 1, d), lambda i: (row(i), 0, sc_idx)),
                  pl.BlockSpec((1, 1, d), lambda i: (row(i), 0, sh_idx)),
                  pl.BlockSpec((N_EXPERTS, d), lambda i: (0, 0)),
                  pl.BlockSpec((N_GROUPS, N_EXPERTS // N_GROUPS, 1), lambda i: (0, 0, 0)),
                  pl.BlockSpec((tm, tm), lambda i: (0, 0))],
        out_specs=[pl.BlockSpec((tm, d), lambda i: (i, 0)), kt, kt, kt,
                   pl.BlockSpec((N_EXPERTS, 1), lambda i: (0, 0))],
        out_shape=[jax.ShapeDtypeStruct((t, d), BF16),
                   jax.ShapeDtypeStruct((TOP_K, t), jnp.int32),
                   jax.ShapeDtypeStruct((TOP_K, t), F32),
                   jax.ShapeDtypeStruct((TOP_K, t), jnp.int32),
                   jax.ShapeDtypeStruct((N_EXPERTS, 1), jnp.int32)],
        scratch_shapes=[pltpu.VMEM((N_EXPERTS, 1), F32)],
        compiler_params=_cparams(("arbitrary",)),
        name="moe_route",
    )(x, mod, mod, router_w.T, router_bias.astype(F32).reshape(N_GROUPS, N_EXPERTS // N_GROUPS, 1),
      jnp.asarray(tri, BF16))


def _dispatch_plan(top_e, rank, counts):
    t = top_e.shape[1]
    n_blocks = t * TOP_K // MOE_BLOCK + N_EXPERTS
    p = n_blocks * MOE_BLOCK
    counts = counts.reshape(N_EXPERTS)
    padded = (counts + MOE_BLOCK - 1) // MOE_BLOCK * MOE_BLOCK
    pad_ends = jnp.cumsum(padded)
    pad_starts = pad_ends - padded
    dest = pad_starts[top_e] + rank
    tok = jnp.broadcast_to(jnp.arange(t, dtype=jnp.int32)[None, :], dest.shape)
    slot_tok = jnp.zeros((p,), jnp.int32).at[dest.reshape(-1)].set(tok.reshape(-1))
    block_start = jnp.arange(n_blocks, dtype=jnp.int32) * MOE_BLOCK
    block_e = jnp.minimum(jnp.sum(pad_ends[None, :] <= block_start[:, None], axis=1), N_EXPERTS - 1)
    n_used = (pad_ends[-1] // MOE_BLOCK).astype(jnp.int32).reshape(1)
    return dest, slot_tok, block_e.astype(jnp.int32), n_used


def _experts_kernel(be_ref, nu_ref, x_ref, wg_ref, wu_ref, wo_ref, o_ref, wg_scr, wu_scr, wo_scr):
    blk = pl.program_id(0)
    prev = be_ref[jnp.maximum(blk - 1, 0)]

    @pl.when((blk == 0) | (be_ref[blk] != prev))
    def _():
        wg_scr[...] = wg_ref[0].astype(BF16)
        wu_scr[...] = wu_ref[0].astype(BF16)
        wo_scr[...] = wo_ref[0].astype(BF16)

    @pl.when(blk < nu_ref[0])
    def _():
        x = x_ref[...]
        gate = _dot(x, wg_scr[...])
        up = _dot(x, wu_scr[...])
        a = (gate * _sigmoid(gate) * up).astype(BF16)
        o_ref[...] = _dot(a, wo_scr[...]).astype(o_ref.dtype)

    @pl.when(blk >= nu_ref[0])
    def _():
        o_ref[...] = jnp.zeros_like(o_ref)


def _experts(xs, block_e, n_used, w_in, w_out):
    p, d = xs.shape
    f = w_out.shape[1]
    n_blocks = p // MOE_BLOCK
    grid_spec = pltpu.PrefetchScalarGridSpec(
        num_scalar_prefetch=2,
        grid=(n_blocks,),
        in_specs=[pl.BlockSpec((MOE_BLOCK, d), lambda i, be, nu: (i, 0)),
                  pl.BlockSpec((1, d, f), lambda i, be, nu: (be[i], 0, 0)),
                  pl.BlockSpec((1, d, f), lambda i, be, nu: (be[i], 0, 1)),
                  pl.BlockSpec((1, f, d), lambda i, be, nu: (be[i], 0, 0))],
        out_specs=pl.BlockSpec((MOE_BLOCK, d), lambda i, be, nu: (i, 0)),
        scratch_shapes=[pltpu.VMEM((d, f), BF16), pltpu.VMEM((d, f), BF16), pltpu.VMEM((f, d), BF16)],
    )
    return pl.pallas_call(
        _experts_kernel,
        grid_spec=grid_spec,
        out_shape=jax.ShapeDtypeStruct((p, d), BF16),
        compiler_params=_cparams(("arbitrary",)),
        name="moe_experts",
    )(block_e, n_used, xs, w_in, w_in, w_out)


def _ffn_out_kernel(x_ref, gate_ref, h_ref, r_ref, wg_ref, wu_ref, wo_ref, lng_ref, lnb_ref, o_ref, *, alpha):
    h = h_ref[...]
    gate = _dot(h, wg_ref[...])
    up = _dot(h, wu_ref[...])
    a = (gate * _sigmoid(gate) * up).astype(BF16)
    y = r_ref[...] + _dot(a, wo_ref[...])
    z = alpha * x_ref[...] + gate_ref[0] * y
    o_ref[...] = _layer_norm(z, lng_ref[...], lnb_ref[...])


def _ffn_out(x, mod, gate_idx, h, routed, sh_in, sh_out, ln_g, ln_b, alpha, geom, tm=512):
    t, d = x.shape
    f = sh_out.shape[0]
    row = functools.partial(_mod_row, tm=tm, **geom)
    return pl.pallas_call(
        functools.partial(_ffn_out_kernel, alpha=alpha),
        grid=(t // tm,),
        in_specs=[pl.BlockSpec((tm, d), lambda i: (i, 0)),
                  pl.BlockSpec((1, 1, d), lambda i: (row(i), 0, gate_idx)),
                  pl.BlockSpec((tm, d), lambda i: (i, 0)),
                  pl.BlockSpec((tm, d), lambda i: (i, 0)),
                  pl.BlockSpec((d, f), lambda i: (0, 0)),
                  pl.BlockSpec((d, f), lambda i: (0, 1)),
                  pl.BlockSpec((f, d), lambda i: (0, 0)),
                  pl.BlockSpec((1, d), lambda i: (0, 0)),
                  pl.BlockSpec((1, d), lambda i: (0, 0))],
        out_specs=pl.BlockSpec((tm, d), lambda i: (i, 0)),
        out_shape=jax.ShapeDtypeStruct((t, d), F32),
        compiler_params=_cparams(("arbitrary",)),
        name="ffn_out",
    )(x, mod, h, routed, sh_in, sh_in, sh_out, ln_g.reshape(1, d), ln_b.reshape(1, d))


def _moe_sublayer(x, mod, geom, router_w, router_bias, w_in, w_out, sh_in, sh_out, ln_g, ln_b, alpha):
    h, top_e, gate, rank, counts = _route(x, mod, 4, 3, router_w, router_bias, geom)
    dest, slot_tok, block_e, n_used = _dispatch_plan(top_e, rank, counts)
    xs = jnp.take(h, slot_tok, axis=0, mode="clip")
    ys = _experts(xs, block_e, n_used, w_in, w_out)
    routed = jnp.sum(jnp.take(ys, dest, axis=0, mode="clip").astype(F32) * gate[:, :, None], axis=0)
    return _ffn_out(x, mod, 5, h, routed, sh_in.astype(BF16), sh_out.astype(BF16), ln_g, ln_b, alpha, geom)


def kernel(x, c, ctx, c_ctx, ada_w, ada_b, ln_mix_g, ln_mix_b, ln_ffn_g, ln_ffn_b, ab_w_in, ab_w_out, diff_lambda, diff_subln_g, gqa_q_norm_g, gqa_k_norm_g, mla_w_down, mla_q_norm_g, mla_w_uq, mla_kv_norm_g, mla_w_ukv, mla_w_o, router_w, router_bias, expert_w_in, expert_w_out, shared_w_in, shared_w_out):
    bsz, seq, d = x.shape
    n_ctx = ctx.shape[1]
    depth = ada_w.shape[0]
    assert depth == 2 and seq % GRID_W == 0 and bsz + 1 <= MOD_ROWS
    alpha = (2 * depth) ** 0.25
    n_ctx_rows = bsz * n_ctx
    n_lat_rows = bsz * seq
    geom_all = dict(n_ctx_rows=n_ctx_rows, seq=seq, ctx_row=bsz)
    geom_lat = dict(n_ctx_rows=0, seq=seq, ctx_row=bsz)

    tables64 = _rope_tables(seq // GRID_W, 64)
    tables128 = _rope_tables(seq // GRID_W, HEAD_DIM)

    cvec = jnp.zeros((MOD_ROWS, d), F32).at[:bsz].set(c).at[bsz].set(c_ctx)
    mods = _ada_mod(cvec, ada_w, ada_b)
    mods = mods.reshape(depth, MOD_ROWS, 1, 6 * d)

    xt = jnp.concatenate([ctx.reshape(n_ctx_rows, d), x.reshape(n_lat_rows, d)], axis=0)

    mod = mods[0]
    n_diff = d // (2 * HEAD_DIM)
    n_gqa = d // (2 * HEAD_DIM)
    n_gqa_kv = n_gqa // 4
    proj = _modproj(xt, mod, 1, 0, ab_w_in[0].astype(BF16), BF16, geom_all)
    lam_init = 0.8 - 0.6 * math.exp(-0.3 * 0)
    oa = _diff_attn(proj, tables64, diff_lambda[0], diff_subln_g[0], lam_init, bsz, n_ctx, seq, n_diff)
    ob = _gqa_attn(proj, tables128, gqa_q_norm_g[0], gqa_k_norm_g[0], 3 * n_diff, bsz, n_ctx, seq, n_gqa, n_gqa_kv)
    w_out = ab_w_out[0].astype(BF16)
    wa, wb = w_out[:n_diff * HEAD_DIM], w_out[n_diff * HEAD_DIM:]
    t_all = n_ctx_rows + n_lat_rows
    xt = _resln(xt, mod, 2, [oa, ob], [wa, wb], ln_mix_g[0], ln_mix_b[0], alpha, geom_all, 0, t_all, 0)
    xt = _moe_sublayer(xt, mod, geom_all, router_w[0], router_bias[0], expert_w_in[0], expert_w_out[0],
                       shared_w_in[0], shared_w_out[0], ln_ffn_g[0], ln_ffn_b[0], alpha)

    mod = mods[1]
    n_mla = d // HEAD_DIM
    qk = MLA_NOPE_DIM + MLA_ROPE_DIM
    w_down = jnp.pad(mla_w_down[0], ((0, 0), (0, LANES - MLA_ROPE_DIM))).astype(BF16)
    down = _modproj(xt, mod, 1, 0, w_down, F32, geom_all)
    w_uq = mla_w_uq[0].reshape(MLA_Q_RANK, n_mla, qk)
    w_uq_rope = jnp.pad(w_uq[:, :, MLA_NOPE_DIM:], ((0, 0), (0, 0), (0, LANES - MLA_ROPE_DIM)))
    w_uq = jnp.concatenate([w_uq[:, :, :MLA_NOPE_DIM].reshape(MLA_Q_RANK, -1),
                            w_uq_rope.reshape(MLA_Q_RANK, -1)], axis=1).astype(BF16)
    lat_blocks = n_ctx_rows // 1024
    q = _normproj(down, 0, mla_q_norm_g[0], w_uq, lat_blocks, n_lat_rows)
    kv = _normproj(down, 1, mla_kv_norm_g[0], mla_w_ukv[0].astype(BF16), 0, t_all)
    kpe_col = (MLA_Q_RANK + MLA_KV_RANK) // LANES
    o = _mla_attn(q, kv, down, tables64, bsz, n_ctx, seq, n_mla, kpe_col)
    xl = _resln(xt, mod, 2, [o], [mla_w_o[0].astype(BF16)], ln_mix_g[1], ln_mix_b[1], alpha, geom_all,
                n_ctx_rows // 512, n_lat_rows, 0)
    xl = _moe_sublayer(xl, mod, geom_lat, router_w[1], router_bias[1], expert_w_in[1], expert_w_out[1],
                       shared_w_in[1], shared_w_out[1], ln_ffn_g[1], ln_ffn_b[1], alpha)
    return xl.reshape(bsz, seq, d)
```

```python
import functools
import math

import numpy as np
import jax
import jax.numpy as jnp
from jax import lax
from jax.experimental import pallas as pl
from jax.experimental.pallas import tpu as pltpu

F32 = jnp.float32
BF16 = jnp.bfloat16

GRID_W = 64
ROPE_THETA = 10000.0
HEAD_DIM = 128
DIFF_QK_DIM = 64
N_EXPERTS = 64
EXPERT_DIM = 512
TOP_K = 8
N_GROUPS = 8
TOPK_GROUPS = 4
ROUTE_SCALE = 2.5
MLA_Q_RANK = 512
MLA_KV_RANK = 512
MLA_NOPE_DIM = 128
MLA_ROPE_DIM = 64
MLA_V_DIM = 128
LANES = 128

MOD_ROWS = 16
VMEM_LIMIT = 56 * 1024 * 1024
MOE_BLOCK = 256


def _cparams(sem):
    return pltpu.CompilerParams(dimension_semantics=sem, vmem_limit_bytes=VMEM_LIMIT)


def _dot(a, b):
    return jnp.dot(a, b, preferred_element_type=F32)


def _dot_nt(a, b):
    return lax.dot_general(a, b, (((1,), (1,)), ((), ())), preferred_element_type=F32)


def _dot_hi(a, b):
    return lax.dot_general(a, b, (((1,), (0,)), ((), ())), precision=lax.Precision.HIGHEST,
                           preferred_element_type=F32)


def _sigmoid(x):
    return 1.0 / (1.0 + jnp.exp(-x))


def _rope(x, cos, sin_up, sin_dn, shift):
    return (x * cos + pltpu.roll(x, LANES - shift, 1) * sin_up + pltpu.roll(x, shift, 1) * sin_dn)


def _rope_tables(rows, rot_dim):
    t = jnp.arange(rows * GRID_W)
    row = (t // GRID_W).astype(F32)
    col = (t % GRID_W).astype(F32)
    axis_dim = rot_dim // 2
    quarter = rot_dim // 4
    inv_freq = 1.0 / (ROPE_THETA ** (jnp.arange(0, axis_dim, 2, dtype=F32) / axis_dim))
    lane = np.arange(LANES) % rot_dim
    is_col = lane >= axis_dim
    within = lane % axis_dim
    first = within < quarter
    freq = within % quarter
    ang = jnp.where(jnp.asarray(is_col)[None, :], col[:, None], row[:, None]) * inv_freq[freq][None, :]
    cos = jnp.cos(ang).astype(F32)
    sin = jnp.sin(ang).astype(F32)
    first = jnp.asarray(first)[None, :]
    sin_up = jnp.where(first, -sin, 0.0)
    sin_dn = jnp.where(first, 0.0, sin)
    return cos, sin_up, sin_dn, quarter


def _softmax_rows(s):
    m = jnp.max(s, axis=-1, keepdims=True)
    e = jnp.exp(s - m)
    return e * (1.0 / jnp.sum(e, axis=-1, keepdims=True))


def _rms(x, g, eps):
    return x * lax.rsqrt(jnp.mean(x * x, axis=-1, keepdims=True) + eps) * g


def _ada_kernel(c_ref, w_ref, b_ref, o_ref):
    c = c_ref[...]
    act = c * _sigmoid(c)
    o_ref[0] = _dot_hi(act, w_ref[0]) + b_ref[0]


def _ada_mod(cvec, ada_w, ada_b):
    depth, d, n = ada_w.shape
    tn = 1536
    return pl.pallas_call(
        _ada_kernel,
        grid=(depth, n // tn),
        in_specs=[pl.BlockSpec((MOD_ROWS, d), lambda l, j: (0, 0)),
                  pl.BlockSpec((1, d, tn), lambda l, j: (l, 0, j)),
                  pl.BlockSpec((1, 1, tn), lambda l, j: (l, 0, j))],
        out_specs=pl.BlockSpec((1, MOD_ROWS, tn), lambda l, j: (l, 0, j)),
        out_shape=jax.ShapeDtypeStruct((depth, MOD_ROWS, n), F32),
        compiler_params=_cparams(("arbitrary", "arbitrary")),
        name="ada_mod",
    )(cvec, ada_w, ada_b.reshape(depth, 1, n))


def _mod_row(i, tm, n_ctx_rows, seq, ctx_row):
    r0 = i * tm
    return jnp.where(r0 < n_ctx_rows, ctx_row, (r0 - n_ctx_rows) // seq)


def _modproj_kernel(x_ref, sc_ref, sh_ref, w_ref, o_ref, h_ref):
    @pl.when(pl.program_id(1) == 0)
    def _():
        h_ref[...] = (x_ref[...] * (1.0 + sc_ref[0]) + sh_ref[0]).astype(h_ref.dtype)

    o_ref[...] = _dot(h_ref[...], w_ref[...]).astype(o_ref.dtype)


def _modproj(x, mod, sc_idx, sh_idx, w, out_dtype, geom, tm=1024, tn=512):
    t, d = x.shape
    n = w.shape[1]
    tn = tn if n % tn == 0 else n
    row = functools.partial(_mod_row, tm=tm, **geom)
    return pl.pallas_call(
        _modproj_kernel,
        grid=(t // tm, n // tn),
        in_specs=[pl.BlockSpec((tm, d), lambda i, j: (i, 0)),
                  pl.BlockSpec((1, 1, d), lambda i, j: (row(i), 0, sc_idx)),
                  pl.BlockSpec((1, 1, d), lambda i, j: (row(i), 0, sh_idx)),
                  pl.BlockSpec((d, tn), lambda i, j: (0, j))],
        out_specs=pl.BlockSpec((tm, tn), lambda i, j: (i, j)),
        out_shape=jax.ShapeDtypeStruct((t, n), out_dtype),
        scratch_shapes=[pltpu.VMEM((tm, d), BF16)],
        compiler_params=_cparams(("arbitrary", "arbitrary")),
        name="modproj",
    )(x, mod, mod, w)


def _normproj_kernel(x_ref, g_ref, w_ref, o_ref, h_ref):
    @pl.when(pl.program_id(1) == 0)
    def _():
        h_ref[...] = _rms(x_ref[...], g_ref[...], 1e-6).astype(h_ref.dtype)

    o_ref[...] = _dot(h_ref[...], w_ref[...]).astype(o_ref.dtype)


def _normproj(x, col_block, g, w, row_off_blocks, n_rows, tm=1024, tn=1024):
    k = w.shape[0]
    n = w.shape[1]
    return pl.pallas_call(
        _normproj_kernel,
        grid=(n_rows // tm, n // tn),
        in_specs=[pl.BlockSpec((tm, k), lambda i, j: (i + row_off_blocks, col_block)),
                  pl.BlockSpec((1, k), lambda i, j: (0, 0)),
                  pl.BlockSpec((k, tn), lambda i, j: (0, j))],
        out_specs=pl.BlockSpec((tm, tn), lambda i, j: (i, j)),
        out_shape=jax.ShapeDtypeStruct((n_rows, n), BF16),
        scratch_shapes=[pltpu.VMEM((tm, k), BF16)],
        compiler_params=_cparams(("arbitrary", "arbitrary")),
        name="normproj",
    )(x, g.reshape(1, k), w)


def _diff_lambda(lam_ref, lam_init):
    lp = lam_ref[...]
    s01 = jnp.sum(lp[0:1] * lp[1:2], axis=-1, keepdims=True)
    s23 = jnp.sum(lp[2:3] * lp[3:4], axis=-1, keepdims=True)
    return jnp.exp(s01) - jnp.exp(s23) + lam_init


def _diff_attend(q, k, v, lam, g, lam_init):
    lane = lax.broadcasted_iota(jnp.int32, q.shape, 1)
    qs = q * (DIFF_QK_DIM ** -0.5)
    q1 = jnp.where(lane < DIFF_QK_DIM, qs, 0.0).astype(BF16)
    q2 = jnp.where(lane >= DIFF_QK_DIM, qs, 0.0).astype(BF16)
    p1 = _softmax_rows(_dot_nt(q1, k))
    p2 = _softmax_rows(_dot_nt(q2, k))
    a = (p1 - lam * p2).astype(BF16)
    o = _dot(a, v)
    return _rms(o, g, 1e-5) * (1.0 - lam_init)


def _diff_attn_kernel(q_ref, kc_ref, kl_ref, vc_ref, vl_ref, cos_ref, su_ref, sd_ref, lam_ref, g_ref,
                      o_ref, k_scr, v_scr, *, n_ctx, tq, shift, lam_init):
    i = pl.program_id(2)

    @pl.when(i == 0)
    def _prep():
        k_scr[0:n_ctx, :] = kc_ref[...]
        kl = kl_ref[...].astype(F32)
        k_scr[n_ctx:, :] = _rope(kl, cos_ref[...], su_ref[...], sd_ref[...], shift).astype(BF16)
        v_scr[0:n_ctx, :] = vc_ref[...]
        v_scr[n_ctx:, :] = vl_ref[...]

    lam = _diff_lambda(lam_ref, lam_init)
    g = g_ref[...]

    @pl.when(i == 0)
    def _ctx_queries():
        q = q_ref[...].astype(F32)
        o_ref[...] = _diff_attend(q, k_scr[0:n_ctx, :], v_scr[0:n_ctx, :], lam, g, lam_init).astype(o_ref.dtype)

    @pl.when(i > 0)
    def _lat_queries():
        r0 = pl.multiple_of((i - 1) * tq, tq)
        q = _rope(q_ref[...].astype(F32), cos_ref[pl.ds(r0, tq), :], su_ref[pl.ds(r0, tq), :],
                  sd_ref[pl.ds(r0, tq), :], shift)
        o_ref[...] = _diff_attend(q, k_scr[...], v_scr[...], lam, g, lam_init).astype(o_ref.dtype)


def _q_block(b, i, bsz, n_ctx, seq, tq):
    return jnp.where(i == 0, b * (n_ctx // tq), (bsz * n_ctx + b * seq) // tq + i - 1)


def _diff_attn(proj, tables, lam_params, subln_g, lam_init, bsz, n_ctx, seq, n_heads, tq=256):
    t = proj.shape[0]
    cos, su, sd, shift = tables
    qb = functools.partial(_q_block, bsz=bsz, n_ctx=n_ctx, seq=seq, tq=tq)
    lat0 = bsz * n_ctx // seq
    kern = functools.partial(_diff_attn_kernel, n_ctx=n_ctx, tq=tq, shift=shift, lam_init=lam_init)
    tab = pl.BlockSpec((seq, LANES), lambda b, h, i: (0, 0))
    return pl.pallas_call(
        kern,
        grid=(bsz, n_heads, 1 + seq // tq),
        in_specs=[pl.BlockSpec((tq, LANES), lambda b, h, i: (qb(b, i), h)),
                  pl.BlockSpec((n_ctx, LANES), lambda b, h, i: (b, n_heads + h)),
                  pl.BlockSpec((seq, LANES), lambda b, h, i: (lat0 + b, n_heads + h)),
                  pl.BlockSpec((n_ctx, LANES), lambda b, h, i: (b, 2 * n_heads + h)),
                  pl.BlockSpec((seq, LANES), lambda b, h, i: (lat0 + b, 2 * n_heads + h)),
                  tab, tab, tab,
                  pl.BlockSpec((4, DIFF_QK_DIM), lambda b, h, i: (0, 0)),
                  pl.BlockSpec((1, LANES), lambda b, h, i: (0, 0))],
        out_specs=pl.BlockSpec((tq, LANES), lambda b, h, i: (qb(b, i), h)),
        out_shape=jax.ShapeDtypeStruct((t, n_heads * LANES), BF16),
        scratch_shapes=[pltpu.VMEM((n_ctx + seq, LANES), BF16), pltpu.VMEM((n_ctx + seq, LANES), BF16)],
        compiler_params=_cparams(("arbitrary", "arbitrary", "arbitrary")),
        name="diff_attn",
    )(proj, proj, proj, proj, proj, cos, su, sd, lam_params, subln_g.reshape(1, LANES))


def _softmax_pv(q, k, v):
    p = _softmax_rows(_dot_nt(q, k)).astype(BF16)
    return _dot(p, v)


def _gqa_attn_kernel(q_ref, kc_ref, kl_ref, vc_ref, vl_ref, cos_ref, su_ref, sd_ref, qg_ref, kg_ref,
                     o_ref, k_scr, v_scr, *, n_ctx, tq, shift):
    g = pl.program_id(2)
    i = pl.program_id(3)

    @pl.when((g == 0) & (i == 0))
    def _prep():
        kg = kg_ref[...]
        k_scr[0:n_ctx, :] = _rms(kc_ref[...].astype(F32), kg, 1e-6).astype(BF16)
        kl = _rms(kl_ref[...].astype(F32), kg, 1e-6)
        k_scr[n_ctx:, :] = _rope(kl, cos_ref[...], su_ref[...], sd_ref[...], shift).astype(BF16)
        v_scr[0:n_ctx, :] = vc_ref[...]
        v_scr[n_ctx:, :] = vl_ref[...]

    scale = HEAD_DIM ** -0.5
    qn = _rms(q_ref[...].astype(F32), qg_ref[...], 1e-6)

    @pl.when(i == 0)
    def _ctx_queries():
        q = (qn * scale).astype(BF16)
        o_ref[...] = _softmax_pv(q, k_scr[0:n_ctx, :], v_scr[0:n_ctx, :]).astype(o_ref.dtype)

    @pl.when(i > 0)
    def _lat_queries():
        r0 = pl.multiple_of((i - 1) * tq, tq)
        q = _rope(qn, cos_ref[pl.ds(r0, tq), :], su_ref[pl.ds(r0, tq), :], sd_ref[pl.ds(r0, tq), :], shift)
        q = (q * scale).astype(BF16)
        o_ref[...] = _softmax_pv(q, k_scr[...], v_scr[...]).astype(o_ref.dtype)


def _gqa_attn(proj, tables, q_norm_g, k_norm_g, col0, bsz, n_ctx, seq, n_heads, n_kv, tq=256):
    t = proj.shape[0]
    cos, su, sd, shift = tables
    grp = n_heads // n_kv
    qb = functools.partial(_q_block, bsz=bsz, n_ctx=n_ctx, seq=seq, tq=tq)
    lat0 = bsz * n_ctx // seq
    kcol = col0 + n_heads
    vcol = kcol + n_kv
    kern = functools.partial(_gqa_attn_kernel, n_ctx=n_ctx, tq=tq, shift=shift)
    tab = pl.BlockSpec((seq, LANES), lambda b, h, g, i: (0, 0))
    vec = pl.BlockSpec((1, LANES), lambda b, h, g, i: (0, 0))
    return pl.pallas_call(
        kern,
        grid=(bsz, n_kv, grp, 1 + seq // tq),
        in_specs=[pl.BlockSpec((tq, LANES), lambda b, h, g, i: (qb(b, i), col0 + h * grp + g)),
                  pl.BlockSpec((n_ctx, LANES), lambda b, h, g, i: (b, kcol + h)),
                  pl.BlockSpec((seq, LANES), lambda b, h, g, i: (lat0 + b, kcol + h)),
                  pl.BlockSpec((n_ctx, LANES), lambda b, h, g, i: (b, vcol + h)),
                  pl.BlockSpec((seq, LANES), lambda b, h, g, i: (lat0 + b, vcol + h)),
                  tab, tab, tab, vec, vec],
        out_specs=pl.BlockSpec((tq, LANES), lambda b, h, g, i: (qb(b, i), h * grp + g)),
        out_shape=jax.ShapeDtypeStruct((t, n_heads * LANES), BF16),
        scratch_shapes=[pltpu.VMEM((n_ctx + seq, LANES), BF16), pltpu.VMEM((n_ctx + seq, LANES), BF16)],
        compiler_params=_cparams(("arbitrary", "arbitrary", "arbitrary", "arbitrary")),
        name="gqa_attn",
    )(proj, proj, proj, proj, proj, cos, su, sd, q_norm_g.reshape(1, LANES), k_norm_g.reshape(1, LANES))


def _mla_attn_kernel(qn_ref, qp_ref, knc_ref, knl_ref, kpc_ref, kpl_ref, vc_ref, vl_ref,
                     cos_ref, su_ref, sd_ref, o_ref, k_scr, v_scr, *, n_ctx, tq, shift):
    i = pl.program_id(2)

    @pl.when(i == 0)
    def _prep():
        k_scr[0:n_ctx, 0:LANES] = knc_ref[...]
        k_scr[n_ctx:, 0:LANES] = knl_ref[...]
        k_scr[0:n_ctx, LANES:] = kpc_ref[...].astype(BF16)
        k_scr[n_ctx:, LANES:] = _rope(kpl_ref[...], cos_ref[...], su_ref[...], sd_ref[...], shift).astype(BF16)
        v_scr[0:n_ctx, :] = vc_ref[...]
        v_scr[n_ctx:, :] = vl_ref[...]

    scale = (MLA_NOPE_DIM + MLA_ROPE_DIM) ** -0.5
    r0 = pl.multiple_of(i * tq, tq)
    qp = _rope(qp_ref[...].astype(F32), cos_ref[pl.ds(r0, tq), :], su_ref[pl.ds(r0, tq), :],
               sd_ref[pl.ds(r0, tq), :], shift)
    q = jnp.concatenate([(qn_ref[...].astype(F32) * scale).astype(BF16), (qp * scale).astype(BF16)], axis=-1)
    o_ref[...] = _softmax_pv(q, k_scr[...], v_scr[...]).astype(o_ref.dtype)


def _mla_attn(q, kv, down, tables, bsz, n_ctx, seq, n_heads, kpe_col, tq=256):
    cos, su, sd, shift = tables
    lat0 = bsz * n_ctx // seq
    nq = seq // tq
    kern = functools.partial(_mla_attn_kernel, n_ctx=n_ctx, tq=tq, shift=shift)
    tab = pl.BlockSpec((seq, LANES), lambda b, h, i: (0, 0))
    return pl.pallas_call(
        kern,
        grid=(bsz, n_heads, nq),
        in_specs=[pl.BlockSpec((tq, LANES), lambda b, h, i: (b * nq + i, h)),
                  pl.BlockSpec((tq, LANES), lambda b, h, i: (b * nq + i, n_heads + h)),
                  pl.BlockSpec((n_ctx, LANES), lambda b, h, i: (b, 2 * h)),
                  pl.BlockSpec((seq, LANES), lambda b, h, i: (lat0 + b, 2 * h)),
                  pl.BlockSpec((n_ctx, LANES), lambda b, h, i: (b, kpe_col)),
                  pl.BlockSpec((seq, LANES), lambda b, h, i: (lat0 + b, kpe_col)),
                  pl.BlockSpec((n_ctx, LANES), lambda b, h, i: (b, 2 * h + 1)),
                  pl.BlockSpec((seq, LANES), lambda b, h, i: (lat0 + b, 2 * h + 1)),
                  tab, tab, tab],
        out_specs=pl.BlockSpec((tq, LANES), lambda b, h, i: (b * nq + i, h)),
        out_shape=jax.ShapeDtypeStruct((bsz * seq, n_heads * LANES), BF16),
        scratch_shapes=[pltpu.VMEM((n_ctx + seq, 2 * LANES), BF16), pltpu.VMEM((n_ctx + seq, LANES), BF16)],
        compiler_params=_cparams(("arbitrary", "arbitrary", "arbitrary")),
        name="mla_attn",
    )(q, q, kv, kv, down, down, kv, kv, cos, su, sd)


def _layer_norm(z, g, b):
    zc = z - jnp.mean(z, axis=-1, keepdims=True)
    var = jnp.mean(zc * zc, axis=-1, keepdims=True)
    return zc * lax.rsqrt(var + 1e-5) * g + b


def _resln_kernel(*refs, n_act, alpha):
    x_ref, gate_ref = refs[0], refs[1]
    acts = refs[2:2 + n_act]
    ws = refs[2 + n_act:2 + 2 * n_act]
    lng_ref, lnb_ref, o_ref = refs[2 + 2 * n_act:]
    y = _dot(acts[0][...], ws[0][...])
    for a_ref, w_ref in zip(acts[1:], ws[1:]):
        y = y + _dot(a_ref[...], w_ref[...])
    z = alpha * x_ref[...] + gate_ref[0] * y
    o_ref[...] = _layer_norm(z, lng_ref[...], lnb_ref[...])


def _resln(x, mod, gate_idx, acts, ws, ln_g, ln_b, alpha, geom, row_off_blocks, n_rows, act_off_blocks, tm=512):
    d = x.shape[1]
    row = functools.partial(_mod_row, tm=tm, **geom)
    n_act = len(acts)
    in_specs = [pl.BlockSpec((tm, d), lambda i: (i + row_off_blocks, 0)),
                pl.BlockSpec((1, 1, d), lambda i: (row(i + row_off_blocks), 0, gate_idx))]
    in_specs += [pl.BlockSpec((tm, a.shape[1]), lambda i: (i + act_off_blocks, 0)) for a in acts]
    in_specs += [pl.BlockSpec(w.shape, lambda i: (0, 0)) for w in ws]
    in_specs += [pl.BlockSpec((1, d), lambda i: (0, 0))] * 2
    return pl.pallas_call(
        functools.partial(_resln_kernel, n_act=n_act, alpha=alpha),
        grid=(n_rows // tm,),
        in_specs=in_specs,
        out_specs=pl.BlockSpec((tm, d), lambda i: (i, 0)),
        out_shape=jax.ShapeDtypeStruct((n_rows, d), F32),
        compiler_params=_cparams(("arbitrary",)),
        name="proj_resln",
    )(x, mod, *acts, *ws, ln_g.reshape(1, d), ln_b.reshape(1, d))


def _pick_first_max(cur, idx, axes, sentinel):
    m = cur
    for ax in axes:
        m = jnp.max(m, axis=ax, keepdims=True)
    first = jnp.where(cur == m, idx, sentinel)
    for ax in axes:
        first = jnp.min(first, axis=ax, keepdims=True)
    return m, first


def _route_kernel(x_ref, sc_ref, sh_ref, rwt_ref, bias_ref, tri_ref, h_ref, e_ref, g_ref, r_ref, cnt_ref, run_ref):
    tm = x_ref.shape[0]
    n_members = N_EXPERTS // N_GROUPS

    @pl.when(pl.program_id(0) == 0)
    def _():
        run_ref[...] = jnp.zeros_like(run_ref)

    h = x_ref[...] * (1.0 + sc_ref[0]) + sh_ref[0]
    h_ref[...] = h.astype(h_ref.dtype)
    logits = lax.dot_general(rwt_ref[...], h, (((1,), (1,)), ((), ())), precision=lax.Precision.HIGHEST,
                             preferred_element_type=F32)
    scores = _sigmoid(logits).reshape(N_GROUPS, n_members, tm)
    biased = scores + bias_ref[...]
    neg = -jnp.inf
    member = lax.broadcasted_iota(jnp.int32, biased.shape, 1).astype(F32)
    group = lax.broadcasted_iota(jnp.int32, biased.shape, 0).astype(F32)
    expert = group * n_members + member

    m1, first = _pick_first_max(biased, member, (1,), float(n_members))
    m2 = jnp.max(jnp.where(member == first, neg, biased), axis=1, keepdims=True)
    gscore = m1 + m2
    gidx = lax.broadcasted_iota(jnp.int32, gscore.shape, 0).astype(F32)
    group_ok = jnp.zeros(gscore.shape, jnp.bool_)
    for _ in range(TOPK_GROUPS):
        _, first = _pick_first_max(gscore, gidx, (0,), float(N_GROUPS))
        pick = gidx == first
        group_ok = group_ok | pick
        gscore = jnp.where(pick, neg, gscore)

    cur = jnp.where(group_ok, biased, neg)
    chosen = jnp.zeros(biased.shape, jnp.bool_)
    top_e, gates = [], []
    for _ in range(TOP_K):
        _, first = _pick_first_max(cur, expert, (0, 1), float(N_EXPERTS))
        pick = expert == first
        chosen = chosen | pick
        cur = jnp.where(pick, neg, cur)
        top_e.append(first)
        gates.append(jnp.sum(jnp.sum(jnp.where(pick, scores, 0.0), axis=0, keepdims=True), axis=1, keepdims=True))
    gsum = gates[0]
    for gk in gates[1:]:
        gsum = gsum + gk
    norm = ROUTE_SCALE / gsum

    chosen2d = jnp.where(chosen, 1.0, 0.0).reshape(N_EXPERTS, tm)
    before = _dot(chosen2d.astype(BF16), tri_ref[...])
    rank = (run_ref[...] + before).reshape(N_GROUPS, n_members, tm)
    run_ref[...] = run_ref[...] + jnp.sum(chosen2d, axis=1, keepdims=True)
    cnt_ref[...] = run_ref[...].astype(jnp.int32)
    for k in range(TOP_K):
        pick = expert == top_e[k]
        rk = jnp.sum(jnp.sum(jnp.where(pick, rank, 0.0), axis=0, keepdims=True), axis=1, keepdims=True)
        e_ref[k:k + 1, :] = top_e[k].reshape(1, tm).astype(jnp.int32)
        g_ref[k:k + 1, :] = (gates[k] * norm).reshape(1, tm)
        r_ref[k:k + 1, :] = rk.reshape(1, tm).astype(jnp.int32)


def _route(x, mod, sc_idx, sh_idx, router_w, router_bias, geom, tm=512):
    t, d = x.shape
    row = functools.partial(_mod_row, tm=tm, **geom)
    tri = (np.arange(tm)[:, None] < np.arange(tm)[None, :]).astype(np.float32)
    kt = pl.BlockSpec((TOP_K, tm), lambda i: (0, i))
    return pl.pallas_call(
        _route_kernel,
        grid=(t // tm,),
        in_specs=[pl.BlockSpec((tm, d), lambda i: (i, 0)),
                  pl.BlockSpec((1, 1, d), lambda i: (row(i), 0, sc_idx)),
                  pl.BlockSpec((1, 1, d), lambda i: (row(i), 0, sh_idx)),
                  pl.BlockSpec((N_EXPERTS, d), lambda i: (0, 0)),
                  pl.BlockSpec((N_GROUPS, N_EXPERTS // N_GROUPS, 1), lambda i: (0, 0, 0)),
                  pl.BlockSpec((tm, tm), lambda i: (0, 0))],
        out_specs=[pl.BlockSpec((tm, d), lambda i: (i, 0)), kt, kt, kt,
                   pl.BlockSpec((N_EXPERTS, 1), lambda i: (0, 0))],
        out_shape=[jax.ShapeDtypeStruct((t, d), BF16),
                   jax.ShapeDtypeStruct((TOP_K, t), jnp.int32),
                   jax.ShapeDtypeStruct((TOP_K, t), F32),
                   jax.ShapeDtypeStruct((TOP_K, t), jnp.int32),
                   jax.ShapeDtypeStruct((N_EXPERTS, 1), jnp.int32)],
        scratch_shapes=[pltpu.VMEM((N_EXPERTS, 1), F32)],
        compiler_params=_cparams(("arbitrary",)),
        name="moe_route",
    )(x, mod, mod, router_w.T, router_bias.astype(F32).reshape(N_GROUPS, N_EXPERTS // N_GROUPS, 1),
      jnp.asarray(tri, BF16))


def _dispatch_plan(top_e, rank, counts):
    t = top_e.shape[1]
    n_blocks = t * TOP_K // MOE_BLOCK + N_EXPERTS
    p = n_blocks * MOE_BLOCK
    counts = counts.reshape(N_EXPERTS)
    padded = (counts + MOE_BLOCK - 1) // MOE_BLOCK * MOE_BLOCK
    pad_ends = jnp.cumsum(padded)
    pad_starts = pad_ends - padded
    dest = pad_starts[top_e] + rank
    tok = jnp.broadcast_to(jnp.arange(t, dtype=jnp.int32)[None, :], dest.shape)
    slot_tok = jnp.zeros((p,), jnp.int32).at[dest.reshape(-1)].set(tok.reshape(-1), unique_indices=True,
                                                                   mode="promise_in_bounds")
    block_start = jnp.arange(n_blocks, dtype=jnp.int32) * MOE_BLOCK
    block_e = jnp.minimum(jnp.sum(pad_ends[None, :] <= block_start[:, None], axis=1), N_EXPERTS - 1)
    n_used = (pad_ends[-1] // MOE_BLOCK).astype(jnp.int32).reshape(1)
    return dest, slot_tok, block_e.astype(jnp.int32), n_used


def _experts_kernel(be_ref, nu_ref, x_ref, wg_ref, wu_ref, wo_ref, o_ref, wg_scr, wu_scr, wo_scr):
    blk = pl.program_id(0)
    prev = be_ref[jnp.maximum(blk - 1, 0)]

    @pl.when((blk == 0) | (be_ref[blk] != prev))
    def _():
        wg_scr[...] = wg_ref[0, 0].astype(BF16)
        wu_scr[...] = wu_ref[0, 0].astype(BF16)
        wo_scr[...] = wo_ref[0, 0].astype(BF16)

    @pl.when(blk < nu_ref[0])
    def _():
        x = x_ref[...]
        gate = _dot(x, wg_scr[...])
        up = _dot(x, wu_scr[...])
        a = (gate * _sigmoid(gate) * up).astype(BF16)
        o_ref[...] = _dot(a, wo_scr[...]).astype(o_ref.dtype)

    @pl.when(blk >= nu_ref[0])
    def _():
        o_ref[...] = jnp.zeros_like(o_ref)


def _experts(xs, block_e, n_used, w_in, w_out, layer):
    p, d = xs.shape
    f = w_out.shape[2]
    n_blocks = p // MOE_BLOCK
    grid_spec = pltpu.PrefetchScalarGridSpec(
        num_scalar_prefetch=2,
        grid=(n_blocks,),
        in_specs=[pl.BlockSpec((MOE_BLOCK, d), lambda i, be, nu: (i, 0)),
                  pl.BlockSpec((1, 1, d, f), lambda i, be, nu: (layer, be[i], 0, 0)),
                  pl.BlockSpec((1, 1, d, f), lambda i, be, nu: (layer, be[i], 0, 1)),
                  pl.BlockSpec((1, 1, f, d), lambda i, be, nu: (layer, be[i], 0, 0))],
        out_specs=pl.BlockSpec((MOE_BLOCK, d), lambda i, be, nu: (i, 0)),
        scratch_shapes=[pltpu.VMEM((d, f), BF16), pltpu.VMEM((d, f), BF16), pltpu.VMEM((f, d), BF16)],
    )
    return pl.pallas_call(
        _experts_kernel,
        grid_spec=grid_spec,
        out_shape=jax.ShapeDtypeStruct((p, d), BF16),
        compiler_params=_cparams(("arbitrary",)),
        name="moe_experts",
    )(block_e, n_used, xs, w_in, w_in, w_out)


def _ffn_out_kernel(x_ref, gate_ref, h_ref, yg_ref, rg_ref, wg_ref, wu_ref, wo_ref, lng_ref, lnb_ref, o_ref, *, alpha):
    h = h_ref[...]
    gate = _dot(h, wg_ref[...])
    up = _dot(h, wu_ref[...])
    a = (gate * _sigmoid(gate) * up).astype(BF16)
    y = _dot(a, wo_ref[...])
    rg = rg_ref[...]
    for k in range(TOP_K):
        y = y + yg_ref[k].astype(F32) * rg[:, k:k + 1]
    z = alpha * x_ref[...] + gate_ref[0] * y
    o_ref[...] = _layer_norm(z, lng_ref[...], lnb_ref[...])


def _ffn_out(x, mod, gate_idx, h, yg, route_gate, sh_in, sh_out, ln_g, ln_b, alpha, geom, tm=256):
    t, d = x.shape
    f = sh_out.shape[0]
    row = functools.partial(_mod_row, tm=tm, **geom)
    return pl.pallas_call(
        functools.partial(_ffn_out_kernel, alpha=alpha),
        grid=(t // tm,),
        in_specs=[pl.BlockSpec((tm, d), lambda i: (i, 0)),
                  pl.BlockSpec((1, 1, d), lambda i: (row(i), 0, gate_idx)),
                  pl.BlockSpec((tm, d), lambda i: (i, 0)),
                  pl.BlockSpec((TOP_K, tm, d), lambda i: (0, i, 0)),
                  pl.BlockSpec((tm, TOP_K), lambda i: (i, 0)),
                  pl.BlockSpec((d, f), lambda i: (0, 0)),
                  pl.BlockSpec((d, f), lambda i: (0, 1)),
                  pl.BlockSpec((f, d), lambda i: (0, 0)),
                  pl.BlockSpec((1, d), lambda i: (0, 0)),
                  pl.BlockSpec((1, d), lambda i: (0, 0))],
        out_specs=pl.BlockSpec((tm, d), lambda i: (i, 0)),
        out_shape=jax.ShapeDtypeStruct((t, d), F32),
        compiler_params=_cparams(("arbitrary",)),
        name="ffn_out",
    )(x, mod, h, yg, route_gate, sh_in, sh_in, sh_out, ln_g.reshape(1, d), ln_b.reshape(1, d))


def _moe_sublayer(x, mod, geom, layer, router_w, router_bias, w_in, w_out, sh_in, sh_out, ln_g, ln_b, alpha):
    h, top_e, gate, rank, counts = _route(x, mod, 4, 3, router_w, router_bias, geom)
    dest, slot_tok, block_e, n_used = _dispatch_plan(top_e, rank, counts)
    xs = jnp.take(h, slot_tok, axis=0, mode="clip")
    ys = _experts(xs, block_e, n_used, w_in, w_out, layer)
    yg = jnp.take(ys, dest, axis=0, mode="clip")
    return _ffn_out(x, mod, 5, h, yg, gate.T, sh_in.astype(BF16), sh_out.astype(BF16), ln_g, ln_b, alpha, geom)


def kernel(x, c, ctx, c_ctx, ada_w, ada_b, ln_mix_g, ln_mix_b, ln_ffn_g, ln_ffn_b, ab_w_in, ab_w_out, diff_lambda, diff_subln_g, gqa_q_norm_g, gqa_k_norm_g, mla_w_down, mla_q_norm_g, mla_w_uq, mla_kv_norm_g, mla_w_ukv, mla_w_o, router_w, router_bias, expert_w_in, expert_w_out, shared_w_in, shared_w_out):
    bsz, seq, d = x.shape
    n_ctx = ctx.shape[1]
    depth = ada_w.shape[0]
    assert depth == 2 and seq % GRID_W == 0 and bsz + 1 <= MOD_ROWS
    alpha = (2 * depth) ** 0.25
    n_ctx_rows = bsz * n_ctx
    n_lat_rows = bsz * seq
    geom_all = dict(n_ctx_rows=n_ctx_rows, seq=seq, ctx_row=bsz)
    geom_lat = dict(n_ctx_rows=0, seq=seq, ctx_row=bsz)

    tables64 = _rope_tables(seq // GRID_W, 64)
    tables128 = _rope_tables(seq // GRID_W, HEAD_DIM)

    cvec = jnp.zeros((MOD_ROWS, d), F32).at[:bsz].set(c).at[bsz].set(c_ctx)
    mods = _ada_mod(cvec, ada_w, ada_b)
    mods = mods.reshape(depth, MOD_ROWS, 1, 6 * d)

    xt = jnp.concatenate([ctx.reshape(n_ctx_rows, d), x.reshape(n_lat_rows, d)], axis=0)

    mod = mods[0]
    n_diff = d // (2 * HEAD_DIM)
    n_gqa = d // (2 * HEAD_DIM)
    n_gqa_kv = n_gqa // 4
    proj = _modproj(xt, mod, 1, 0, ab_w_in[0].astype(BF16), BF16, geom_all)
    lam_init = 0.8 - 0.6 * math.exp(-0.3 * 0)
    oa = _diff_attn(proj, tables64, diff_lambda[0], diff_subln_g[0], lam_init, bsz, n_ctx, seq, n_diff)
    ob = _gqa_attn(proj, tables128, gqa_q_norm_g[0], gqa_k_norm_g[0], 3 * n_diff, bsz, n_ctx, seq, n_gqa, n_gqa_kv)
    w_out = ab_w_out[0].astype(BF16)
    wa, wb = w_out[:n_diff * HEAD_DIM], w_out[n_diff * HEAD_DIM:]
    t_all = n_ctx_rows + n_lat_rows
    xt = _resln(xt, mod, 2, [oa, ob], [wa, wb], ln_mix_g[0], ln_mix_b[0], alpha, geom_all, 0, t_all, 0)
    xt = _moe_sublayer(xt, mod, geom_all, 0, router_w[0], router_bias[0], expert_w_in, expert_w_out,
                       shared_w_in[0], shared_w_out[0], ln_ffn_g[0], ln_ffn_b[0], alpha)

    mod = mods[1]
    n_mla = d // HEAD_DIM
    qk = MLA_NOPE_DIM + MLA_ROPE_DIM
    w_down = jnp.pad(mla_w_down[0], ((0, 0), (0, LANES - MLA_ROPE_DIM))).astype(BF16)
    down = _modproj(xt, mod, 1, 0, w_down, F32, geom_all)
    w_uq = mla_w_uq[0].reshape(MLA_Q_RANK, n_mla, qk)
    w_uq_rope = jnp.pad(w_uq[:, :, MLA_NOPE_DIM:], ((0, 0), (0, 0), (0, LANES - MLA_ROPE_DIM)))
    w_uq = jnp.concatenate([w_uq[:, :, :MLA_NOPE_DIM].reshape(MLA_Q_RANK, -1),
                            w_uq_rope.reshape(MLA_Q_RANK, -1)], axis=1).astype(BF16)
    lat_blocks = n_ctx_rows // 1024
    q = _normproj(down, 0, mla_q_norm_g[0], w_uq, lat_blocks, n_lat_rows)
    kv = _normproj(down, 1, mla_kv_norm_g[0], mla_w_ukv[0].astype(BF16), 0, t_all)
    kpe_col = (MLA_Q_RANK + MLA_KV_RANK) // LANES
    o = _mla_attn(q, kv, down, tables64, bsz, n_ctx, seq, n_mla, kpe_col)
    xl = _resln(xt, mod, 2, [o], [mla_w_o[0].astype(BF16)], ln_mix_g[1], ln_mix_b[1], alpha, geom_all,
                n_ctx_rows // 512, n_lat_rows, 0)
    xl = _moe_sublayer(xl, mod, geom_lat, 1, router_w[1], router_bias[1], expert_w_in, expert_w_out,
                       shared_w_in[1], shared_w_out[1], ln_ffn_g[1], ln_ffn_b[1], alpha)
    return xl.reshape(bsz, seq, d)
```

```python
import functools
import math

import numpy as np
import jax
import jax.numpy as jnp
from jax import lax
from jax.experimental import pallas as pl
from jax.experimental.pallas import tpu as pltpu

F32 = jnp.float32
BF16 = jnp.bfloat16

GRID_W = 64
ROPE_THETA = 10000.0
HEAD_DIM = 128
DIFF_QK_DIM = 64
N_EXPERTS = 64
EXPERT_DIM = 512
TOP_K = 8
N_GROUPS = 8
TOPK_GROUPS = 4
ROUTE_SCALE = 2.5
MLA_Q_RANK = 512
MLA_KV_RANK = 512
MLA_NOPE_DIM = 128
MLA_ROPE_DIM = 64
MLA_V_DIM = 128
LANES = 128

MOD_ROWS = 16
VMEM_LIMIT = 56 * 1024 * 1024
MOE_BLOCK = 256


def _cparams(sem):
    return pltpu.CompilerParams(dimension_semantics=sem, vmem_limit_bytes=VMEM_LIMIT)


def _dot(a, b):
    return jnp.dot(a, b, preferred_element_type=F32)


def _dot_nt(a, b):
    return lax.dot_general(a, b, (((1,), (1,)), ((), ())), preferred_element_type=F32)


def _dot_hi(a, b):
    return lax.dot_general(a, b, (((1,), (0,)), ((), ())), precision=lax.Precision.HIGHEST,
                           preferred_element_type=F32)


def _sigmoid(x):
    return 1.0 / (1.0 + jnp.exp(-x))


def _rope(x, cos, sin_up, sin_dn, shift):
    return (x * cos + pltpu.roll(x, LANES - shift, 1) * sin_up + pltpu.roll(x, shift, 1) * sin_dn)


def _rope_tables(rows, rot_dim):
    t = jnp.arange(rows * GRID_W)
    row = (t // GRID_W).astype(F32)
    col = (t % GRID_W).astype(F32)
    axis_dim = rot_dim // 2
    quarter = rot_dim // 4
    inv_freq = 1.0 / (ROPE_THETA ** (jnp.arange(0, axis_dim, 2, dtype=F32) / axis_dim))
    lane = np.arange(LANES) % rot_dim
    is_col = lane >= axis_dim
    within = lane % axis_dim
    first = within < quarter
    freq = within % quarter
    ang = jnp.where(jnp.asarray(is_col)[None, :], col[:, None], row[:, None]) * inv_freq[freq][None, :]
    cos = jnp.cos(ang).astype(F32)
    sin = jnp.sin(ang).astype(F32)
    first = jnp.asarray(first)[None, :]
    sin_up = jnp.where(first, -sin, 0.0)
    sin_dn = jnp.where(first, 0.0, sin)
    return cos, sin_up, sin_dn, quarter


def _softmax_rows(s):
    m = jnp.max(s, axis=-1, keepdims=True)
    e = jnp.exp(s - m)
    return e * (1.0 / jnp.sum(e, axis=-1, keepdims=True))


def _rms(x, g, eps):
    return x * lax.rsqrt(jnp.mean(x * x, axis=-1, keepdims=True) + eps) * g


def _ada_kernel(c_ref, w_ref, b_ref, o_ref):
    c = c_ref[...]
    act = c * _sigmoid(c)
    o_ref[0] = _dot_hi(act, w_ref[0]) + b_ref[0]


def _ada_mod(cvec, ada_w, ada_b):
    depth, d, n = ada_w.shape
    tn = 1536
    return pl.pallas_call(
        _ada_kernel,
        grid=(depth, n // tn),
        in_specs=[pl.BlockSpec((MOD_ROWS, d), lambda l, j: (0, 0)),
                  pl.BlockSpec((1, d, tn), lambda l, j: (l, 0, j)),
                  pl.BlockSpec((1, 1, tn), lambda l, j: (l, 0, j))],
        out_specs=pl.BlockSpec((1, MOD_ROWS, tn), lambda l, j: (l, 0, j)),
        out_shape=jax.ShapeDtypeStruct((depth, MOD_ROWS, n), F32),
        compiler_params=_cparams(("arbitrary", "arbitrary")),
        name="ada_mod",
    )(cvec, ada_w, ada_b.reshape(depth, 1, n))


def _mod_row(i, tm, n_ctx_rows, seq, ctx_row):
    r0 = i * tm
    return jnp.where(r0 < n_ctx_rows, ctx_row, (r0 - n_ctx_rows) // seq)


def _modproj_kernel(x_ref, sc_ref, sh_ref, w_ref, o_ref, h_ref):
    @pl.when(pl.program_id(1) == 0)
    def _():
        h_ref[...] = (x_ref[...] * (1.0 + sc_ref[0]) + sh_ref[0]).astype(h_ref.dtype)

    o_ref[...] = _dot(h_ref[...], w_ref[...]).astype(o_ref.dtype)


def _modproj(x, mod, sc_idx, sh_idx, w, out_dtype, geom, tm=1024, tn=512):
    t, d = x.shape
    n = w.shape[1]
    tn = tn if n % tn == 0 else n
    row = functools.partial(_mod_row, tm=tm, **geom)
    return pl.pallas_call(
        _modproj_kernel,
        grid=(t // tm, n // tn),
        in_specs=[pl.BlockSpec((tm, d), lambda i, j: (i, 0)),
                  pl.BlockSpec((1, 1, d), lambda i, j: (row(i), 0, sc_idx)),
                  pl.BlockSpec((1, 1, d), lambda i, j: (row(i), 0, sh_idx)),
                  pl.BlockSpec((d, tn), lambda i, j: (0, j))],
        out_specs=pl.BlockSpec((tm, tn), lambda i, j: (i, j)),
        out_shape=jax.ShapeDtypeStruct((t, n), out_dtype),
        scratch_shapes=[pltpu.VMEM((tm, d), BF16)],
        compiler_params=_cparams(("arbitrary", "arbitrary")),
        name="modproj",
    )(x, mod, mod, w)


def _normproj_kernel(x_ref, g_ref, w_ref, o_ref, h_ref):
    @pl.when(pl.program_id(1) == 0)
    def _():
        h_ref[...] = _rms(x_ref[...], g_ref[...], 1e-6).astype(h_ref.dtype)

    o_ref[...] = _dot(h_ref[...], w_ref[...]).astype(o_ref.dtype)


def _normproj(x, col_block, g, w, row_off_blocks, n_rows, tm=1024, tn=1024):
    k = w.shape[0]
    n = w.shape[1]
    return pl.pallas_call(
        _normproj_kernel,
        grid=(n_rows // tm, n // tn),
        in_specs=[pl.BlockSpec((tm, k), lambda i, j: (i + row_off_blocks, col_block)),
                  pl.BlockSpec((1, k), lambda i, j: (0, 0)),
                  pl.BlockSpec((k, tn), lambda i, j: (0, j))],
        out_specs=pl.BlockSpec((tm, tn), lambda i, j: (i, j)),
        out_shape=jax.ShapeDtypeStruct((n_rows, n), BF16),
        scratch_shapes=[pltpu.VMEM((tm, k), BF16)],
        compiler_params=_cparams(("arbitrary", "arbitrary")),
        name="normproj",
    )(x, g.reshape(1, k), w)


def _diff_lambda(lam_ref, lam_init):
    lp = lam_ref[...]
    s01 = jnp.sum(lp[0:1] * lp[1:2], axis=-1, keepdims=True)
    s23 = jnp.sum(lp[2:3] * lp[3:4], axis=-1, keepdims=True)
    return jnp.exp(s01) - jnp.exp(s23) + lam_init


def _diff_attend(q, k, v, lam, g, lam_init):
    lane = lax.broadcasted_iota(jnp.int32, q.shape, 1)
    qs = q * (DIFF_QK_DIM ** -0.5)
    q1 = jnp.where(lane < DIFF_QK_DIM, qs, 0.0).astype(BF16)
    q2 = jnp.where(lane >= DIFF_QK_DIM, qs, 0.0).astype(BF16)
    p1 = _softmax_rows(_dot_nt(q1, k))
    p2 = _softmax_rows(_dot_nt(q2, k))
    a = (p1 - lam * p2).astype(BF16)
    o = _dot(a, v)
    return _rms(o, g, 1e-5) * (1.0 - lam_init)


def _diff_attn_kernel(q_ref, kc_ref, kl_ref, vc_ref, vl_ref, cos_ref, su_ref, sd_ref, lam_ref, g_ref,
                      o_ref, k_scr, v_scr, *, n_ctx, tq, shift, lam_init):
    i = pl.program_id(2)

    @pl.when(i == 0)
    def _prep():
        k_scr[0:n_ctx, :] = kc_ref[...]
        kl = kl_ref[...].astype(F32)
        k_scr[n_ctx:, :] = _rope(kl, cos_ref[...], su_ref[...], sd_ref[...], shift).astype(BF16)
        v_scr[0:n_ctx, :] = vc_ref[...]
        v_scr[n_ctx:, :] = vl_ref[...]

    lam = _diff_lambda(lam_ref, lam_init)
    g = g_ref[...]

    @pl.when(i == 0)
    def _ctx_queries():
        q = q_ref[...].astype(F32)
        o_ref[...] = _diff_attend(q, k_scr[0:n_ctx, :], v_scr[0:n_ctx, :], lam, g, lam_init).astype(o_ref.dtype)

    @pl.when(i > 0)
    def _lat_queries():
        r0 = pl.multiple_of((i - 1) * tq, tq)
        q = _rope(q_ref[...].astype(F32), cos_ref[pl.ds(r0, tq), :], su_ref[pl.ds(r0, tq), :],
                  sd_ref[pl.ds(r0, tq), :], shift)
        o_ref[...] = _diff_attend(q, k_scr[...], v_scr[...], lam, g, lam_init).astype(o_ref.dtype)


def _q_block(b, i, bsz, n_ctx, seq, tq):
    return jnp.where(i == 0, b * (n_ctx // tq), (bsz * n_ctx + b * seq) // tq + i - 1)


def _diff_attn(proj, tables, lam_params, subln_g, lam_init, bsz, n_ctx, seq, n_heads, tq=256):
    t = proj.shape[0]
    cos, su, sd, shift = tables
    qb = functools.partial(_q_block, bsz=bsz, n_ctx=n_ctx, seq=seq, tq=tq)
    lat0 = bsz * n_ctx // seq
    kern = functools.partial(_diff_attn_kernel, n_ctx=n_ctx, tq=tq, shift=shift, lam_init=lam_init)
    tab = pl.BlockSpec((seq, LANES), lambda b, h, i: (0, 0))
    return pl.pallas_call(
        kern,
        grid=(bsz, n_heads, 1 + seq // tq),
        in_specs=[pl.BlockSpec((tq, LANES), lambda b, h, i: (qb(b, i), h)),
                  pl.BlockSpec((n_ctx, LANES), lambda b, h, i: (b, n_heads + h)),
                  pl.BlockSpec((seq, LANES), lambda b, h, i: (lat0 + b, n_heads + h)),
                  pl.BlockSpec((n_ctx, LANES), lambda b, h, i: (b, 2 * n_heads + h)),
                  pl.BlockSpec((seq, LANES), lambda b, h, i: (lat0 + b, 2 * n_heads + h)),
                  tab, tab, tab,
                  pl.BlockSpec((4, DIFF_QK_DIM), lambda b, h, i: (0, 0)),
                  pl.BlockSpec((1, LANES), lambda b, h, i: (0, 0))],
        out_specs=pl.BlockSpec((tq, LANES), lambda b, h, i: (qb(b, i), h)),
        out_shape=jax.ShapeDtypeStruct((t, n_heads * LANES), BF16),
        scratch_shapes=[pltpu.VMEM((n_ctx + seq, LANES), BF16), pltpu.VMEM((n_ctx + seq, LANES), BF16)],
        compiler_params=_cparams(("arbitrary", "arbitrary", "arbitrary")),
        name="diff_attn",
    )(proj, proj, proj, proj, proj, cos, su, sd, lam_params, subln_g.reshape(1, LANES))


def _softmax_pv(q, k, v):
    p = _softmax_rows(_dot_nt(q, k)).astype(BF16)
    return _dot(p, v)


def _gqa_attn_kernel(q_ref, kc_ref, kl_ref, vc_ref, vl_ref, cos_ref, su_ref, sd_ref, qg_ref, kg_ref,
                     o_ref, k_scr, v_scr, *, n_ctx, tq, shift):
    g = pl.program_id(2)
    i = pl.program_id(3)

    @pl.when((g == 0) & (i == 0))
    def _prep():
        kg = kg_ref[...]
        k_scr[0:n_ctx, :] = _rms(kc_ref[...].astype(F32), kg, 1e-6).astype(BF16)
        kl = _rms(kl_ref[...].astype(F32), kg, 1e-6)
        k_scr[n_ctx:, :] = _rope(kl, cos_ref[...], su_ref[...], sd_ref[...], shift).astype(BF16)
        v_scr[0:n_ctx, :] = vc_ref[...]
        v_scr[n_ctx:, :] = vl_ref[...]

    scale = HEAD_DIM ** -0.5
    qn = _rms(q_ref[...].astype(F32), qg_ref[...], 1e-6)

    @pl.when(i == 0)
    def _ctx_queries():
        q = (qn * scale).astype(BF16)
        o_ref[...] = _softmax_pv(q, k_scr[0:n_ctx, :], v_scr[0:n_ctx, :]).astype(o_ref.dtype)

    @pl.when(i > 0)
    def _lat_queries():
        r0 = pl.multiple_of((i - 1) * tq, tq)
        q = _rope(qn, cos_ref[pl.ds(r0, tq), :], su_ref[pl.ds(r0, tq), :], sd_ref[pl.ds(r0, tq), :], shift)
        q = (q * scale).astype(BF16)
        o_ref[...] = _softmax_pv(q, k_scr[...], v_scr[...]).astype(o_ref.dtype)


def _gqa_attn(proj, tables, q_norm_g, k_norm_g, col0, bsz, n_ctx, seq, n_heads, n_kv, tq=256):
    t = proj.shape[0]
    cos, su, sd, shift = tables
    grp = n_heads // n_kv
    qb = functools.partial(_q_block, bsz=bsz, n_ctx=n_ctx, seq=seq, tq=tq)
    lat0 = bsz * n_ctx // seq
    kcol = col0 + n_heads
    vcol = kcol + n_kv
    kern = functools.partial(_gqa_attn_kernel, n_ctx=n_ctx, tq=tq, shift=shift)
    tab = pl.BlockSpec((seq, LANES), lambda b, h, g, i: (0, 0))
    vec = pl.BlockSpec((1, LANES), lambda b, h, g, i: (0, 0))
    return pl.pallas_call(
        kern,
        grid=(bsz, n_kv, grp, 1 + seq // tq),
        in_specs=[pl.BlockSpec((tq, LANES), lambda b, h, g, i: (qb(b, i), col0 + h * grp + g)),
                  pl.BlockSpec((n_ctx, LANES), lambda b, h, g, i: (b, kcol + h)),
                  pl.BlockSpec((seq, LANES), lambda b, h, g, i: (lat0 + b, kcol + h)),
                  pl.BlockSpec((n_ctx, LANES), lambda b, h, g, i: (b, vcol + h)),
                  pl.BlockSpec((seq, LANES), lambda b, h, g, i: (lat0 + b, vcol + h)),
                  tab, tab, tab, vec, vec],
        out_specs=pl.BlockSpec((tq, LANES), lambda b, h, g, i: (qb(b, i), h * grp + g)),
        out_shape=jax.ShapeDtypeStruct((t, n_heads * LANES), BF16),
        scratch_shapes=[pltpu.VMEM((n_ctx + seq, LANES), BF16), pltpu.VMEM((n_ctx + seq, LANES), BF16)],
        compiler_params=_cparams(("arbitrary", "arbitrary", "arbitrary", "arbitrary")),
        name="gqa_attn",
    )(proj, proj, proj, proj, proj, cos, su, sd, q_norm_g.reshape(1, LANES), k_norm_g.reshape(1, LANES))


def _mla_attn_kernel(qn_ref, qp_ref, knc_ref, knl_ref, kpc_ref, kpl_ref, vc_ref, vl_ref,
                     cos_ref, su_ref, sd_ref, o_ref, k_scr, v_scr, *, n_ctx, tq, shift):
    i = pl.program_id(2)

    @pl.when(i == 0)
    def _prep():
        k_scr[0:n_ctx, 0:LANES] = knc_ref[...]
        k_scr[n_ctx:, 0:LANES] = knl_ref[...]
        k_scr[0:n_ctx, LANES:] = kpc_ref[...].astype(BF16)
        k_scr[n_ctx:, LANES:] = _rope(kpl_ref[...], cos_ref[...], su_ref[...], sd_ref[...], shift).astype(BF16)
        v_scr[0:n_ctx, :] = vc_ref[...]
        v_scr[n_ctx:, :] = vl_ref[...]

    scale = (MLA_NOPE_DIM + MLA_ROPE_DIM) ** -0.5
    r0 = pl.multiple_of(i * tq, tq)
    qp = _rope(qp_ref[...].astype(F32), cos_ref[pl.ds(r0, tq), :], su_ref[pl.ds(r0, tq), :],
               sd_ref[pl.ds(r0, tq), :], shift)
    q = jnp.concatenate([(qn_ref[...].astype(F32) * scale).astype(BF16), (qp * scale).astype(BF16)], axis=-1)
    o_ref[...] = _softmax_pv(q, k_scr[...], v_scr[...]).astype(o_ref.dtype)


def _mla_attn(q, kv, down, tables, bsz, n_ctx, seq, n_heads, kpe_col, tq=256):
    cos, su, sd, shift = tables
    lat0 = bsz * n_ctx // seq
    nq = seq // tq
    kern = functools.partial(_mla_attn_kernel, n_ctx=n_ctx, tq=tq, shift=shift)
    tab = pl.BlockSpec((seq, LANES), lambda b, h, i: (0, 0))
    return pl.pallas_call(
        kern,
        grid=(bsz, n_heads, nq),
        in_specs=[pl.BlockSpec((tq, LANES), lambda b, h, i: (b * nq + i, h)),
                  pl.BlockSpec((tq, LANES), lambda b, h, i: (b * nq + i, n_heads + h)),
                  pl.BlockSpec((n_ctx, LANES), lambda b, h, i: (b, 2 * h)),
                  pl.BlockSpec((seq, LANES), lambda b, h, i: (lat0 + b, 2 * h)),
                  pl.BlockSpec((n_ctx, LANES), lambda b, h, i: (b, kpe_col)),
                  pl.BlockSpec((seq, LANES), lambda b, h, i: (lat0 + b, kpe_col)),
                  pl.BlockSpec((n_ctx, LANES), lambda b, h, i: (b, 2 * h + 1)),
                  pl.BlockSpec((seq, LANES), lambda b, h, i: (lat0 + b, 2 * h + 1)),
                  tab, tab, tab],
        out_specs=pl.BlockSpec((tq, LANES), lambda b, h, i: (b * nq + i, h)),
        out_shape=jax.ShapeDtypeStruct((bsz * seq, n_heads * LANES), BF16),
        scratch_shapes=[pltpu.VMEM((n_ctx + seq, 2 * LANES), BF16), pltpu.VMEM((n_ctx + seq, LANES), BF16)],
        compiler_params=_cparams(("arbitrary", "arbitrary", "arbitrary")),
        name="mla_attn",
    )(q, q, kv, kv, down, down, kv, kv, cos, su, sd)


def _layer_norm(z, g, b):
    zc = z - jnp.mean(z, axis=-1, keepdims=True)
    var = jnp.mean(zc * zc, axis=-1, keepdims=True)
    return zc * lax.rsqrt(var + 1e-5) * g + b


def _resln_kernel(*refs, n_act, alpha):
    x_ref, gate_ref = refs[0], refs[1]
    acts = refs[2:2 + n_act]
    ws = refs[2 + n_act:2 + 2 * n_act]
    lng_ref, lnb_ref, o_ref = refs[2 + 2 * n_act:]
    y = _dot(acts[0][...], ws[0][...])
    for a_ref, w_ref in zip(acts[1:], ws[1:]):
        y = y + _dot(a_ref[...], w_ref[...])
    z = alpha * x_ref[...] + gate_ref[0] * y
    o_ref[...] = _layer_norm(z, lng_ref[...], lnb_ref[...])


def _resln(x, mod, gate_idx, acts, ws, ln_g, ln_b, alpha, geom, row_off_blocks, n_rows, act_off_blocks, tm=512):
    d = x.shape[1]
    row = functools.partial(_mod_row, tm=tm, **geom)
    n_act = len(acts)
    in_specs = [pl.BlockSpec((tm, d), lambda i: (i + row_off_blocks, 0)),
                pl.BlockSpec((1, 1, d), lambda i: (row(i + row_off_blocks), 0, gate_idx))]
    in_specs += [pl.BlockSpec((tm, a.shape[1]), lambda i: (i + act_off_blocks, 0)) for a in acts]
    in_specs += [pl.BlockSpec(w.shape, lambda i: (0, 0)) for w in ws]
    in_specs += [pl.BlockSpec((1, d), lambda i: (0, 0))] * 2
    return pl.pallas_call(
        functools.partial(_resln_kernel, n_act=n_act, alpha=alpha),
        grid=(n_rows // tm,),
        in_specs=in_specs,
        out_specs=pl.BlockSpec((tm, d), lambda i: (i, 0)),
        out_shape=jax.ShapeDtypeStruct((n_rows, d), F32),
        compiler_params=_cparams(("arbitrary",)),
        name="proj_resln",
    )(x, mod, *acts, *ws, ln_g.reshape(1, d), ln_b.reshape(1, d))


def _pack_halves(x):
    n = x.shape[1] // 2
    bits = lax.bitcast_convert_type(x.astype(BF16).astype(F32), jnp.uint32)
    return (bits[:, :n] >> 16) | (bits[:, n:] & jnp.uint32(0xFFFF0000))


def _unpack_halves(w):
    lo = lax.bitcast_convert_type(w << 16, F32)
    hi = lax.bitcast_convert_type(w & jnp.uint32(0xFFFF0000), F32)
    return lo, hi


ROW_TILE = 8


def _store_token_tiles(ref, packed):
    m = packed.shape[0]
    for s in range(ROW_TILE):
        ref[pl.ds(s, m, stride=ROW_TILE), :] = packed[:, s * LANES:(s + 1) * LANES]


def _load_token_tiles(ref, m):
    return jnp.concatenate([ref[pl.ds(s, m, stride=ROW_TILE), :] for s in range(ROW_TILE)], axis=-1)


def _token_tile(ref, tok):
    return ref.at[pl.ds(pl.multiple_of(tok * ROW_TILE, ROW_TILE), ROW_TILE)]


def _pick_first_max(cur, idx, axes, sentinel):
    m = cur
    for ax in axes:
        m = jnp.max(m, axis=ax, keepdims=True)
    first = jnp.where(cur == m, idx, sentinel)
    for ax in axes:
        first = jnp.min(first, axis=ax, keepdims=True)
    return m, first


def _route_kernel(x_ref, sc_ref, sh_ref, rwt_ref, bias_ref, tri_ref, h_ref, e_ref, g_ref, r_ref, cnt_ref, run_ref):
    tm = x_ref.shape[0]
    n_members = N_EXPERTS // N_GROUPS

    @pl.when(pl.program_id(0) == 0)
    def _():
        run_ref[...] = jnp.zeros_like(run_ref)

    h = x_ref[...] * (1.0 + sc_ref[0]) + sh_ref[0]
    _store_token_tiles(h_ref, _pack_halves(h))
    logits = lax.dot_general(rwt_ref[...], h, (((1,), (1,)), ((), ())), precision=lax.Precision.HIGHEST,
                             preferred_element_type=F32)
    scores = _sigmoid(logits).reshape(N_GROUPS, n_members, tm)
    biased = scores + bias_ref[...]
    neg = -jnp.inf
    member = lax.broadcasted_iota(jnp.int32, biased.shape, 1).astype(F32)
    group = lax.broadcasted_iota(jnp.int32, biased.shape, 0).astype(F32)
    expert = group * n_members + member

    m1, first = _pick_first_max(biased, member, (1,), float(n_members))
    m2 = jnp.max(jnp.where(member == first, neg, biased), axis=1, keepdims=True)
    gscore = m1 + m2
    gidx = lax.broadcasted_iota(jnp.int32, gscore.shape, 0).astype(F32)
    group_ok = jnp.zeros(gscore.shape, jnp.bool_)
    for _ in range(TOPK_GROUPS):
        _, first = _pick_first_max(gscore, gidx, (0,), float(N_GROUPS))
        pick = gidx == first
        group_ok = group_ok | pick
        gscore = jnp.where(pick, neg, gscore)

    cur = jnp.where(group_ok, biased, neg)
    chosen = jnp.zeros(biased.shape, jnp.bool_)
    top_e, gates = [], []
    for _ in range(TOP_K):
        _, first = _pick_first_max(cur, expert, (0, 1), float(N_EXPERTS))
        pick = expert == first
        chosen = chosen | pick
        cur = jnp.where(pick, neg, cur)
        top_e.append(first)
        gates.append(jnp.sum(jnp.sum(jnp.where(pick, scores, 0.0), axis=0, keepdims=True), axis=1, keepdims=True))
    gsum = gates[0]
    for gk in gates[1:]:
        gsum = gsum + gk
    norm = ROUTE_SCALE / gsum

    chosen2d = jnp.where(chosen, 1.0, 0.0).reshape(N_EXPERTS, tm)
    before = _dot(chosen2d.astype(BF16), tri_ref[...])
    rank = (run_ref[...] + before).reshape(N_GROUPS, n_members, tm)
    run_ref[...] = run_ref[...] + jnp.sum(chosen2d, axis=1, keepdims=True)
    cnt_ref[...] = run_ref[...].astype(jnp.int32)
    for k in range(TOP_K):
        pick = expert == top_e[k]
        rk = jnp.sum(jnp.sum(jnp.where(pick, rank, 0.0), axis=0, keepdims=True), axis=1, keepdims=True)
        e_ref[k:k + 1, :] = top_e[k].reshape(1, tm).astype(jnp.int32)
        g_ref[k:k + 1, :] = (gates[k] * norm).reshape(1, tm)
        r_ref[k:k + 1, :] = rk.reshape(1, tm).astype(jnp.int32)


def _route(x, mod, sc_idx, sh_idx, router_w, router_bias, geom, tm=512):
    t, d = x.shape
    row = functools.partial(_mod_row, tm=tm, **geom)
    tri = (np.arange(tm)[:, None] < np.arange(tm)[None, :]).astype(np.float32)
    kt = pl.BlockSpec((TOP_K, tm), lambda i: (0, i))
    return pl.pallas_call(
        _route_kernel,
        grid=(t // tm,),
        in_specs=[pl.BlockSpec((tm, d), lambda i: (i, 0)),
                  pl.BlockSpec((1, 1, d), lambda i: (row(i), 0, sc_idx)),
                  pl.BlockSpec((1, 1, d), lambda i: (row(i), 0, sh_idx)),
                  pl.BlockSpec((N_EXPERTS, d), lambda i: (0, 0)),
                  pl.BlockSpec((N_GROUPS, N_EXPERTS // N_GROUPS, 1), lambda i: (0, 0, 0)),
                  pl.BlockSpec((tm, tm), lambda i: (0, 0))],
        out_specs=[pl.BlockSpec((tm * ROW_TILE, LANES), lambda i: (i, 0)), kt, kt, kt,
                   pl.BlockSpec((N_EXPERTS, 1), lambda i: (0, 0))],
        out_shape=[jax.ShapeDtypeStruct((t * ROW_TILE, LANES), jnp.uint32),
                   jax.ShapeDtypeStruct((TOP_K, t), jnp.int32),
                   jax.ShapeDtypeStruct((TOP_K, t), F32),
                   jax.ShapeDtypeStruct((TOP_K, t), jnp.int32),
                   jax.ShapeDtypeStruct((N_EXPERTS, 1), jnp.int32)],
        scratch_shapes=[pltpu.VMEM((N_EXPERTS, 1), F32)],
        compiler_params=_cparams(("arbitrary",)),
        name="moe_route",
    )(x, mod, mod, router_w.T, router_bias.astype(F32).reshape(N_GROUPS, N_EXPERTS // N_GROUPS, 1),
      jnp.asarray(tri, BF16))


def _dispatch_plan(top_e, rank, counts, n_blocks):
    counts = counts.reshape(N_EXPERTS)
    padded = (counts + MOE_BLOCK - 1) // MOE_BLOCK * MOE_BLOCK
    pad_ends = jnp.cumsum(padded)
    pad_starts = pad_ends - padded
    onehot = top_e[:, :, None] == jnp.arange(N_EXPERTS, dtype=jnp.int32)
    dest = jnp.sum(jnp.where(onehot, pad_starts, 0), axis=-1) + rank
    block_start = jnp.arange(n_blocks, dtype=jnp.int32) * MOE_BLOCK
    block_e = jnp.minimum(jnp.sum(pad_ends[None, :] <= block_start[:, None], axis=1), N_EXPERTS - 1)
    n_used = (pad_ends[-1] // MOE_BLOCK).astype(jnp.int32).reshape(1)
    return dest, pad_ends.astype(jnp.int32), block_e.astype(jnp.int32), n_used


def _dest_tiles(dest, tm):
    k, t = dest.shape
    return dest.reshape(k, t // tm, tm).transpose(1, 0, 2).reshape(t // tm, 1, k * tm)


def _dispatch_kernel(pe_ref, nu_ref, dest_ref, h_ref, xs_ref, dest_smem, zero_ref, sems, *, tm, n_blocks):
    i = pl.program_id(0)

    def zero_fill(slot0):
        rows = MOE_BLOCK * ROW_TILE
        row0 = slot0 * ROW_TILE if isinstance(slot0, int) else pl.multiple_of(slot0 * ROW_TILE, rows)
        return pltpu.make_async_copy(zero_ref, xs_ref.at[pl.ds(row0, rows)], sems.at[1])

    @pl.when(i == 0)
    def _():
        zero_ref[...] = jnp.zeros_like(zero_ref)
        for e in range(N_EXPERTS):
            start = pe_ref[e - 1] if e else 0

            @pl.when(pe_ref[e] > start)
            def _():
                zero_fill(pe_ref[e] - MOE_BLOCK).start()

        def start_unused(b, c):
            zero_fill(b * MOE_BLOCK).start()
            return c

        lax.fori_loop(nu_ref[0], n_blocks, start_unused, 0)
        for e in range(N_EXPERTS):
            start = pe_ref[e - 1] if e else 0

            @pl.when(pe_ref[e] > start)
            def _():
                zero_fill(0).wait()

        def wait_unused(b, c):
            zero_fill(0).wait()
            return c

        lax.fori_loop(nu_ref[0], n_blocks, wait_unused, 0)

    load = pltpu.make_async_copy(dest_ref.at[0, 0], dest_smem, sems.at[0])
    load.start()
    load.wait()

    def issue(t, c):
        for k in range(TOP_K):
            d = dest_smem[k * tm + t]
            pltpu.make_async_copy(_token_tile(h_ref, t), _token_tile(xs_ref, d), sems.at[2]).start()
        return c

    lax.fori_loop(0, tm, issue, 0)
    for k in range(TOP_K):
        pltpu.make_async_copy(h_ref, xs_ref.at[pl.ds(0, tm * ROW_TILE)], sems.at[2]).wait()


def _dispatch(h, dest, pad_ends, n_used, n_blocks, tm=512):
    t = h.shape[0] // ROW_TILE
    grid_spec = pltpu.PrefetchScalarGridSpec(
        num_scalar_prefetch=2,
        grid=(t // tm,),
        in_specs=[pl.BlockSpec((1, 1, TOP_K * tm), lambda i, pe, nu: (i, 0, 0)),
                  pl.BlockSpec((tm * ROW_TILE, LANES), lambda i, pe, nu: (i, 0))],
        out_specs=pl.BlockSpec(memory_space=pl.ANY),
        scratch_shapes=[pltpu.SMEM((TOP_K * tm,), jnp.int32),
                        pltpu.VMEM((MOE_BLOCK * ROW_TILE, LANES), jnp.uint32),
                        pltpu.SemaphoreType.DMA((3,))],
    )
    return pl.pallas_call(
        functools.partial(_dispatch_kernel, tm=tm, n_blocks=n_blocks),
        grid_spec=grid_spec,
        out_shape=jax.ShapeDtypeStruct((n_blocks * MOE_BLOCK * ROW_TILE, LANES), jnp.uint32),
        compiler_params=_cparams(("arbitrary",)),
        name="moe_dispatch",
    )(pad_ends, n_used, _dest_tiles(dest, tm), h)


def _experts_kernel(be_ref, nu_ref, x_ref, wg_ref, wu_ref, wo_ref, o_ref, wg_scr, wu_scr, wo_scr):
    blk = pl.program_id(0)
    prev = be_ref[jnp.maximum(blk - 1, 0)]
    half = ROW_TILE * LANES

    @pl.when((blk == 0) | (be_ref[blk] != prev))
    def _():
        wg_scr[...] = wg_ref[0, 0].astype(BF16)
        wu_scr[...] = wu_ref[0, 0].astype(BF16)
        wo_scr[...] = wo_ref[0, 0].astype(BF16)

    @pl.when(blk < nu_ref[0])
    def _():
        lo, hi = _unpack_halves(_load_token_tiles(x_ref, MOE_BLOCK))
        lo, hi = lo.astype(BF16), hi.astype(BF16)
        gate = _dot(lo, wg_scr[0:half, :]) + _dot(hi, wg_scr[half:, :])
        up = _dot(lo, wu_scr[0:half, :]) + _dot(hi, wu_scr[half:, :])
        a = (gate * _sigmoid(gate) * up).astype(BF16)
        _store_token_tiles(o_ref, _pack_halves(_dot(a, wo_scr[...])))

    @pl.when(blk >= nu_ref[0])
    def _():
        o_ref[...] = jnp.zeros_like(o_ref)


def _experts(xs, block_e, n_used, w_in, w_out, layer):
    d = w_in.shape[2]
    f = w_out.shape[2]
    assert d == 2 * ROW_TILE * LANES
    n_blocks = xs.shape[0] // (MOE_BLOCK * ROW_TILE)
    grid_spec = pltpu.PrefetchScalarGridSpec(
        num_scalar_prefetch=2,
        grid=(n_blocks,),
        in_specs=[pl.BlockSpec((MOE_BLOCK * ROW_TILE, LANES), lambda i, be, nu: (i, 0)),
                  pl.BlockSpec((1, 1, d, f), lambda i, be, nu: (layer, be[i], 0, 0)),
                  pl.BlockSpec((1, 1, d, f), lambda i, be, nu: (layer, be[i], 0, 1)),
                  pl.BlockSpec((1, 1, f, d), lambda i, be, nu: (layer, be[i], 0, 0))],
        out_specs=pl.BlockSpec((MOE_BLOCK * ROW_TILE, LANES), lambda i, be, nu: (i, 0)),
        scratch_shapes=[pltpu.VMEM((d, f), BF16), pltpu.VMEM((d, f), BF16), pltpu.VMEM((f, d), BF16)],
    )
    return pl.pallas_call(
        _experts_kernel,
        grid_spec=grid_spec,
        out_shape=jax.ShapeDtypeStruct(xs.shape, jnp.uint32),
        compiler_params=_cparams(("arbitrary",)),
        name="moe_experts",
    )(block_e, n_used, xs, w_in, w_in, w_out)


def _ffn_out_kernel(dest_ref, x_ref, gate_ref, h_ref, rg_ref, wg_ref, wu_ref, wo_ref, lng_ref, lnb_ref, ys_ref,
                    o_ref, dest_smem, rows_ref, sems, *, alpha, tm):
    half = ROW_TILE * LANES
    load = pltpu.make_async_copy(dest_ref.at[0, 0], dest_smem, sems.at[0])
    load.start()
    load.wait()

    def issue(t, c):
        for k in range(TOP_K):
            d = dest_smem[k * tm + t]
            pltpu.make_async_copy(_token_tile(ys_ref, d), _token_tile(rows_ref.at[k], t), sems.at[1]).start()
        return c

    lax.fori_loop(0, tm, issue, 0)

    lo, hi = _unpack_halves(_load_token_tiles(h_ref, tm))
    lo, hi = lo.astype(BF16), hi.astype(BF16)
    gate = _dot(lo, wg_ref[0:half, :]) + _dot(hi, wg_ref[half:, :])
    up = _dot(lo, wu_ref[0:half, :]) + _dot(hi, wu_ref[half:, :])
    a = (gate * _sigmoid(gate) * up).astype(BF16)
    y = _dot(a, wo_ref[...])
    y_lo, y_hi = y[:, :half], y[:, half:]

    rg = rg_ref[...]
    for k in range(TOP_K):
        pltpu.make_async_copy(ys_ref.at[pl.ds(0, tm * ROW_TILE)], rows_ref.at[k], sems.at[1]).wait()
    for k in range(TOP_K):
        r_lo, r_hi = _unpack_halves(_load_token_tiles(rows_ref.at[k], tm))
        y_lo = y_lo + r_lo * rg[:, k:k + 1]
        y_hi = y_hi + r_hi * rg[:, k:k + 1]
    z = alpha * x_ref[...] + gate_ref[0] * jnp.concatenate([y_lo, y_hi], axis=-1)
    o_ref[...] = _layer_norm(z, lng_ref[...], lnb_ref[...])


def _ffn_out(x, mod, gate_idx, h, ys, dest, route_gate, sh_in, sh_out, ln_g, ln_b, alpha, geom, tm=256):
    t, d = x.shape
    f = sh_out.shape[0]
    assert d == 2 * ROW_TILE * LANES
    row = functools.partial(_mod_row, tm=tm, **geom)
    return pl.pallas_call(
        functools.partial(_ffn_out_kernel, alpha=alpha, tm=tm),
        grid=(t // tm,),
        in_specs=[pl.BlockSpec((1, 1, TOP_K * tm), lambda i: (i, 0, 0)),
                  pl.BlockSpec((tm, d), lambda i: (i, 0)),
                  pl.BlockSpec((1, 1, d), lambda i: (row(i), 0, gate_idx)),
                  pl.BlockSpec((tm * ROW_TILE, LANES), lambda i: (i, 0)),
                  pl.BlockSpec((tm, TOP_K), lambda i: (i, 0)),
                  pl.BlockSpec((d, f), lambda i: (0, 0)),
                  pl.BlockSpec((d, f), lambda i: (0, 1)),
                  pl.BlockSpec((f, d), lambda i: (0, 0)),
                  pl.BlockSpec((1, d), lambda i: (0, 0)),
                  pl.BlockSpec((1, d), lambda i: (0, 0)),
                  pl.BlockSpec(memory_space=pl.ANY)],
        out_specs=pl.BlockSpec((tm, d), lambda i: (i, 0)),
        out_shape=jax.ShapeDtypeStruct((t, d), F32),
        scratch_shapes=[pltpu.SMEM((TOP_K * tm,), jnp.int32),
                        pltpu.VMEM((TOP_K, tm * ROW_TILE, LANES), jnp.uint32),
                        pltpu.SemaphoreType.DMA((2,))],
        compiler_params=_cparams(("arbitrary",)),
        name="ffn_out",
    )(_dest_tiles(dest, tm), x, mod, h, route_gate, sh_in, sh_in, sh_out, ln_g.reshape(1, d), ln_b.reshape(1, d), ys)


def _moe_sublayer(x, mod, geom, layer, router_w, router_bias, w_in, w_out, sh_in, sh_out, ln_g, ln_b, alpha):
    h, top_e, gate, rank, counts = _route(x, mod, 4, 3, router_w, router_bias, geom)
    n_blocks = x.shape[0] * TOP_K // MOE_BLOCK + N_EXPERTS
    dest, pad_ends, block_e, n_used = _dispatch_plan(top_e, rank, counts, n_blocks)
    xs = _dispatch(h, dest, pad_ends, n_used, n_blocks)
    ys = _experts(xs, block_e, n_used, w_in, w_out, layer)
    return _ffn_out(x, mod, 5, h, ys, dest, gate.T, sh_in.astype(BF16), sh_out.astype(BF16), ln_g, ln_b, alpha,
                    geom)


def kernel(x, c, ctx, c_ctx, ada_w, ada_b, ln_mix_g, ln_mix_b, ln_ffn_g, ln_ffn_b, ab_w_in, ab_w_out, diff_lambda, diff_subln_g, gqa_q_norm_g, gqa_k_norm_g, mla_w_down, mla_q_norm_g, mla_w_uq, mla_kv_norm_g, mla_w_ukv, mla_w_o, router_w, router_bias, expert_w_in, expert_w_out, shared_w_in, shared_w_out):
    bsz, seq, d = x.shape
    n_ctx = ctx.shape[1]
    depth = ada_w.shape[0]
    assert depth == 2 and seq % GRID_W == 0 and bsz + 1 <= MOD_ROWS
    alpha = (2 * depth) ** 0.25
    n_ctx_rows = bsz * n_ctx
    n_lat_rows = bsz * seq
    geom_all = dict(n_ctx_rows=n_ctx_rows, seq=seq, ctx_row=bsz)
    geom_lat = dict(n_ctx_rows=0, seq=seq, ctx_row=bsz)

    tables64 = _rope_tables(seq // GRID_W, 64)
    tables128 = _rope_tables(seq // GRID_W, HEAD_DIM)

    cvec = jnp.zeros((MOD_ROWS, d), F32).at[:bsz].set(c).at[bsz].set(c_ctx)
    mods = _ada_mod(cvec, ada_w, ada_b)
    mods = mods.reshape(depth, MOD_ROWS, 1, 6 * d)

    xt = jnp.concatenate([ctx.reshape(n_ctx_rows, d), x.reshape(n_lat_rows, d)], axis=0)

    mod = mods[0]
    n_diff = d // (2 * HEAD_DIM)
    n_gqa = d // (2 * HEAD_DIM)
    n_gqa_kv = n_gqa // 4
    proj = _modproj(xt, mod, 1, 0, ab_w_in[0].astype(BF16), BF16, geom_all)
    lam_init = 0.8 - 0.6 * math.exp(-0.3 * 0)
    oa = _diff_attn(proj, tables64, diff_lambda[0], diff_subln_g[0], lam_init, bsz, n_ctx, seq, n_diff)
    ob = _gqa_attn(proj, tables128, gqa_q_norm_g[0], gqa_k_norm_g[0], 3 * n_diff, bsz, n_ctx, seq, n_gqa, n_gqa_kv)
    w_out = ab_w_out[0].astype(BF16)
    wa, wb = w_out[:n_diff * HEAD_DIM], w_out[n_diff * HEAD_DIM:]
    t_all = n_ctx_rows + n_lat_rows
    xt = _resln(xt, mod, 2, [oa, ob], [wa, wb], ln_mix_g[0], ln_mix_b[0], alpha, geom_all, 0, t_all, 0)
    xt = _moe_sublayer(xt, mod, geom_all, 0, router_w[0], router_bias[0], expert_w_in, expert_w_out,
                       shared_w_in[0], shared_w_out[0], ln_ffn_g[0], ln_ffn_b[0], alpha)

    mod = mods[1]
    n_mla = d // HEAD_DIM
    qk = MLA_NOPE_DIM + MLA_ROPE_DIM
    w_down = jnp.pad(mla_w_down[0], ((0, 0), (0, LANES - MLA_ROPE_DIM))).astype(BF16)
    down = _modproj(xt, mod, 1, 0, w_down, F32, geom_all)
    w_uq = mla_w_uq[0].reshape(MLA_Q_RANK, n_mla, qk)
    w_uq_rope = jnp.pad(w_uq[:, :, MLA_NOPE_DIM:], ((0, 0), (0, 0), (0, LANES - MLA_ROPE_DIM)))
    w_uq = jnp.concatenate([w_uq[:, :, :MLA_NOPE_DIM].reshape(MLA_Q_RANK, -1),
                            w_uq_rope.reshape(MLA_Q_RANK, -1)], axis=1).astype(BF16)
    lat_blocks = n_ctx_rows // 1024
    q = _normproj(down, 0, mla_q_norm_g[0], w_uq, lat_blocks, n_lat_rows)
    kv = _normproj(down, 1, mla_kv_norm_g[0], mla_w_ukv[0].astype(BF16), 0, t_all)
    kpe_col = (MLA_Q_RANK + MLA_KV_RANK) // LANES
    o = _mla_attn(q, kv, down, tables64, bsz, n_ctx, seq, n_mla, kpe_col)
    xl = _resln(xt, mod, 2, [o], [mla_w_o[0].astype(BF16)], ln_mix_g[1], ln_mix_b[1], alpha, geom_all,
                n_ctx_rows // 512, n_lat_rows, 0)
    xl = _moe_sublayer(xl, mod, geom_lat, 1, router_w[1], router_bias[1], expert_w_in, expert_w_out,
                       shared_w_in[1], shared_w_out[1], ln_ffn_g[1], ln_ffn_b[1], alpha)
    return xl.reshape(bsz, seq, d)
```

```python
import functools
import math

import numpy as np
import jax
import jax.numpy as jnp
from jax import lax
from jax.experimental import pallas as pl
from jax.experimental.pallas import tpu as pltpu

F32 = jnp.float32
BF16 = jnp.bfloat16

GRID_W = 64
ROPE_THETA = 10000.0
HEAD_DIM = 128
DIFF_QK_DIM = 64
N_EXPERTS = 64
EXPERT_DIM = 512
TOP_K = 8
N_GROUPS = 8
TOPK_GROUPS = 4
ROUTE_SCALE = 2.5
MLA_Q_RANK = 512
MLA_KV_RANK = 512
MLA_NOPE_DIM = 128
MLA_ROPE_DIM = 64
MLA_V_DIM = 128
LANES = 128

MOD_ROWS = 16
VMEM_LIMIT = 56 * 1024 * 1024
MOE_BLOCK = 256


def _cparams(sem):
    return pltpu.CompilerParams(dimension_semantics=sem, vmem_limit_bytes=VMEM_LIMIT)


def _dot(a, b):
    return jnp.dot(a, b, preferred_element_type=F32)


def _dot_nt(a, b):
    return lax.dot_general(a, b, (((1,), (1,)), ((), ())), preferred_element_type=F32)


def _dot_hi(a, b):
    return lax.dot_general(a, b, (((1,), (0,)), ((), ())), precision=lax.Precision.HIGHEST,
                           preferred_element_type=F32)


def _sigmoid(x):
    return 1.0 / (1.0 + jnp.exp(-x))


def _rope(x, cos, sin_up, sin_dn, shift):
    return (x * cos + pltpu.roll(x, LANES - shift, 1) * sin_up + pltpu.roll(x, shift, 1) * sin_dn)


def _rope_tables(rows, rot_dim):
    t = jnp.arange(rows * GRID_W)
    row = (t // GRID_W).astype(F32)
    col = (t % GRID_W).astype(F32)
    axis_dim = rot_dim // 2
    quarter = rot_dim // 4
    inv_freq = 1.0 / (ROPE_THETA ** (jnp.arange(0, axis_dim, 2, dtype=F32) / axis_dim))
    lane = np.arange(LANES) % rot_dim
    is_col = lane >= axis_dim
    within = lane % axis_dim
    first = within < quarter
    freq = within % quarter
    ang = jnp.where(jnp.asarray(is_col)[None, :], col[:, None], row[:, None]) * inv_freq[freq][None, :]
    cos = jnp.cos(ang).astype(F32)
    sin = jnp.sin(ang).astype(F32)
    first = jnp.asarray(first)[None, :]
    sin_up = jnp.where(first, -sin, 0.0)
    sin_dn = jnp.where(first, 0.0, sin)
    return cos, sin_up, sin_dn, quarter


LOG2E = 1.4426950408889634
HEADS_PER_STEP = 2


def _exp2_rows(s):
    e = jnp.exp2(s - jnp.max(s, axis=-1, keepdims=True))
    return e, 1.0 / jnp.sum(e, axis=-1, keepdims=True)


def _rms(x, g, eps):
    return x * lax.rsqrt(jnp.mean(x * x, axis=-1, keepdims=True) + eps) * g


def _ada_kernel(c_ref, w_ref, b_ref, o_ref):
    c = c_ref[...]
    act = c * _sigmoid(c)
    o_ref[0] = _dot_hi(act, w_ref[0]) + b_ref[0]


def _ada_mod(cvec, ada_w, ada_b):
    depth, d, n = ada_w.shape
    tn = 1536
    return pl.pallas_call(
        _ada_kernel,
        grid=(depth, n // tn),
        in_specs=[pl.BlockSpec((MOD_ROWS, d), lambda l, j: (0, 0)),
                  pl.BlockSpec((1, d, tn), lambda l, j: (l, 0, j)),
                  pl.BlockSpec((1, 1, tn), lambda l, j: (l, 0, j))],
        out_specs=pl.BlockSpec((1, MOD_ROWS, tn), lambda l, j: (l, 0, j)),
        out_shape=jax.ShapeDtypeStruct((depth, MOD_ROWS, n), F32),
        compiler_params=_cparams(("arbitrary", "arbitrary")),
        name="ada_mod",
    )(cvec, ada_w, ada_b.reshape(depth, 1, n))


def _mod_row(i, tm, n_ctx_rows, seq, ctx_row):
    r0 = i * tm
    return jnp.where(r0 < n_ctx_rows, ctx_row, (r0 - n_ctx_rows) // seq)


def _modproj_kernel(x_ref, sc_ref, sh_ref, w_ref, o_ref, h_ref):
    @pl.when(pl.program_id(1) == 0)
    def _():
        h_ref[...] = (x_ref[...] * (1.0 + sc_ref[0]) + sh_ref[0]).astype(h_ref.dtype)

    o_ref[...] = _dot(h_ref[...], w_ref[...]).astype(o_ref.dtype)


def _modproj(x, mod, sc_idx, sh_idx, w, out_dtype, geom, tm=1024, tn=512):
    t, d = x.shape
    n = w.shape[1]
    tn = tn if n % tn == 0 else n
    row = functools.partial(_mod_row, tm=tm, **geom)
    return pl.pallas_call(
        _modproj_kernel,
        grid=(t // tm, n // tn),
        in_specs=[pl.BlockSpec((tm, d), lambda i, j: (i, 0)),
                  pl.BlockSpec((1, 1, d), lambda i, j: (row(i), 0, sc_idx)),
                  pl.BlockSpec((1, 1, d), lambda i, j: (row(i), 0, sh_idx)),
                  pl.BlockSpec((d, tn), lambda i, j: (0, j))],
        out_specs=pl.BlockSpec((tm, tn), lambda i, j: (i, j)),
        out_shape=jax.ShapeDtypeStruct((t, n), out_dtype),
        scratch_shapes=[pltpu.VMEM((tm, d), BF16)],
        compiler_params=_cparams(("arbitrary", "arbitrary")),
        name="modproj",
    )(x, mod, mod, w)


def _normproj_kernel(x_ref, g_ref, w_ref, o_ref, h_ref):
    @pl.when(pl.program_id(1) == 0)
    def _():
        h_ref[...] = _rms(x_ref[...], g_ref[...], 1e-6).astype(h_ref.dtype)

    o_ref[...] = _dot(h_ref[...], w_ref[...]).astype(o_ref.dtype)


def _normproj(x, col_block, g, w, row_off_blocks, n_rows, tm=1024, tn=1024):
    k = w.shape[0]
    n = w.shape[1]
    return pl.pallas_call(
        _normproj_kernel,
        grid=(n_rows // tm, n // tn),
        in_specs=[pl.BlockSpec((tm, k), lambda i, j: (i + row_off_blocks, col_block)),
                  pl.BlockSpec((1, k), lambda i, j: (0, 0)),
                  pl.BlockSpec((k, tn), lambda i, j: (0, j))],
        out_specs=pl.BlockSpec((tm, tn), lambda i, j: (i, j)),
        out_shape=jax.ShapeDtypeStruct((n_rows, n), BF16),
        scratch_shapes=[pltpu.VMEM((tm, k), BF16)],
        compiler_params=_cparams(("arbitrary", "arbitrary")),
        name="normproj",
    )(x, g.reshape(1, k), w)


def _diff_lambda(lam_ref, lam_init):
    lp = lam_ref[...]
    s01 = jnp.sum(lp[0:1] * lp[1:2], axis=-1, keepdims=True)
    s23 = jnp.sum(lp[2:3] * lp[3:4], axis=-1, keepdims=True)
    return jnp.exp(s01) - jnp.exp(s23) + lam_init


def _diff_logits(q, k):
    lane = lax.broadcasted_iota(jnp.int32, q.shape, 1)
    qs = q * (DIFF_QK_DIM ** -0.5 * LOG2E)
    q1 = jnp.where(lane < DIFF_QK_DIM, qs, 0.0).astype(BF16)
    q2 = jnp.where(lane >= DIFF_QK_DIM, qs, 0.0).astype(BF16)
    return _dot_nt(q1, k), _dot_nt(q2, k)


def _diff_combine(s1, s2, v, lam, g, lam_init):
    e1, r1 = _exp2_rows(s1)
    e2, r2 = _exp2_rows(s2)
    a = (e1 * r1 - e2 * (lam * r2)).astype(BF16)
    return _rms(_dot(a, v), g, 1e-5) * (1.0 - lam_init)


def _head_cols(j):
    return slice(j * LANES, (j + 1) * LANES)


def _diff_attn_kernel(q_ref, kc_ref, kl_ref, vc_ref, vl_ref, cos_ref, su_ref, sd_ref, lam_ref, g_ref,
                      o_ref, k_scr, v_scr, *, n_ctx, tq, shift, lam_init):
    i = pl.program_id(2)
    heads = range(HEADS_PER_STEP)

    @pl.when(i == 0)
    def _prep():
        for j in heads:
            k_scr[j, 0:n_ctx, :] = kc_ref[:, _head_cols(j)]
            kl = kl_ref[:, _head_cols(j)].astype(F32)
            k_scr[j, n_ctx:, :] = _rope(kl, cos_ref[...], su_ref[...], sd_ref[...], shift).astype(BF16)
            v_scr[j, 0:n_ctx, :] = vc_ref[:, _head_cols(j)]
            v_scr[j, n_ctx:, :] = vl_ref[:, _head_cols(j)]

    lam = _diff_lambda(lam_ref, lam_init)
    g = g_ref[...]

    def attend(qs, n_keys):
        logits = [_diff_logits(qs[j], k_scr[j, 0:n_keys, :]) for j in heads]
        for j in heads:
            o = _diff_combine(*logits[j], v_scr[j, 0:n_keys, :], lam, g, lam_init)
            o_ref[:, _head_cols(j)] = o.astype(o_ref.dtype)

    @pl.when(i == 0)
    def _ctx_queries():
        attend([q_ref[:, _head_cols(j)].astype(F32) for j in heads], n_ctx)

    @pl.when(i > 0)
    def _lat_queries():
        r0 = pl.multiple_of((i - 1) * tq, tq)
        cos, su, sd = cos_ref[pl.ds(r0, tq), :], su_ref[pl.ds(r0, tq), :], sd_ref[pl.ds(r0, tq), :]
        attend([_rope(q_ref[:, _head_cols(j)].astype(F32), cos, su, sd, shift) for j in heads],
               k_scr.shape[1])


def _q_block(b, i, bsz, n_ctx, seq, tq):
    return jnp.where(i == 0, b * (n_ctx // tq), (bsz * n_ctx + b * seq) // tq + i - 1)


def _diff_attn(proj, tables, lam_params, subln_g, lam_init, bsz, n_ctx, seq, n_heads, tq=256):
    t = proj.shape[0]
    cos, su, sd, shift = tables
    hp = HEADS_PER_STEP
    wide = hp * LANES
    n_pairs = n_heads // hp
    qb = functools.partial(_q_block, bsz=bsz, n_ctx=n_ctx, seq=seq, tq=tq)
    lat0 = bsz * n_ctx // seq
    kern = functools.partial(_diff_attn_kernel, n_ctx=n_ctx, tq=tq, shift=shift, lam_init=lam_init)
    tab = pl.BlockSpec((seq, LANES), lambda b, h, i: (0, 0))
    return pl.pallas_call(
        kern,
        grid=(bsz, n_pairs, 1 + seq // tq),
        in_specs=[pl.BlockSpec((tq, wide), lambda b, h, i: (qb(b, i), h)),
                  pl.BlockSpec((n_ctx, wide), lambda b, h, i: (b, n_pairs + h)),
                  pl.BlockSpec((seq, wide), lambda b, h, i: (lat0 + b, n_pairs + h)),
                  pl.BlockSpec((n_ctx, wide), lambda b, h, i: (b, 2 * n_pairs + h)),
                  pl.BlockSpec((seq, wide), lambda b, h, i: (lat0 + b, 2 * n_pairs + h)),
                  tab, tab, tab,
                  pl.BlockSpec((4, DIFF_QK_DIM), lambda b, h, i: (0, 0)),
                  pl.BlockSpec((1, LANES), lambda b, h, i: (0, 0))],
        out_specs=pl.BlockSpec((tq, wide), lambda b, h, i: (qb(b, i), h)),
        out_shape=jax.ShapeDtypeStruct((t, n_heads * LANES), BF16),
        scratch_shapes=[pltpu.VMEM((hp, n_ctx + seq, LANES), BF16), pltpu.VMEM((hp, n_ctx + seq, LANES), BF16)],
        compiler_params=_cparams(("arbitrary", "arbitrary", "arbitrary")),
        name="diff_attn",
    )(proj, proj, proj, proj, proj, cos, su, sd, lam_params, subln_g.reshape(1, LANES))


def _softmax_pv(s, v):
    e, r = _exp2_rows(s)
    return _dot(e.astype(BF16), v) * r


def _gqa_attn_kernel(q_ref, kc_ref, kl_ref, vc_ref, vl_ref, cos_ref, su_ref, sd_ref, qg_ref, kg_ref,
                     o_ref, k_scr, v_scr, *, n_ctx, tq, shift):
    g = pl.program_id(2)
    i = pl.program_id(3)
    heads = range(HEADS_PER_STEP)

    @pl.when((g == 0) & (i == 0))
    def _prep():
        kg = kg_ref[...]
        k_scr[0:n_ctx, :] = _rms(kc_ref[...].astype(F32), kg, 1e-6).astype(BF16)
        kl = _rms(kl_ref[...].astype(F32), kg, 1e-6)
        k_scr[n_ctx:, :] = _rope(kl, cos_ref[...], su_ref[...], sd_ref[...], shift).astype(BF16)
        v_scr[0:n_ctx, :] = vc_ref[...]
        v_scr[n_ctx:, :] = vl_ref[...]

    scale = HEAD_DIM ** -0.5 * LOG2E

    def attend(qs, n_keys):
        logits = [_dot_nt((qs[j] * scale).astype(BF16), k_scr[0:n_keys, :]) for j in heads]
        for j in heads:
            o_ref[:, _head_cols(j)] = _softmax_pv(logits[j], v_scr[0:n_keys, :]).astype(o_ref.dtype)

    @pl.when(i == 0)
    def _ctx_queries():
        attend([_rms(q_ref[:, _head_cols(j)].astype(F32), qg_ref[...], 1e-6) for j in heads], n_ctx)

    @pl.when(i > 0)
    def _lat_queries():
        r0 = pl.multiple_of((i - 1) * tq, tq)
        cos, su, sd = cos_ref[pl.ds(r0, tq), :], su_ref[pl.ds(r0, tq), :], sd_ref[pl.ds(r0, tq), :]
        attend([_rope(_rms(q_ref[:, _head_cols(j)].astype(F32), qg_ref[...], 1e-6), cos, su, sd, shift)
                for j in heads], k_scr.shape[0])


def _gqa_attn(proj, tables, q_norm_g, k_norm_g, col0, bsz, n_ctx, seq, n_heads, n_kv, tq=256):
    t = proj.shape[0]
    cos, su, sd, shift = tables
    hp = HEADS_PER_STEP
    wide = hp * LANES
    pairs = n_heads // n_kv // hp
    assert col0 % hp == 0
    qb = functools.partial(_q_block, bsz=bsz, n_ctx=n_ctx, seq=seq, tq=tq)
    lat0 = bsz * n_ctx // seq
    kcol = col0 + n_heads
    vcol = kcol + n_kv
    kern = functools.partial(_gqa_attn_kernel, n_ctx=n_ctx, tq=tq, shift=shift)
    tab = pl.BlockSpec((seq, LANES), lambda b, h, g, i: (0, 0))
    vec = pl.BlockSpec((1, LANES), lambda b, h, g, i: (0, 0))
    return pl.pallas_call(
        kern,
        grid=(bsz, n_kv, pairs, 1 + seq // tq),
        in_specs=[pl.BlockSpec((tq, wide), lambda b, h, g, i: (qb(b, i), col0 // hp + h * pairs + g)),
                  pl.BlockSpec((n_ctx, LANES), lambda b, h, g, i: (b, kcol + h)),
                  pl.BlockSpec((seq, LANES), lambda b, h, g, i: (lat0 + b, kcol + h)),
                  pl.BlockSpec((n_ctx, LANES), lambda b, h, g, i: (b, vcol + h)),
                  pl.BlockSpec((seq, LANES), lambda b, h, g, i: (lat0 + b, vcol + h)),
                  tab, tab, tab, vec, vec],
        out_specs=pl.BlockSpec((tq, wide), lambda b, h, g, i: (qb(b, i), h * pairs + g)),
        out_shape=jax.ShapeDtypeStruct((t, n_heads * LANES), BF16),
        scratch_shapes=[pltpu.VMEM((n_ctx + seq, LANES), BF16), pltpu.VMEM((n_ctx + seq, LANES), BF16)],
        compiler_params=_cparams(("arbitrary", "arbitrary", "arbitrary", "arbitrary")),
        name="gqa_attn",
    )(proj, proj, proj, proj, proj, cos, su, sd, q_norm_g.reshape(1, LANES), k_norm_g.reshape(1, LANES))


def _mla_attn_kernel(qn_ref, qp_ref, kvc_ref, kvl_ref, kpc_ref, kpl_ref, cos_ref, su_ref, sd_ref,
                     o_ref, k_scr, v_scr, *, n_ctx, tq, shift):
    i = pl.program_id(2)

    @pl.when(i == 0)
    def _prep():
        kpc = kpc_ref[...].astype(BF16)
        kpl = _rope(kpl_ref[...], cos_ref[...], su_ref[...], sd_ref[...], shift).astype(BF16)
        for j in range(HEADS_PER_STEP):
            c0 = 2 * j * LANES
            k_scr[j, 0:n_ctx, 0:LANES] = kvc_ref[:, c0:c0 + LANES]
            k_scr[j, n_ctx:, 0:LANES] = kvl_ref[:, c0:c0 + LANES]
            k_scr[j, 0:n_ctx, LANES:] = kpc
            k_scr[j, n_ctx:, LANES:] = kpl
            v_scr[j, 0:n_ctx, :] = kvc_ref[:, c0 + LANES:c0 + 2 * LANES]
            v_scr[j, n_ctx:, :] = kvl_ref[:, c0 + LANES:c0 + 2 * LANES]

    scale = (MLA_NOPE_DIM + MLA_ROPE_DIM) ** -0.5 * LOG2E
    r0 = pl.multiple_of(i * tq, tq)
    cos, su, sd = cos_ref[pl.ds(r0, tq), :], su_ref[pl.ds(r0, tq), :], sd_ref[pl.ds(r0, tq), :]
    logits = []
    for j in range(HEADS_PER_STEP):
        qp = _rope(qp_ref[:, _head_cols(j)].astype(F32), cos, su, sd, shift)
        q = jnp.concatenate([(qn_ref[:, _head_cols(j)].astype(F32) * scale).astype(BF16), (qp * scale).astype(BF16)],
                            axis=-1)
        logits.append(_dot_nt(q, k_scr[j]))
    for j in range(HEADS_PER_STEP):
        o_ref[:, _head_cols(j)] = _softmax_pv(logits[j], v_scr[j]).astype(o_ref.dtype)


def _mla_attn(q, kv, down, tables, bsz, n_ctx, seq, n_heads, kpe_col, tq=256):
    cos, su, sd, shift = tables
    lat0 = bsz * n_ctx // seq
    nq = seq // tq
    hp = HEADS_PER_STEP
    kern = functools.partial(_mla_attn_kernel, n_ctx=n_ctx, tq=tq, shift=shift)
    tab = pl.BlockSpec((seq, LANES), lambda b, h, i: (0, 0))
    return pl.pallas_call(
        kern,
        grid=(bsz, n_heads // hp, nq),
        in_specs=[pl.BlockSpec((tq, hp * LANES), lambda b, h, i: (b * nq + i, h)),
                  pl.BlockSpec((tq, hp * LANES), lambda b, h, i: (b * nq + i, n_heads // hp + h)),
                  pl.BlockSpec((n_ctx, 2 * hp * LANES), lambda b, h, i: (b, h)),
                  pl.BlockSpec((seq, 2 * hp * LANES), lambda b, h, i: (lat0 + b, h)),
                  pl.BlockSpec((n_ctx, LANES), lambda b, h, i: (b, kpe_col)),
                  pl.BlockSpec((seq, LANES), lambda b, h, i: (lat0 + b, kpe_col)),
                  tab, tab, tab],
        out_specs=pl.BlockSpec((tq, hp * LANES), lambda b, h, i: (b * nq + i, h)),
        out_shape=jax.ShapeDtypeStruct((bsz * seq, n_heads * LANES), BF16),
        scratch_shapes=[pltpu.VMEM((hp, n_ctx + seq, 2 * LANES), BF16), pltpu.VMEM((hp, n_ctx + seq, LANES), BF16)],
        compiler_params=_cparams(("arbitrary", "arbitrary", "arbitrary")),
        name="mla_attn",
    )(q, q, kv, kv, down, down, cos, su, sd)


def _layer_norm(z, g, b):
    zc = z - jnp.mean(z, axis=-1, keepdims=True)
    var = jnp.mean(zc * zc, axis=-1, keepdims=True)
    return zc * lax.rsqrt(var + 1e-5) * g + b


def _resln_kernel(*refs, n_act, alpha):
    x_ref, gate_ref = refs[0], refs[1]
    acts = refs[2:2 + n_act]
    ws = refs[2 + n_act:2 + 2 * n_act]
    lng_ref, lnb_ref, o_ref = refs[2 + 2 * n_act:]
    y = _dot(acts[0][...], ws[0][...])
    for a_ref, w_ref in zip(acts[1:], ws[1:]):
        y = y + _dot(a_ref[...], w_ref[...])
    z = alpha * x_ref[...] + gate_ref[0] * y
    o_ref[...] = _layer_norm(z, lng_ref[...], lnb_ref[...])


def _resln(x, mod, gate_idx, acts, ws, ln_g, ln_b, alpha, geom, row_off_blocks, n_rows, act_off_blocks, tm=512):
    d = x.shape[1]
    row = functools.partial(_mod_row, tm=tm, **geom)
    n_act = len(acts)
    in_specs = [pl.BlockSpec((tm, d), lambda i: (i + row_off_blocks, 0)),
                pl.BlockSpec((1, 1, d), lambda i: (row(i + row_off_blocks), 0, gate_idx))]
    in_specs += [pl.BlockSpec((tm, a.shape[1]), lambda i: (i + act_off_blocks, 0)) for a in acts]
    in_specs += [pl.BlockSpec(w.shape, lambda i: (0, 0)) for w in ws]
    in_specs += [pl.BlockSpec((1, d), lambda i: (0, 0))] * 2
    return pl.pallas_call(
        functools.partial(_resln_kernel, n_act=n_act, alpha=alpha),
        grid=(n_rows // tm,),
        in_specs=in_specs,
        out_specs=pl.BlockSpec((tm, d), lambda i: (i, 0)),
        out_shape=jax.ShapeDtypeStruct((n_rows, d), F32),
        compiler_params=_cparams(("arbitrary",)),
        name="proj_resln",
    )(x, mod, *acts, *ws, ln_g.reshape(1, d), ln_b.reshape(1, d))


def _pack_halves(x):
    n = x.shape[1] // 2
    bits = lax.bitcast_convert_type(x.astype(BF16).astype(F32), jnp.uint32)
    return (bits[:, :n] >> 16) | (bits[:, n:] & jnp.uint32(0xFFFF0000))


def _unpack_halves(w):
    lo = lax.bitcast_convert_type(w << 16, F32)
    hi = lax.bitcast_convert_type(w & jnp.uint32(0xFFFF0000), F32)
    return lo, hi


ROW_TILE = 8


def _store_token_tiles(ref, packed):
    m = packed.shape[0]
    for s in range(ROW_TILE):
        ref[pl.ds(s, m, stride=ROW_TILE), :] = packed[:, s * LANES:(s + 1) * LANES]


def _load_token_tiles(ref, m):
    return jnp.concatenate([ref[pl.ds(s, m, stride=ROW_TILE), :] for s in range(ROW_TILE)], axis=-1)


def _token_tile(ref, tok):
    return ref.at[pl.ds(pl.multiple_of(tok * ROW_TILE, ROW_TILE), ROW_TILE)]


def _pick_first_max(cur, idx, axes, sentinel):
    m = cur
    for ax in axes:
        m = jnp.max(m, axis=ax, keepdims=True)
    first = jnp.where(cur == m, idx, sentinel)
    for ax in axes:
        first = jnp.min(first, axis=ax, keepdims=True)
    return m, first


def _route_kernel(x_ref, sc_ref, sh_ref, rwt_ref, bias_ref, tri_ref, h_ref, e_ref, g_ref, r_ref, cnt_ref, run_ref):
    tm = x_ref.shape[0]
    n_members = N_EXPERTS // N_GROUPS

    @pl.when(pl.program_id(0) == 0)
    def _():
        run_ref[...] = jnp.zeros_like(run_ref)

    h = x_ref[...] * (1.0 + sc_ref[0]) + sh_ref[0]
    _store_token_tiles(h_ref, _pack_halves(h))
    logits = lax.dot_general(rwt_ref[...], h, (((1,), (1,)), ((), ())), precision=lax.Precision.HIGHEST,
                             preferred_element_type=F32)
    scores = _sigmoid(logits).reshape(N_GROUPS, n_members, tm)
    biased = scores + bias_ref[...]
    neg = -jnp.inf
    member = lax.broadcasted_iota(jnp.int32, biased.shape, 1).astype(F32)
    group = lax.broadcasted_iota(jnp.int32, biased.shape, 0).astype(F32)
    expert = group * n_members + member

    m1, first = _pick_first_max(biased, member, (1,), float(n_members))
    m2 = jnp.max(jnp.where(member == first, neg, biased), axis=1, keepdims=True)
    gscore = m1 + m2
    gidx = lax.broadcasted_iota(jnp.int32, gscore.shape, 0).astype(F32)
    group_ok = jnp.zeros(gscore.shape, jnp.bool_)
    for _ in range(TOPK_GROUPS):
        _, first = _pick_first_max(gscore, gidx, (0,), float(N_GROUPS))
        pick = gidx == first
        group_ok = group_ok | pick
        gscore = jnp.where(pick, neg, gscore)

    cur = jnp.where(group_ok, biased, neg)
    chosen = jnp.zeros(biased.shape, jnp.bool_)
    top_e, gates = [], []
    for _ in range(TOP_K):
        _, first = _pick_first_max(cur, expert, (0, 1), float(N_EXPERTS))
        pick = expert == first
        chosen = chosen | pick
        cur = jnp.where(pick, neg, cur)
        top_e.append(first)
        gates.append(jnp.sum(jnp.sum(jnp.where(pick, scores, 0.0), axis=0, keepdims=True), axis=1, keepdims=True))
    gsum = gates[0]
    for gk in gates[1:]:
        gsum = gsum + gk
    norm = ROUTE_SCALE / gsum

    chosen2d = jnp.where(chosen, 1.0, 0.0).reshape(N_EXPERTS, tm)
    before = _dot(chosen2d.astype(BF16), tri_ref[...])
    rank = (run_ref[...] + before).reshape(N_GROUPS, n_members, tm)
    run_ref[...] = run_ref[...] + jnp.sum(chosen2d, axis=1, keepdims=True)
    cnt_ref[...] = run_ref[...].astype(jnp.int32)
    for k in range(TOP_K):
        pick = expert == top_e[k]
        rk = jnp.sum(jnp.sum(jnp.where(pick, rank, 0.0), axis=0, keepdims=True), axis=1, keepdims=True)
        e_ref[k:k + 1, :] = top_e[k].reshape(1, tm).astype(jnp.int32)
        g_ref[k:k + 1, :] = (gates[k] * norm).reshape(1, tm)
        r_ref[k:k + 1, :] = rk.reshape(1, tm).astype(jnp.int32)


def _route(x, mod, sc_idx, sh_idx, router_w, router_bias, geom, tm=512):
    t, d = x.shape
    row = functools.partial(_mod_row, tm=tm, **geom)
    tri = (np.arange(tm)[:, None] < np.arange(tm)[None, :]).astype(np.float32)
    kt = pl.BlockSpec((TOP_K, tm), lambda i: (0, i))
    return pl.pallas_call(
        _route_kernel,
        grid=(t // tm,),
        in_specs=[pl.BlockSpec((tm, d), lambda i: (i, 0)),
                  pl.BlockSpec((1, 1, d), lambda i: (row(i), 0, sc_idx)),
                  pl.BlockSpec((1, 1, d), lambda i: (row(i), 0, sh_idx)),
                  pl.BlockSpec((N_EXPERTS, d), lambda i: (0, 0)),
                  pl.BlockSpec((N_GROUPS, N_EXPERTS // N_GROUPS, 1), lambda i: (0, 0, 0)),
                  pl.BlockSpec((tm, tm), lambda i: (0, 0))],
        out_specs=[pl.BlockSpec((tm * ROW_TILE, LANES), lambda i: (i, 0)), kt, kt, kt,
                   pl.BlockSpec((N_EXPERTS, 1), lambda i: (0, 0))],
        out_shape=[jax.ShapeDtypeStruct((t * ROW_TILE, LANES), jnp.uint32),
                   jax.ShapeDtypeStruct((TOP_K, t), jnp.int32),
                   jax.ShapeDtypeStruct((TOP_K, t), F32),
                   jax.ShapeDtypeStruct((TOP_K, t), jnp.int32),
                   jax.ShapeDtypeStruct((N_EXPERTS, 1), jnp.int32)],
        scratch_shapes=[pltpu.VMEM((N_EXPERTS, 1), F32)],
        compiler_params=_cparams(("arbitrary",)),
        name="moe_route",
    )(x, mod, mod, router_w.T, router_bias.astype(F32).reshape(N_GROUPS, N_EXPERTS // N_GROUPS, 1),
      jnp.asarray(tri, BF16))


def _dispatch_plan(top_e, rank, counts, n_blocks):
    counts = counts.reshape(N_EXPERTS)
    padded = (counts + MOE_BLOCK - 1) // MOE_BLOCK * MOE_BLOCK
    pad_ends = jnp.cumsum(padded)
    pad_starts = pad_ends - padded
    onehot = top_e[:, :, None] == jnp.arange(N_EXPERTS, dtype=jnp.int32)
    dest = jnp.sum(jnp.where(onehot, pad_starts, 0), axis=-1) + rank
    block_start = jnp.arange(n_blocks, dtype=jnp.int32) * MOE_BLOCK
    block_e = jnp.minimum(jnp.sum(pad_ends[None, :] <= block_start[:, None], axis=1), N_EXPERTS - 1)
    n_used = (pad_ends[-1] // MOE_BLOCK).astype(jnp.int32).reshape(1)
    return dest, pad_ends.astype(jnp.int32), block_e.astype(jnp.int32), n_used


def _dest_tiles(dest, tm):
    k, t = dest.shape
    return dest.reshape(k, t // tm, tm).transpose(1, 0, 2).reshape(t // tm, 1, k * tm)


def _dispatch_kernel(pe_ref, nu_ref, dest_ref, h_ref, xs_ref, dest_smem, zero_ref, sems, *, tm, n_blocks):
    i = pl.program_id(0)

    def zero_fill(slot0):
        rows = MOE_BLOCK * ROW_TILE
        row0 = slot0 * ROW_TILE if isinstance(slot0, int) else pl.multiple_of(slot0 * ROW_TILE, rows)
        return pltpu.make_async_copy(zero_ref, xs_ref.at[pl.ds(row0, rows)], sems.at[1])

    @pl.when(i == 0)
    def _():
        zero_ref[...] = jnp.zeros_like(zero_ref)
        for e in range(N_EXPERTS):
            start = pe_ref[e - 1] if e else 0

            @pl.when(pe_ref[e] > start)
            def _():
                zero_fill(pe_ref[e] - MOE_BLOCK).start()

        def start_unused(b, c):
            zero_fill(b * MOE_BLOCK).start()
            return c

        lax.fori_loop(nu_ref[0], n_blocks, start_unused, 0)
        for e in range(N_EXPERTS):
            start = pe_ref[e - 1] if e else 0

            @pl.when(pe_ref[e] > start)
            def _():
                zero_fill(0).wait()

        def wait_unused(b, c):
            zero_fill(0).wait()
            return c

        lax.fori_loop(nu_ref[0], n_blocks, wait_unused, 0)

    load = pltpu.make_async_copy(dest_ref.at[0, 0], dest_smem, sems.at[0])
    load.start()
    load.wait()

    def issue(t, c):
        for k in range(TOP_K):
            d = dest_smem[k * tm + t]
            pltpu.make_async_copy(_token_tile(h_ref, t), _token_tile(xs_ref, d), sems.at[2]).start(priority=k % 2)
        return c

    lax.fori_loop(0, tm, issue, 0)
    for k in range(TOP_K):
        pltpu.make_async_copy(h_ref, xs_ref.at[pl.ds(0, tm * ROW_TILE)], sems.at[2]).wait()


def _dispatch(h, dest, pad_ends, n_used, n_blocks, tm=512):
    t = h.shape[0] // ROW_TILE
    grid_spec = pltpu.PrefetchScalarGridSpec(
        num_scalar_prefetch=2,
        grid=(t // tm,),
        in_specs=[pl.BlockSpec((1, 1, TOP_K * tm), lambda i, pe, nu: (i, 0, 0)),
                  pl.BlockSpec((tm * ROW_TILE, LANES), lambda i, pe, nu: (i, 0))],
        out_specs=pl.BlockSpec(memory_space=pl.ANY),
        scratch_shapes=[pltpu.SMEM((TOP_K * tm,), jnp.int32),
                        pltpu.VMEM((MOE_BLOCK * ROW_TILE, LANES), jnp.uint32),
                        pltpu.SemaphoreType.DMA((3,))],
    )
    return pl.pallas_call(
        functools.partial(_dispatch_kernel, tm=tm, n_blocks=n_blocks),
        grid_spec=grid_spec,
        out_shape=jax.ShapeDtypeStruct((n_blocks * MOE_BLOCK * ROW_TILE, LANES), jnp.uint32),
        compiler_params=_cparams(("arbitrary",)),
        name="moe_dispatch",
    )(pad_ends, n_used, _dest_tiles(dest, tm), h)


def _experts_kernel(be_ref, nu_ref, x_ref, wg_ref, wu_ref, wo_ref, o_ref, wg_scr, wu_scr, wo_scr):
    blk = pl.program_id(0)
    prev = be_ref[jnp.maximum(blk - 1, 0)]
    half = ROW_TILE * LANES

    @pl.when((blk == 0) | (be_ref[blk] != prev))
    def _():
        wg_scr[...] = wg_ref[0, 0].astype(BF16)
        wu_scr[...] = wu_ref[0, 0].astype(BF16)
        wo_scr[...] = wo_ref[0, 0].astype(BF16)

    @pl.when(blk < nu_ref[0])
    def _():
        lo, hi = _unpack_halves(_load_token_tiles(x_ref, MOE_BLOCK))
        lo, hi = lo.astype(BF16), hi.astype(BF16)
        gate = _dot(lo, wg_scr[0:half, :]) + _dot(hi, wg_scr[half:, :])
        up = _dot(lo, wu_scr[0:half, :]) + _dot(hi, wu_scr[half:, :])
        a = (gate * _sigmoid(gate) * up).astype(BF16)
        _store_token_tiles(o_ref, _pack_halves(_dot(a, wo_scr[...])))

    @pl.when(blk >= nu_ref[0])
    def _():
        o_ref[...] = jnp.zeros_like(o_ref)


def _experts(xs, block_e, n_used, w_in, w_out, layer):
    d = w_in.shape[2]
    f = w_out.shape[2]
    assert d == 2 * ROW_TILE * LANES
    n_blocks = xs.shape[0] // (MOE_BLOCK * ROW_TILE)
    grid_spec = pltpu.PrefetchScalarGridSpec(
        num_scalar_prefetch=2,
        grid=(n_blocks,),
        in_specs=[pl.BlockSpec((MOE_BLOCK * ROW_TILE, LANES), lambda i, be, nu: (i, 0)),
                  pl.BlockSpec((1, 1, d, f), lambda i, be, nu: (layer, be[i], 0, 0)),
                  pl.BlockSpec((1, 1, d, f), lambda i, be, nu: (layer, be[i], 0, 1)),
                  pl.BlockSpec((1, 1, f, d), lambda i, be, nu: (layer, be[i], 0, 0))],
        out_specs=pl.BlockSpec((MOE_BLOCK * ROW_TILE, LANES), lambda i, be, nu: (i, 0)),
        scratch_shapes=[pltpu.VMEM((d, f), BF16), pltpu.VMEM((d, f), BF16), pltpu.VMEM((f, d), BF16)],
    )
    return pl.pallas_call(
        _experts_kernel,
        grid_spec=grid_spec,
        out_shape=jax.ShapeDtypeStruct(xs.shape, jnp.uint32),
        compiler_params=_cparams(("arbitrary",)),
        name="moe_experts",
    )(block_e, n_used, xs, w_in, w_in, w_out)


def _ffn_out_kernel(dest_ref, x_ref, gate_ref, h_ref, rg_ref, wg_ref, wu_ref, wo_ref, lng_ref, lnb_ref, ys_ref,
                    o_ref, dest_smem, rows_ref, sems, *, alpha, tm):
    half = ROW_TILE * LANES
    load = pltpu.make_async_copy(dest_ref.at[0, 0], dest_smem, sems.at[0])
    load.start()
    load.wait()

    def issue(t, c):
        for k in range(TOP_K):
            d = dest_smem[k * tm + t]
            pltpu.make_async_copy(_token_tile(ys_ref, d), _token_tile(rows_ref.at[k], t),
                                  sems.at[1]).start(priority=k % 2)
        return c

    lax.fori_loop(0, tm, issue, 0)

    lo, hi = _unpack_halves(_load_token_tiles(h_ref, tm))
    lo, hi = lo.astype(BF16), hi.astype(BF16)
    gate = _dot(lo, wg_ref[0:half, :]) + _dot(hi, wg_ref[half:, :])
    up = _dot(lo, wu_ref[0:half, :]) + _dot(hi, wu_ref[half:, :])
    a = (gate * _sigmoid(gate) * up).astype(BF16)
    y = _dot(a, wo_ref[...])
    y_lo, y_hi = y[:, :half], y[:, half:]

    rg = rg_ref[...]
    for k in range(TOP_K):
        pltpu.make_async_copy(ys_ref.at[pl.ds(0, tm * ROW_TILE)], rows_ref.at[k], sems.at[1]).wait()
    for k in range(TOP_K):
        r_lo, r_hi = _unpack_halves(_load_token_tiles(rows_ref.at[k], tm))
        y_lo = y_lo + r_lo * rg[:, k:k + 1]
        y_hi = y_hi + r_hi * rg[:, k:k + 1]
    z = alpha * x_ref[...] + gate_ref[0] * jnp.concatenate([y_lo, y_hi], axis=-1)
    o_ref[...] = _layer_norm(z, lng_ref[...], lnb_ref[...])


def _ffn_out(x, mod, gate_idx, h, ys, dest, route_gate, sh_in, sh_out, ln_g, ln_b, alpha, geom, tm=256):
    t, d = x.shape
    f = sh_out.shape[0]
    assert d == 2 * ROW_TILE * LANES
    row = functools.partial(_mod_row, tm=tm, **geom)
    return pl.pallas_call(
        functools.partial(_ffn_out_kernel, alpha=alpha, tm=tm),
        grid=(t // tm,),
        in_specs=[pl.BlockSpec((1, 1, TOP_K * tm), lambda i: (i, 0, 0)),
                  pl.BlockSpec((tm, d), lambda i: (i, 0)),
                  pl.BlockSpec((1, 1, d), lambda i: (row(i), 0, gate_idx)),
                  pl.BlockSpec((tm * ROW_TILE, LANES), lambda i: (i, 0)),
                  pl.BlockSpec((tm, TOP_K), lambda i: (i, 0)),
                  pl.BlockSpec((d, f), lambda i: (0, 0)),
                  pl.BlockSpec((d, f), lambda i: (0, 1)),
                  pl.BlockSpec((f, d), lambda i: (0, 0)),
                  pl.BlockSpec((1, d), lambda i: (0, 0)),
                  pl.BlockSpec((1, d), lambda i: (0, 0)),
                  pl.BlockSpec(memory_space=pl.ANY)],
        out_specs=pl.BlockSpec((tm, d), lambda i: (i, 0)),
        out_shape=jax.ShapeDtypeStruct((t, d), F32),
        scratch_shapes=[pltpu.SMEM((TOP_K * tm,), jnp.int32),
                        pltpu.VMEM((TOP_K, tm * ROW_TILE, LANES), jnp.uint32),
                        pltpu.SemaphoreType.DMA((2,))],
        compiler_params=_cparams(("arbitrary",)),
        name="ffn_out",
    )(_dest_tiles(dest, tm), x, mod, h, route_gate, sh_in, sh_in, sh_out, ln_g.reshape(1, d), ln_b.reshape(1, d), ys)


def _moe_sublayer(x, mod, geom, layer, router_w, router_bias, w_in, w_out, sh_in, sh_out, ln_g, ln_b, alpha):
    h, top_e, gate, rank, counts = _route(x, mod, 4, 3, router_w, router_bias, geom)
    n_blocks = x.shape[0] * TOP_K // MOE_BLOCK + N_EXPERTS
    dest, pad_ends, block_e, n_used = _dispatch_plan(top_e, rank, counts, n_blocks)
    xs = _dispatch(h, dest, pad_ends, n_used, n_blocks)
    ys = _experts(xs, block_e, n_used, w_in, w_out, layer)
    return _ffn_out(x, mod, 5, h, ys, dest, gate.T, sh_in.astype(BF16), sh_out.astype(BF16), ln_g, ln_b, alpha,
                    geom)


def kernel(x, c, ctx, c_ctx, ada_w, ada_b, ln_mix_g, ln_mix_b, ln_ffn_g, ln_ffn_b, ab_w_in, ab_w_out, diff_lambda, diff_subln_g, gqa_q_norm_g, gqa_k_norm_g, mla_w_down, mla_q_norm_g, mla_w_uq, mla_kv_norm_g, mla_w_ukv, mla_w_o, router_w, router_bias, expert_w_in, expert_w_out, shared_w_in, shared_w_out):
    bsz, seq, d = x.shape
    n_ctx = ctx.shape[1]
    depth = ada_w.shape[0]
    assert depth == 2 and seq % GRID_W == 0 and bsz + 1 <= MOD_ROWS
    alpha = (2 * depth) ** 0.25
    n_ctx_rows = bsz * n_ctx
    n_lat_rows = bsz * seq
    geom_all = dict(n_ctx_rows=n_ctx_rows, seq=seq, ctx_row=bsz)
    geom_lat = dict(n_ctx_rows=0, seq=seq, ctx_row=bsz)

    tables64 = _rope_tables(seq // GRID_W, 64)
    tables128 = _rope_tables(seq // GRID_W, HEAD_DIM)

    cvec = jnp.zeros((MOD_ROWS, d), F32).at[:bsz].set(c).at[bsz].set(c_ctx)
    mods = _ada_mod(cvec, ada_w, ada_b)
    mods = mods.reshape(depth, MOD_ROWS, 1, 6 * d)

    xt = jnp.concatenate([ctx.reshape(n_ctx_rows, d), x.reshape(n_lat_rows, d)], axis=0)

    mod = mods[0]
    n_diff = d // (2 * HEAD_DIM)
    n_gqa = d // (2 * HEAD_DIM)
    n_gqa_kv = n_gqa // 4
    proj = _modproj(xt, mod, 1, 0, ab_w_in[0].astype(BF16), BF16, geom_all)
    lam_init = 0.8 - 0.6 * math.exp(-0.3 * 0)
    oa = _diff_attn(proj, tables64, diff_lambda[0], diff_subln_g[0], lam_init, bsz, n_ctx, seq, n_diff)
    ob = _gqa_attn(proj, tables128, gqa_q_norm_g[0], gqa_k_norm_g[0], 3 * n_diff, bsz, n_ctx, seq, n_gqa, n_gqa_kv)
    w_out = ab_w_out[0].astype(BF16)
    wa, wb = w_out[:n_diff * HEAD_DIM], w_out[n_diff * HEAD_DIM:]
    t_all = n_ctx_rows + n_lat_rows
    xt = _resln(xt, mod, 2, [oa, ob], [wa, wb], ln_mix_g[0], ln_mix_b[0], alpha, geom_all, 0, t_all, 0)
    xt = _moe_sublayer(xt, mod, geom_all, 0, router_w[0], router_bias[0], expert_w_in, expert_w_out,
                       shared_w_in[0], shared_w_out[0], ln_ffn_g[0], ln_ffn_b[0], alpha)

    mod = mods[1]
    n_mla = d // HEAD_DIM
    qk = MLA_NOPE_DIM + MLA_ROPE_DIM
    w_down = jnp.pad(mla_w_down[0], ((0, 0), (0, LANES - MLA_ROPE_DIM))).astype(BF16)
    down = _modproj(xt, mod, 1, 0, w_down, F32, geom_all)
    w_uq = mla_w_uq[0].reshape(MLA_Q_RANK, n_mla, qk)
    w_uq_rope = jnp.pad(w_uq[:, :, MLA_NOPE_DIM:], ((0, 0), (0, 0), (0, LANES - MLA_ROPE_DIM)))
    w_uq = jnp.concatenate([w_uq[:, :, :MLA_NOPE_DIM].reshape(MLA_Q_RANK, -1),
                            w_uq_rope.reshape(MLA_Q_RANK, -1)], axis=1).astype(BF16)
    lat_blocks = n_ctx_rows // 1024
    q = _normproj(down, 0, mla_q_norm_g[0], w_uq, lat_blocks, n_lat_rows)
    kv = _normproj(down, 1, mla_kv_norm_g[0], mla_w_ukv[0].astype(BF16), 0, t_all)
    kpe_col = (MLA_Q_RANK + MLA_KV_RANK) // LANES
    o = _mla_attn(q, kv, down, tables64, bsz, n_ctx, seq, n_mla, kpe_col)
    xl = _resln(xt, mod, 2, [o], [mla_w_o[0].astype(BF16)], ln_mix_g[1], ln_mix_b[1], alpha, geom_all,
                n_ctx_rows // 512, n_lat_rows, 0)
    xl = _moe_sublayer(xl, mod, geom_lat, 1, router_w[1], router_bias[1], expert_w_in, expert_w_out,
                       shared_w_in[1], shared_w_out[1], ln_ffn_g[1], ln_ffn_b[1], alpha)
    return xl.reshape(bsz, seq, d)
```

```python
import functools
import math

import numpy as np
import jax
import jax.numpy as jnp
from jax import lax
from jax.experimental import pallas as pl
from jax.experimental.pallas import tpu as pltpu

F32 = jnp.float32
BF16 = jnp.bfloat16

GRID_W = 64
ROPE_THETA = 10000.0
HEAD_DIM = 128
DIFF_QK_DIM = 64
N_EXPERTS = 64
EXPERT_DIM = 512
TOP_K = 8
N_GROUPS = 8
TOPK_GROUPS = 4
ROUTE_SCALE = 2.5
MLA_Q_RANK = 512
MLA_KV_RANK = 512
MLA_NOPE_DIM = 128
MLA_ROPE_DIM = 64
MLA_V_DIM = 128
LANES = 128

MOD_ROWS = 16
VMEM_LIMIT = 56 * 1024 * 1024
MOE_BLOCK = 256


def _cparams(sem):
    return pltpu.CompilerParams(dimension_semantics=sem, vmem_limit_bytes=VMEM_LIMIT)


def _dot(a, b):
    return jnp.dot(a, b, preferred_element_type=F32)


def _dot_nt(a, b):
    return lax.dot_general(a, b, (((1,), (1,)), ((), ())), preferred_element_type=F32)


def _dot_hi(a, b):
    return lax.dot_general(a, b, (((1,), (0,)), ((), ())), precision=lax.Precision.HIGHEST,
                           preferred_element_type=F32)


def _sigmoid(x):
    return 1.0 / (1.0 + jnp.exp(-x))


def _rope(x, cos, sin_up, sin_dn, shift):
    return (x * cos + pltpu.roll(x, LANES - shift, 1) * sin_up + pltpu.roll(x, shift, 1) * sin_dn)


def _rope_tables(rows, rot_dim):
    t = jnp.arange(rows * GRID_W)
    row = (t // GRID_W).astype(F32)
    col = (t % GRID_W).astype(F32)
    axis_dim = rot_dim // 2
    quarter = rot_dim // 4
    inv_freq = 1.0 / (ROPE_THETA ** (jnp.arange(0, axis_dim, 2, dtype=F32) / axis_dim))
    lane = np.arange(LANES) % rot_dim
    is_col = lane >= axis_dim
    within = lane % axis_dim
    first = within < quarter
    freq = within % quarter
    ang = jnp.where(jnp.asarray(is_col)[None, :], col[:, None], row[:, None]) * inv_freq[freq][None, :]
    cos = jnp.cos(ang).astype(F32)
    sin = jnp.sin(ang).astype(F32)
    first = jnp.asarray(first)[None, :]
    sin_up = jnp.where(first, -sin, 0.0)
    sin_dn = jnp.where(first, 0.0, sin)
    return cos, sin_up, sin_dn, quarter


LOG2E = 1.4426950408889634
HEADS_PER_STEP = 4


def _exp2_rows(s):
    e = jnp.exp2(s - jnp.max(s, axis=-1, keepdims=True))
    return e, 1.0 / jnp.sum(e, axis=-1, keepdims=True)


def _rms(x, g, eps):
    return x * lax.rsqrt(jnp.mean(x * x, axis=-1, keepdims=True) + eps) * g


def _ada_kernel(c_ref, w_ref, b_ref, o_ref):
    c = c_ref[...]
    act = c * _sigmoid(c)
    o_ref[0] = _dot_hi(act, w_ref[0]) + b_ref[0]


def _ada_mod(cvec, ada_w, ada_b):
    depth, d, n = ada_w.shape
    tn = 1536
    return pl.pallas_call(
        _ada_kernel,
        grid=(depth, n // tn),
        in_specs=[pl.BlockSpec((MOD_ROWS, d), lambda l, j: (0, 0)),
                  pl.BlockSpec((1, d, tn), lambda l, j: (l, 0, j)),
                  pl.BlockSpec((1, 1, tn), lambda l, j: (l, 0, j))],
        out_specs=pl.BlockSpec((1, MOD_ROWS, tn), lambda l, j: (l, 0, j)),
        out_shape=jax.ShapeDtypeStruct((depth, MOD_ROWS, n), F32),
        compiler_params=_cparams(("arbitrary", "arbitrary")),
        name="ada_mod",
    )(cvec, ada_w, ada_b.reshape(depth, 1, n))


def _mod_row(i, tm, n_ctx_rows, seq, ctx_row):
    r0 = i * tm
    return jnp.where(r0 < n_ctx_rows, ctx_row, (r0 - n_ctx_rows) // seq)


def _modproj_kernel(x_ref, sc_ref, sh_ref, w_ref, o_ref, h_ref):
    @pl.when(pl.program_id(1) == 0)
    def _():
        h_ref[...] = (x_ref[...] * (1.0 + sc_ref[0]) + sh_ref[0]).astype(h_ref.dtype)

    o_ref[...] = _dot(h_ref[...], w_ref[...]).astype(o_ref.dtype)


def _modproj(x, mod, sc_idx, sh_idx, w, out_dtype, geom, tm=1024, tn=512):
    t, d = x.shape
    n = w.shape[1]
    tn = tn if n % tn == 0 else n
    row = functools.partial(_mod_row, tm=tm, **geom)
    return pl.pallas_call(
        _modproj_kernel,
        grid=(t // tm, n // tn),
        in_specs=[pl.BlockSpec((tm, d), lambda i, j: (i, 0)),
                  pl.BlockSpec((1, 1, d), lambda i, j: (row(i), 0, sc_idx)),
                  pl.BlockSpec((1, 1, d), lambda i, j: (row(i), 0, sh_idx)),
                  pl.BlockSpec((d, tn), lambda i, j: (0, j))],
        out_specs=pl.BlockSpec((tm, tn), lambda i, j: (i, j)),
        out_shape=jax.ShapeDtypeStruct((t, n), out_dtype),
        scratch_shapes=[pltpu.VMEM((tm, d), BF16)],
        compiler_params=_cparams(("arbitrary", "arbitrary")),
        name="modproj",
    )(x, mod, mod, w)


def _normproj_kernel(x_ref, g_ref, w_ref, o_ref, h_ref):
    @pl.when(pl.program_id(1) == 0)
    def _():
        h_ref[...] = _rms(x_ref[...], g_ref[...], 1e-6).astype(h_ref.dtype)

    o_ref[...] = _dot(h_ref[...], w_ref[...]).astype(o_ref.dtype)


def _normproj(x, col_block, g, w, row_off_blocks, n_rows, tm=1024, tn=1024):
    k = w.shape[0]
    n = w.shape[1]
    return pl.pallas_call(
        _normproj_kernel,
        grid=(n_rows // tm, n // tn),
        in_specs=[pl.BlockSpec((tm, k), lambda i, j: (i + row_off_blocks, col_block)),
                  pl.BlockSpec((1, k), lambda i, j: (0, 0)),
                  pl.BlockSpec((k, tn), lambda i, j: (0, j))],
        out_specs=pl.BlockSpec((tm, tn), lambda i, j: (i, j)),
        out_shape=jax.ShapeDtypeStruct((n_rows, n), BF16),
        scratch_shapes=[pltpu.VMEM((tm, k), BF16)],
        compiler_params=_cparams(("arbitrary", "arbitrary")),
        name="normproj",
    )(x, g.reshape(1, k), w)


def _diff_lambda(lam_ref, lam_init):
    lp = lam_ref[...]
    s01 = jnp.sum(lp[0:1] * lp[1:2], axis=-1, keepdims=True)
    s23 = jnp.sum(lp[2:3] * lp[3:4], axis=-1, keepdims=True)
    return jnp.exp(s01) - jnp.exp(s23) + lam_init


def _diff_logits(q, k):
    lane = lax.broadcasted_iota(jnp.int32, q.shape, 1)
    qs = q * (DIFF_QK_DIM ** -0.5 * LOG2E)
    q1 = jnp.where(lane < DIFF_QK_DIM, qs, 0.0).astype(BF16)
    q2 = jnp.where(lane >= DIFF_QK_DIM, qs, 0.0).astype(BF16)
    return _dot_nt(q1, k), _dot_nt(q2, k)


def _diff_combine(s1, s2, v, lam, g, lam_init):
    e1, r1 = _exp2_rows(s1)
    e2, r2 = _exp2_rows(s2)
    a = (e1 * r1 - e2 * (lam * r2)).astype(BF16)
    return _rms(_dot(a, v), g, 1e-5) * (1.0 - lam_init)


def _head_cols(j):
    return slice(j * LANES, (j + 1) * LANES)


def _diff_attn_kernel(q_ref, kc_ref, kl_ref, vc_ref, vl_ref, cos_ref, su_ref, sd_ref, lam_ref, g_ref,
                      o_ref, k_scr, v_scr, *, n_ctx, tq, shift, lam_init):
    i = pl.program_id(2)
    heads = range(HEADS_PER_STEP)

    @pl.when(i == 0)
    def _prep():
        for j in heads:
            k_scr[j, 0:n_ctx, :] = kc_ref[:, _head_cols(j)]
            kl = kl_ref[:, _head_cols(j)].astype(F32)
            k_scr[j, n_ctx:, :] = _rope(kl, cos_ref[...], su_ref[...], sd_ref[...], shift).astype(BF16)
            v_scr[j, 0:n_ctx, :] = vc_ref[:, _head_cols(j)]
            v_scr[j, n_ctx:, :] = vl_ref[:, _head_cols(j)]

    lam = _diff_lambda(lam_ref, lam_init)
    g = g_ref[...]

    def attend(qs, n_keys):
        logits = [_diff_logits(qs[j], k_scr[j, 0:n_keys, :]) for j in heads]
        for j in heads:
            o = _diff_combine(*logits[j], v_scr[j, 0:n_keys, :], lam, g, lam_init)
            o_ref[:, _head_cols(j)] = o.astype(o_ref.dtype)

    @pl.when(i == 0)
    def _ctx_queries():
        attend([q_ref[:, _head_cols(j)].astype(F32) for j in heads], n_ctx)

    @pl.when(i > 0)
    def _lat_queries():
        r0 = pl.multiple_of((i - 1) * tq, tq)
        cos, su, sd = cos_ref[pl.ds(r0, tq), :], su_ref[pl.ds(r0, tq), :], sd_ref[pl.ds(r0, tq), :]
        attend([_rope(q_ref[:, _head_cols(j)].astype(F32), cos, su, sd, shift) for j in heads],
               k_scr.shape[1])


def _q_block(b, i, bsz, n_ctx, seq, tq):
    return jnp.where(i == 0, b * (n_ctx // tq), (bsz * n_ctx + b * seq) // tq + i - 1)


def _diff_attn(proj, tables, lam_params, subln_g, lam_init, bsz, n_ctx, seq, n_heads, tq=256):
    t = proj.shape[0]
    cos, su, sd, shift = tables
    hp = HEADS_PER_STEP
    wide = hp * LANES
    n_pairs = n_heads // hp
    qb = functools.partial(_q_block, bsz=bsz, n_ctx=n_ctx, seq=seq, tq=tq)
    lat0 = bsz * n_ctx // seq
    kern = functools.partial(_diff_attn_kernel, n_ctx=n_ctx, tq=tq, shift=shift, lam_init=lam_init)
    tab = pl.BlockSpec((seq, LANES), lambda b, h, i: (0, 0))
    return pl.pallas_call(
        kern,
        grid=(bsz, n_pairs, 1 + seq // tq),
        in_specs=[pl.BlockSpec((tq, wide), lambda b, h, i: (qb(b, i), h)),
                  pl.BlockSpec((n_ctx, wide), lambda b, h, i: (b, n_pairs + h)),
                  pl.BlockSpec((seq, wide), lambda b, h, i: (lat0 + b, n_pairs + h)),
                  pl.BlockSpec((n_ctx, wide), lambda b, h, i: (b, 2 * n_pairs + h)),
                  pl.BlockSpec((seq, wide), lambda b, h, i: (lat0 + b, 2 * n_pairs + h)),
                  tab, tab, tab,
                  pl.BlockSpec((4, DIFF_QK_DIM), lambda b, h, i: (0, 0)),
                  pl.BlockSpec((1, LANES), lambda b, h, i: (0, 0))],
        out_specs=pl.BlockSpec((tq, wide), lambda b, h, i: (qb(b, i), h)),
        out_shape=jax.ShapeDtypeStruct((t, n_heads * LANES), BF16),
        scratch_shapes=[pltpu.VMEM((hp, n_ctx + seq, LANES), BF16), pltpu.VMEM((hp, n_ctx + seq, LANES), BF16)],
        compiler_params=_cparams(("arbitrary", "arbitrary", "arbitrary")),
        name="diff_attn",
    )(proj, proj, proj, proj, proj, cos, su, sd, lam_params, subln_g.reshape(1, LANES))


def _softmax_pv(s, v):
    e, r = _exp2_rows(s)
    return _dot(e.astype(BF16), v) * r


def _gqa_attn_kernel(q_ref, kc_ref, kl_ref, vc_ref, vl_ref, cos_ref, su_ref, sd_ref, qg_ref, kg_ref,
                     o_ref, k_scr, v_scr, *, n_ctx, tq, shift):
    g = pl.program_id(2)
    i = pl.program_id(3)
    heads = range(HEADS_PER_STEP)

    @pl.when((g == 0) & (i == 0))
    def _prep():
        kg = kg_ref[...]
        k_scr[0:n_ctx, :] = _rms(kc_ref[...].astype(F32), kg, 1e-6).astype(BF16)
        kl = _rms(kl_ref[...].astype(F32), kg, 1e-6)
        k_scr[n_ctx:, :] = _rope(kl, cos_ref[...], su_ref[...], sd_ref[...], shift).astype(BF16)
        v_scr[0:n_ctx, :] = vc_ref[...]
        v_scr[n_ctx:, :] = vl_ref[...]

    scale = HEAD_DIM ** -0.5 * LOG2E

    def attend(qs, n_keys):
        logits = [_dot_nt((qs[j] * scale).astype(BF16), k_scr[0:n_keys, :]) for j in heads]
        for j in heads:
            o_ref[:, _head_cols(j)] = _softmax_pv(logits[j], v_scr[0:n_keys, :]).astype(o_ref.dtype)

    @pl.when(i == 0)
    def _ctx_queries():
        attend([_rms(q_ref[:, _head_cols(j)].astype(F32), qg_ref[...], 1e-6) for j in heads], n_ctx)

    @pl.when(i > 0)
    def _lat_queries():
        r0 = pl.multiple_of((i - 1) * tq, tq)
        cos, su, sd = cos_ref[pl.ds(r0, tq), :], su_ref[pl.ds(r0, tq), :], sd_ref[pl.ds(r0, tq), :]
        attend([_rope(_rms(q_ref[:, _head_cols(j)].astype(F32), qg_ref[...], 1e-6), cos, su, sd, shift)
                for j in heads], k_scr.shape[0])


def _gqa_attn(proj, tables, q_norm_g, k_norm_g, col0, bsz, n_ctx, seq, n_heads, n_kv, tq=256):
    t = proj.shape[0]
    cos, su, sd, shift = tables
    hp = HEADS_PER_STEP
    wide = hp * LANES
    pairs = n_heads // n_kv // hp
    assert col0 % hp == 0
    qb = functools.partial(_q_block, bsz=bsz, n_ctx=n_ctx, seq=seq, tq=tq)
    lat0 = bsz * n_ctx // seq
    kcol = col0 + n_heads
    vcol = kcol + n_kv
    kern = functools.partial(_gqa_attn_kernel, n_ctx=n_ctx, tq=tq, shift=shift)
    tab = pl.BlockSpec((seq, LANES), lambda b, h, g, i: (0, 0))
    vec = pl.BlockSpec((1, LANES), lambda b, h, g, i: (0, 0))
    return pl.pallas_call(
        kern,
        grid=(bsz, n_kv, pairs, 1 + seq // tq),
        in_specs=[pl.BlockSpec((tq, wide), lambda b, h, g, i: (qb(b, i), col0 // hp + h * pairs + g)),
                  pl.BlockSpec((n_ctx, LANES), lambda b, h, g, i: (b, kcol + h)),
                  pl.BlockSpec((seq, LANES), lambda b, h, g, i: (lat0 + b, kcol + h)),
                  pl.BlockSpec((n_ctx, LANES), lambda b, h, g, i: (b, vcol + h)),
                  pl.BlockSpec((seq, LANES), lambda b, h, g, i: (lat0 + b, vcol + h)),
                  tab, tab, tab, vec, vec],
        out_specs=pl.BlockSpec((tq, wide), lambda b, h, g, i: (qb(b, i), h * pairs + g)),
        out_shape=jax.ShapeDtypeStruct((t, n_heads * LANES), BF16),
        scratch_shapes=[pltpu.VMEM((n_ctx + seq, LANES), BF16), pltpu.VMEM((n_ctx + seq, LANES), BF16)],
        compiler_params=_cparams(("arbitrary", "arbitrary", "arbitrary", "arbitrary")),
        name="gqa_attn",
    )(proj, proj, proj, proj, proj, cos, su, sd, q_norm_g.reshape(1, LANES), k_norm_g.reshape(1, LANES))


def _mla_attn_kernel(qn_ref, qp_ref, kvc_ref, kvl_ref, kpc_ref, kpl_ref, cos_ref, su_ref, sd_ref,
                     o_ref, k_scr, v_scr, *, n_ctx, tq, shift):
    i = pl.program_id(2)

    @pl.when(i == 0)
    def _prep():
        kpc = kpc_ref[...].astype(BF16)
        kpl = _rope(kpl_ref[...], cos_ref[...], su_ref[...], sd_ref[...], shift).astype(BF16)
        for j in range(HEADS_PER_STEP):
            c0 = 2 * j * LANES
            k_scr[j, 0:n_ctx, 0:LANES] = kvc_ref[:, c0:c0 + LANES]
            k_scr[j, n_ctx:, 0:LANES] = kvl_ref[:, c0:c0 + LANES]
            k_scr[j, 0:n_ctx, LANES:] = kpc
            k_scr[j, n_ctx:, LANES:] = kpl
            v_scr[j, 0:n_ctx, :] = kvc_ref[:, c0 + LANES:c0 + 2 * LANES]
            v_scr[j, n_ctx:, :] = kvl_ref[:, c0 + LANES:c0 + 2 * LANES]

    scale = (MLA_NOPE_DIM + MLA_ROPE_DIM) ** -0.5 * LOG2E
    r0 = pl.multiple_of(i * tq, tq)
    cos, su, sd = cos_ref[pl.ds(r0, tq), :], su_ref[pl.ds(r0, tq), :], sd_ref[pl.ds(r0, tq), :]
    logits = []
    for j in range(HEADS_PER_STEP):
        qp = _rope(qp_ref[:, _head_cols(j)].astype(F32), cos, su, sd, shift)
        q = jnp.concatenate([(qn_ref[:, _head_cols(j)].astype(F32) * scale).astype(BF16), (qp * scale).astype(BF16)],
                            axis=-1)
        logits.append(_dot_nt(q, k_scr[j]))
    for j in range(HEADS_PER_STEP):
        o_ref[:, _head_cols(j)] = _softmax_pv(logits[j], v_scr[j]).astype(o_ref.dtype)


def _mla_attn(q, kv, down, tables, bsz, n_ctx, seq, n_heads, kpe_col, tq=256):
    cos, su, sd, shift = tables
    lat0 = bsz * n_ctx // seq
    nq = seq // tq
    hp = HEADS_PER_STEP
    kern = functools.partial(_mla_attn_kernel, n_ctx=n_ctx, tq=tq, shift=shift)
    tab = pl.BlockSpec((seq, LANES), lambda b, h, i: (0, 0))
    return pl.pallas_call(
        kern,
        grid=(bsz, n_heads // hp, nq),
        in_specs=[pl.BlockSpec((tq, hp * LANES), lambda b, h, i: (b * nq + i, h)),
                  pl.BlockSpec((tq, hp * LANES), lambda b, h, i: (b * nq + i, n_heads // hp + h)),
                  pl.BlockSpec((n_ctx, 2 * hp * LANES), lambda b, h, i: (b, h)),
                  pl.BlockSpec((seq, 2 * hp * LANES), lambda b, h, i: (lat0 + b, h)),
                  pl.BlockSpec((n_ctx, LANES), lambda b, h, i: (b, kpe_col)),
                  pl.BlockSpec((seq, LANES), lambda b, h, i: (lat0 + b, kpe_col)),
                  tab, tab, tab],
        out_specs=pl.BlockSpec((tq, hp * LANES), lambda b, h, i: (b * nq + i, h)),
        out_shape=jax.ShapeDtypeStruct((bsz * seq, n_heads * LANES), BF16),
        scratch_shapes=[pltpu.VMEM((hp, n_ctx + seq, 2 * LANES), BF16), pltpu.VMEM((hp, n_ctx + seq, LANES), BF16)],
        compiler_params=_cparams(("arbitrary", "arbitrary", "arbitrary")),
        name="mla_attn",
    )(q, q, kv, kv, down, down, cos, su, sd)


def _layer_norm(z, g, b):
    zc = z - jnp.mean(z, axis=-1, keepdims=True)
    var = jnp.mean(zc * zc, axis=-1, keepdims=True)
    return zc * lax.rsqrt(var + 1e-5) * g + b


def _resln_kernel(*refs, n_act, alpha):
    x_ref, gate_ref = refs[0], refs[1]
    acts = refs[2:2 + n_act]
    ws = refs[2 + n_act:2 + 2 * n_act]
    lng_ref, lnb_ref, o_ref = refs[2 + 2 * n_act:]
    y = _dot(acts[0][...], ws[0][...])
    for a_ref, w_ref in zip(acts[1:], ws[1:]):
        y = y + _dot(a_ref[...], w_ref[...])
    z = alpha * x_ref[...] + gate_ref[0] * y
    o_ref[...] = _layer_norm(z, lng_ref[...], lnb_ref[...])


def _resln(x, mod, gate_idx, acts, ws, ln_g, ln_b, alpha, geom, row_off_blocks, n_rows, act_off_blocks, tm=512):
    d = x.shape[1]
    row = functools.partial(_mod_row, tm=tm, **geom)
    n_act = len(acts)
    in_specs = [pl.BlockSpec((tm, d), lambda i: (i + row_off_blocks, 0)),
                pl.BlockSpec((1, 1, d), lambda i: (row(i + row_off_blocks), 0, gate_idx))]
    in_specs += [pl.BlockSpec((tm, a.shape[1]), lambda i: (i + act_off_blocks, 0)) for a in acts]
    in_specs += [pl.BlockSpec(w.shape, lambda i: (0, 0)) for w in ws]
    in_specs += [pl.BlockSpec((1, d), lambda i: (0, 0))] * 2
    return pl.pallas_call(
        functools.partial(_resln_kernel, n_act=n_act, alpha=alpha),
        grid=(n_rows // tm,),
        in_specs=in_specs,
        out_specs=pl.BlockSpec((tm, d), lambda i: (i, 0)),
        out_shape=jax.ShapeDtypeStruct((n_rows, d), F32),
        compiler_params=_cparams(("arbitrary",)),
        name="proj_resln",
    )(x, mod, *acts, *ws, ln_g.reshape(1, d), ln_b.reshape(1, d))


def _pack_halves(x):
    n = x.shape[1] // 2
    bits = lax.bitcast_convert_type(x.astype(BF16).astype(F32), jnp.uint32)
    return (bits[:, :n] >> 16) | (bits[:, n:] & jnp.uint32(0xFFFF0000))


def _unpack_halves(w):
    lo = lax.bitcast_convert_type(w << 16, F32)
    hi = lax.bitcast_convert_type(w & jnp.uint32(0xFFFF0000), F32)
    return lo, hi


ROW_TILE = 8


def _store_token_tiles(ref, packed):
    m = packed.shape[0]
    for s in range(ROW_TILE):
        ref[pl.ds(s, m, stride=ROW_TILE), :] = packed[:, s * LANES:(s + 1) * LANES]


def _load_token_tiles(ref, m):
    return jnp.concatenate([ref[pl.ds(s, m, stride=ROW_TILE), :] for s in range(ROW_TILE)], axis=-1)


def _token_tile(ref, tok):
    return ref.at[pl.ds(pl.multiple_of(tok * ROW_TILE, ROW_TILE), ROW_TILE)]


def _pick_first_max(cur, idx, axes, sentinel):
    m = cur
    for ax in axes:
        m = jnp.max(m, axis=ax, keepdims=True)
    first = jnp.where(cur == m, idx, sentinel)
    for ax in axes:
        first = jnp.min(first, axis=ax, keepdims=True)
    return m, first


def _route_kernel(x_ref, sc_ref, sh_ref, rwt_ref, bias_ref, tri_ref, h_ref, e_ref, g_ref, r_ref, cnt_ref, run_ref):
    tm = x_ref.shape[0]
    n_members = N_EXPERTS // N_GROUPS

    @pl.when(pl.program_id(0) == 0)
    def _():
        run_ref[...] = jnp.zeros_like(run_ref)

    h = x_ref[...] * (1.0 + sc_ref[0]) + sh_ref[0]
    _store_token_tiles(h_ref, _pack_halves(h))
    logits = lax.dot_general(rwt_ref[...], h, (((1,), (1,)), ((), ())), precision=lax.Precision.HIGHEST,
                             preferred_element_type=F32)
    scores = _sigmoid(logits).reshape(N_GROUPS, n_members, tm)
    biased = scores + bias_ref[...]
    neg = -jnp.inf
    member = lax.broadcasted_iota(jnp.int32, biased.shape, 1).astype(F32)
    group = lax.broadcasted_iota(jnp.int32, biased.shape, 0).astype(F32)
    expert = group * n_members + member

    m1, first = _pick_first_max(biased, member, (1,), float(n_members))
    m2 = jnp.max(jnp.where(member == first, neg, biased), axis=1, keepdims=True)
    gscore = m1 + m2
    gidx = lax.broadcasted_iota(jnp.int32, gscore.shape, 0).astype(F32)
    group_ok = jnp.zeros(gscore.shape, jnp.bool_)
    for _ in range(TOPK_GROUPS):
        _, first = _pick_first_max(gscore, gidx, (0,), float(N_GROUPS))
        pick = gidx == first
        group_ok = group_ok | pick
        gscore = jnp.where(pick, neg, gscore)

    cur = jnp.where(group_ok, biased, neg)
    chosen = jnp.zeros(biased.shape, jnp.bool_)
    top_e, gates = [], []
    for _ in range(TOP_K):
        _, first = _pick_first_max(cur, expert, (0, 1), float(N_EXPERTS))
        pick = expert == first
        chosen = chosen | pick
        cur = jnp.where(pick, neg, cur)
        top_e.append(first)
        gates.append(jnp.sum(jnp.sum(jnp.where(pick, scores, 0.0), axis=0, keepdims=True), axis=1, keepdims=True))
    gsum = gates[0]
    for gk in gates[1:]:
        gsum = gsum + gk
    norm = ROUTE_SCALE / gsum

    chosen2d = jnp.where(chosen, 1.0, 0.0).reshape(N_EXPERTS, tm)
    before = _dot(chosen2d.astype(BF16), tri_ref[...])
    rank = (run_ref[...] + before).reshape(N_GROUPS, n_members, tm)
    run_ref[...] = run_ref[...] + jnp.sum(chosen2d, axis=1, keepdims=True)
    cnt_ref[...] = run_ref[...].astype(jnp.int32)
    for k in range(TOP_K):
        pick = expert == top_e[k]
        rk = jnp.sum(jnp.sum(jnp.where(pick, rank, 0.0), axis=0, keepdims=True), axis=1, keepdims=True)
        e_ref[k:k + 1, :] = top_e[k].reshape(1, tm).astype(jnp.int32)
        g_ref[k:k + 1, :] = (gates[k] * norm).reshape(1, tm)
        r_ref[k:k + 1, :] = rk.reshape(1, tm).astype(jnp.int32)


def _route(x, mod, sc_idx, sh_idx, router_w, router_bias, geom, tm=512):
    t, d = x.shape
    row = functools.partial(_mod_row, tm=tm, **geom)
    tri = (np.arange(tm)[:, None] < np.arange(tm)[None, :]).astype(np.float32)
    kt = pl.BlockSpec((TOP_K, tm), lambda i: (0, i))
    return pl.pallas_call(
        _route_kernel,
        grid=(t // tm,),
        in_specs=[pl.BlockSpec((tm, d), lambda i: (i, 0)),
                  pl.BlockSpec((1, 1, d), lambda i: (row(i), 0, sc_idx)),
                  pl.BlockSpec((1, 1, d), lambda i: (row(i), 0, sh_idx)),
                  pl.BlockSpec((N_EXPERTS, d), lambda i: (0, 0)),
                  pl.BlockSpec((N_GROUPS, N_EXPERTS // N_GROUPS, 1), lambda i: (0, 0, 0)),
                  pl.BlockSpec((tm, tm), lambda i: (0, 0))],
        out_specs=[pl.BlockSpec((tm * ROW_TILE, LANES), lambda i: (i, 0)), kt, kt, kt,
                   pl.BlockSpec((N_EXPERTS, 1), lambda i: (0, 0))],
        out_shape=[jax.ShapeDtypeStruct((t * ROW_TILE, LANES), jnp.uint32),
                   jax.ShapeDtypeStruct((TOP_K, t), jnp.int32),
                   jax.ShapeDtypeStruct((TOP_K, t), F32),
                   jax.ShapeDtypeStruct((TOP_K, t), jnp.int32),
                   jax.ShapeDtypeStruct((N_EXPERTS, 1), jnp.int32)],
        scratch_shapes=[pltpu.VMEM((N_EXPERTS, 1), F32)],
        compiler_params=_cparams(("arbitrary",)),
        name="moe_route",
    )(x, mod, mod, router_w.T, router_bias.astype(F32).reshape(N_GROUPS, N_EXPERTS // N_GROUPS, 1),
      jnp.asarray(tri, BF16))


def _dispatch_plan(top_e, rank, counts, n_blocks):
    counts = counts.reshape(N_EXPERTS)
    padded = (counts + MOE_BLOCK - 1) // MOE_BLOCK * MOE_BLOCK
    pad_ends = jnp.cumsum(padded)
    pad_starts = pad_ends - padded
    onehot = top_e[:, :, None] == jnp.arange(N_EXPERTS, dtype=jnp.int32)
    dest = jnp.sum(jnp.where(onehot, pad_starts, 0), axis=-1) + rank
    block_start = jnp.arange(n_blocks, dtype=jnp.int32) * MOE_BLOCK
    block_e = jnp.minimum(jnp.sum(pad_ends[None, :] <= block_start[:, None], axis=1), N_EXPERTS - 1)
    n_used = (pad_ends[-1] // MOE_BLOCK).astype(jnp.int32).reshape(1)
    return dest, pad_ends.astype(jnp.int32), block_e.astype(jnp.int32), n_used


def _dest_tiles(dest, tm):
    k, t = dest.shape
    return dest.reshape(k, t // tm, tm).transpose(1, 0, 2).reshape(t // tm, 1, k * tm)


def _dispatch_kernel(pe_ref, nu_ref, dest_ref, h_ref, xs_ref, dest_smem, zero_ref, sems, *, tm, n_blocks):
    i = pl.program_id(0)

    def zero_fill(slot0):
        rows = MOE_BLOCK * ROW_TILE
        row0 = slot0 * ROW_TILE if isinstance(slot0, int) else pl.multiple_of(slot0 * ROW_TILE, rows)
        return pltpu.make_async_copy(zero_ref, xs_ref.at[pl.ds(row0, rows)], sems.at[1])

    @pl.when(i == 0)
    def _():
        zero_ref[...] = jnp.zeros_like(zero_ref)
        for e in range(N_EXPERTS):
            start = pe_ref[e - 1] if e else 0

            @pl.when(pe_ref[e] > start)
            def _():
                zero_fill(pe_ref[e] - MOE_BLOCK).start()

        def start_unused(b, c):
            zero_fill(b * MOE_BLOCK).start()
            return c

        lax.fori_loop(nu_ref[0], n_blocks, start_unused, 0)
        for e in range(N_EXPERTS):
            start = pe_ref[e - 1] if e else 0

            @pl.when(pe_ref[e] > start)
            def _():
                zero_fill(0).wait()

        def wait_unused(b, c):
            zero_fill(0).wait()
            return c

        lax.fori_loop(nu_ref[0], n_blocks, wait_unused, 0)

    load = pltpu.make_async_copy(dest_ref.at[0, 0], dest_smem, sems.at[0])
    load.start()
    load.wait()

    def issue(t, c):
        for k in range(TOP_K):
            d = dest_smem[k * tm + t]
            pltpu.make_async_copy(_token_tile(h_ref, t), _token_tile(xs_ref, d), sems.at[2]).start(priority=k % 2)
        return c

    lax.fori_loop(0, tm, issue, 0)
    for k in range(TOP_K):
        pltpu.make_async_copy(h_ref, xs_ref.at[pl.ds(0, tm * ROW_TILE)], sems.at[2]).wait()


def _dispatch(h, dest, pad_ends, n_used, n_blocks, tm=512):
    t = h.shape[0] // ROW_TILE
    grid_spec = pltpu.PrefetchScalarGridSpec(
        num_scalar_prefetch=2,
        grid=(t // tm,),
        in_specs=[pl.BlockSpec((1, 1, TOP_K * tm), lambda i, pe, nu: (i, 0, 0)),
                  pl.BlockSpec((tm * ROW_TILE, LANES), lambda i, pe, nu: (i, 0))],
        out_specs=pl.BlockSpec(memory_space=pl.ANY),
        scratch_shapes=[pltpu.SMEM((TOP_K * tm,), jnp.int32),
                        pltpu.VMEM((MOE_BLOCK * ROW_TILE, LANES), jnp.uint32),
                        pltpu.SemaphoreType.DMA((3,))],
    )
    return pl.pallas_call(
        functools.partial(_dispatch_kernel, tm=tm, n_blocks=n_blocks),
        grid_spec=grid_spec,
        out_shape=jax.ShapeDtypeStruct((n_blocks * MOE_BLOCK * ROW_TILE, LANES), jnp.uint32),
        compiler_params=_cparams(("arbitrary",)),
        name="moe_dispatch",
    )(pad_ends, n_used, _dest_tiles(dest, tm), h)


def _experts_kernel(be_ref, nu_ref, first_ref, slot_ref, next_ref, x_ref, wi_ref, wo_ref, o_ref,
                    gbuf, ubuf, obuf, wg_scr, wu_scr, wo_scr, sems, *, layer):
    blk = pl.program_id(0)
    half = ROW_TILE * LANES
    f = wg_scr.shape[1]

    def weight_copies(e, slot):
        return (pltpu.make_async_copy(wi_ref.at[layer, e, :, pl.ds(0, f)], gbuf.at[slot], sems.at[slot, 0]),
                pltpu.make_async_copy(wi_ref.at[layer, e, :, pl.ds(f, f)], ubuf.at[slot], sems.at[slot, 1]),
                pltpu.make_async_copy(wo_ref.at[layer, e], obuf.at[slot], sems.at[slot, 2]))

    @pl.when(blk == 0)
    def _():
        for c in weight_copies(be_ref[0], 0):
            c.start()

    @pl.when(first_ref[blk] == 1)
    def _():
        slot = slot_ref[blk]
        for c in weight_copies(be_ref[blk], slot):
            c.wait()
        wg_scr[...] = gbuf[slot].astype(BF16)
        wu_scr[...] = ubuf[slot].astype(BF16)
        wo_scr[...] = obuf[slot].astype(BF16)

        @pl.when(next_ref[blk] >= 0)
        def _():
            for c in weight_copies(next_ref[blk], 1 - slot):
                c.start()

    @pl.when(blk < nu_ref[0])
    def _():
        lo, hi = _unpack_halves(_load_token_tiles(x_ref, MOE_BLOCK))
        lo, hi = lo.astype(BF16), hi.astype(BF16)
        gate = _dot(lo, wg_scr[0:half, :]) + _dot(hi, wg_scr[half:, :])
        up = _dot(lo, wu_scr[0:half, :]) + _dot(hi, wu_scr[half:, :])
        a = (gate * _sigmoid(gate) * up).astype(BF16)
        _store_token_tiles(o_ref, _pack_halves(_dot(a, wo_scr[...])))

    @pl.when(blk >= nu_ref[0])
    def _():
        o_ref[...] = jnp.zeros_like(o_ref)


def _experts(xs, block_e, n_used, w_in, w_out, layer):
    d = w_in.shape[2]
    f = w_out.shape[2]
    assert d == 2 * ROW_TILE * LANES
    n_blocks = xs.shape[0] // (MOE_BLOCK * ROW_TILE)
    first = jnp.concatenate([jnp.ones((1,), jnp.int32), (block_e[1:] != block_e[:-1]).astype(jnp.int32)])
    slot = (jnp.cumsum(first) - 1) % 2
    later = jnp.where(block_e[None, :] > block_e[:, None], block_e[None, :], N_EXPERTS)
    nxt = jnp.min(later, axis=1)
    nxt = jnp.where(nxt >= N_EXPERTS, -1, nxt)
    grid_spec = pltpu.PrefetchScalarGridSpec(
        num_scalar_prefetch=5,
        grid=(n_blocks,),
        in_specs=[pl.BlockSpec((MOE_BLOCK * ROW_TILE, LANES), lambda i, *_: (i, 0)),
                  pl.BlockSpec(memory_space=pl.ANY),
                  pl.BlockSpec(memory_space=pl.ANY)],
        out_specs=pl.BlockSpec((MOE_BLOCK * ROW_TILE, LANES), lambda i, *_: (i, 0)),
        scratch_shapes=[pltpu.VMEM((2, d, f), F32), pltpu.VMEM((2, d, f), F32), pltpu.VMEM((2, f, d), F32),
                        pltpu.VMEM((d, f), BF16), pltpu.VMEM((d, f), BF16), pltpu.VMEM((f, d), BF16),
                        pltpu.SemaphoreType.DMA((2, 3))],
    )
    return pl.pallas_call(
        functools.partial(_experts_kernel, layer=layer),
        grid_spec=grid_spec,
        out_shape=jax.ShapeDtypeStruct(xs.shape, jnp.uint32),
        compiler_params=_cparams(("arbitrary",)),
        name="moe_experts",
    )(block_e, n_used, first, slot.astype(jnp.int32), nxt.astype(jnp.int32), xs, w_in, w_out)


def _ffn_out_kernel(dest_ref, dest_next_ref, x_ref, gate_ref, h_ref, rg_ref, wg_ref, wu_ref, wo_ref, lng_ref,
                    lnb_ref, ys_ref, o_ref, dest_smem, rows_ref, sems, *, alpha, tm, n_steps):
    i = pl.program_id(0)
    slot = i % 2
    half = ROW_TILE * LANES

    def gather(tile_ref, into):
        load = pltpu.make_async_copy(tile_ref.at[0, 0], dest_smem, sems.at[0])
        load.start()
        load.wait()

        def issue(t, c):
            for k in range(TOP_K):
                d = dest_smem[k * tm + t]
                pltpu.make_async_copy(_token_tile(ys_ref, d), _token_tile(rows_ref.at[into, k], t),
                                      sems.at[1 + into]).start(priority=k % 2)
            return c

        lax.fori_loop(0, tm, issue, 0)

    @pl.when(i == 0)
    def _():
        gather(dest_ref, 0)

    @pl.when(i + 1 < n_steps)
    def _():
        gather(dest_next_ref, 1 - slot)

    lo, hi = _unpack_halves(_load_token_tiles(h_ref, tm))
    lo, hi = lo.astype(BF16), hi.astype(BF16)
    gate = _dot(lo, wg_ref[0:half, :]) + _dot(hi, wg_ref[half:, :])
    up = _dot(lo, wu_ref[0:half, :]) + _dot(hi, wu_ref[half:, :])
    a = (gate * _sigmoid(gate) * up).astype(BF16)
    y = _dot(a, wo_ref[...])
    y_lo, y_hi = y[:, :half], y[:, half:]

    rg = rg_ref[...]
    for k in range(TOP_K):
        pltpu.make_async_copy(ys_ref.at[pl.ds(0, tm * ROW_TILE)], rows_ref.at[slot, k], sems.at[1 + slot]).wait()
    for k in range(TOP_K):
        r_lo, r_hi = _unpack_halves(_load_token_tiles(rows_ref.at[slot, k], tm))
        y_lo = y_lo + r_lo * rg[:, k:k + 1]
        y_hi = y_hi + r_hi * rg[:, k:k + 1]
    z = alpha * x_ref[...] + gate_ref[0] * jnp.concatenate([y_lo, y_hi], axis=-1)
    o_ref[...] = _layer_norm(z, lng_ref[...], lnb_ref[...])


def _ffn_out(x, mod, gate_idx, h, ys, dest, route_gate, sh_in, sh_out, ln_g, ln_b, alpha, geom, tm=256):
    t, d = x.shape
    f = sh_out.shape[0]
    assert d == 2 * ROW_TILE * LANES
    n_steps = t // tm
    row = functools.partial(_mod_row, tm=tm, **geom)
    tiles = _dest_tiles(dest, tm)
    return pl.pallas_call(
        functools.partial(_ffn_out_kernel, alpha=alpha, tm=tm, n_steps=n_steps),
        grid=(n_steps,),
        in_specs=[pl.BlockSpec((1, 1, TOP_K * tm), lambda i: (i, 0, 0)),
                  pl.BlockSpec((1, 1, TOP_K * tm), lambda i: (jnp.minimum(i + 1, n_steps - 1), 0, 0)),
                  pl.BlockSpec((tm, d), lambda i: (i, 0)),
                  pl.BlockSpec((1, 1, d), lambda i: (row(i), 0, gate_idx)),
                  pl.BlockSpec((tm * ROW_TILE, LANES), lambda i: (i, 0)),
                  pl.BlockSpec((tm, TOP_K), lambda i: (i, 0)),
                  pl.BlockSpec((d, f), lambda i: (0, 0)),
                  pl.BlockSpec((d, f), lambda i: (0, 1)),
                  pl.BlockSpec((f, d), lambda i: (0, 0)),
                  pl.BlockSpec((1, d), lambda i: (0, 0)),
                  pl.BlockSpec((1, d), lambda i: (0, 0)),
                  pl.BlockSpec(memory_space=pl.ANY)],
        out_specs=pl.BlockSpec((tm, d), lambda i: (i, 0)),
        out_shape=jax.ShapeDtypeStruct((t, d), F32),
        scratch_shapes=[pltpu.SMEM((TOP_K * tm,), jnp.int32),
                        pltpu.VMEM((2, TOP_K, tm * ROW_TILE, LANES), jnp.uint32),
                        pltpu.SemaphoreType.DMA((3,))],
        compiler_params=_cparams(("arbitrary",)),
        name="ffn_out",
    )(tiles, tiles, x, mod, h, route_gate, sh_in, sh_in, sh_out, ln_g.reshape(1, d), ln_b.reshape(1, d), ys)


def _moe_sublayer(x, mod, geom, layer, router_w, router_bias, w_in, w_out, sh_in, sh_out, ln_g, ln_b, alpha):
    h, top_e, gate, rank, counts = _route(x, mod, 4, 3, router_w, router_bias, geom)
    n_blocks = x.shape[0] * TOP_K // MOE_BLOCK + N_EXPERTS
    dest, pad_ends, block_e, n_used = _dispatch_plan(top_e, rank, counts, n_blocks)
    xs = _dispatch(h, dest, pad_ends, n_used, n_blocks)
    ys = _experts(xs, block_e, n_used, w_in, w_out, layer)
    return _ffn_out(x, mod, 5, h, ys, dest, gate.T, sh_in.astype(BF16), sh_out.astype(BF16), ln_g, ln_b, alpha,
                    geom)


def kernel(x, c, ctx, c_ctx, ada_w, ada_b, ln_mix_g, ln_mix_b, ln_ffn_g, ln_ffn_b, ab_w_in, ab_w_out, diff_lambda, diff_subln_g, gqa_q_norm_g, gqa_k_norm_g, mla_w_down, mla_q_norm_g, mla_w_uq, mla_kv_norm_g, mla_w_ukv, mla_w_o, router_w, router_bias, expert_w_in, expert_w_out, shared_w_in, shared_w_out):
    bsz, seq, d = x.shape
    n_ctx = ctx.shape[1]
    depth = ada_w.shape[0]
    assert depth == 2 and seq % GRID_W == 0 and bsz + 1 <= MOD_ROWS
    alpha = (2 * depth) ** 0.25
    n_ctx_rows = bsz * n_ctx
    n_lat_rows = bsz * seq
    geom_all = dict(n_ctx_rows=n_ctx_rows, seq=seq, ctx_row=bsz)
    geom_lat = dict(n_ctx_rows=0, seq=seq, ctx_row=bsz)

    tables64 = _rope_tables(seq // GRID_W, 64)
    tables128 = _rope_tables(seq // GRID_W, HEAD_DIM)

    cvec = jnp.zeros((MOD_ROWS, d), F32).at[:bsz].set(c).at[bsz].set(c_ctx)
    mods = _ada_mod(cvec, ada_w, ada_b)
    mods = mods.reshape(depth, MOD_ROWS, 1, 6 * d)

    xt = jnp.concatenate([ctx.reshape(n_ctx_rows, d), x.reshape(n_lat_rows, d)], axis=0)

    mod = mods[0]
    n_diff = d // (2 * HEAD_DIM)
    n_gqa = d // (2 * HEAD_DIM)
    n_gqa_kv = n_gqa // 4
    proj = _modproj(xt, mod, 1, 0, ab_w_in[0].astype(BF16), BF16, geom_all)
    lam_init = 0.8 - 0.6 * math.exp(-0.3 * 0)
    oa = _diff_attn(proj, tables64, diff_lambda[0], diff_subln_g[0], lam_init, bsz, n_ctx, seq, n_diff)
    ob = _gqa_attn(proj, tables128, gqa_q_norm_g[0], gqa_k_norm_g[0], 3 * n_diff, bsz, n_ctx, seq, n_gqa, n_gqa_kv)
    w_out = ab_w_out[0].astype(BF16)
    wa, wb = w_out[:n_diff * HEAD_DIM], w_out[n_diff * HEAD_DIM:]
    t_all = n_ctx_rows + n_lat_rows
    xt = _resln(xt, mod, 2, [oa, ob], [wa, wb], ln_mix_g[0], ln_mix_b[0], alpha, geom_all, 0, t_all, 0)
    xt = _moe_sublayer(xt, mod, geom_all, 0, router_w[0], router_bias[0], expert_w_in, expert_w_out,
                       shared_w_in[0], shared_w_out[0], ln_ffn_g[0], ln_ffn_b[0], alpha)

    mod = mods[1]
    n_mla = d // HEAD_DIM
    qk = MLA_NOPE_DIM + MLA_ROPE_DIM
    w_down = jnp.pad(mla_w_down[0], ((0, 0), (0, LANES - MLA_ROPE_DIM))).astype(BF16)
    down = _modproj(xt, mod, 1, 0, w_down, F32, geom_all)
    w_uq = mla_w_uq[0].reshape(MLA_Q_RANK, n_mla, qk)
    w_uq_rope = jnp.pad(w_uq[:, :, MLA_NOPE_DIM:], ((0, 0), (0, 0), (0, LANES - MLA_ROPE_DIM)))
    w_uq = jnp.concatenate([w_uq[:, :, :MLA_NOPE_DIM].reshape(MLA_Q_RANK, -1),
                            w_uq_rope.reshape(MLA_Q_RANK, -1)], axis=1).astype(BF16)
    lat_blocks = n_ctx_rows // 1024
    q = _normproj(down, 0, mla_q_norm_g[0], w_uq, lat_blocks, n_lat_rows)
    kv = _normproj(down, 1, mla_kv_norm_g[0], mla_w_ukv[0].astype(BF16), 0, t_all)
    kpe_col = (MLA_Q_RANK + MLA_KV_RANK) // LANES
    o = _mla_attn(q, kv, down, tables64, bsz, n_ctx, seq, n_mla, kpe_col)
    xl = _resln(xt, mod, 2, [o], [mla_w_o[0].astype(BF16)], ln_mix_g[1], ln_mix_b[1], alpha, geom_all,
                n_ctx_rows // 512, n_lat_rows, 0)
    xl = _moe_sublayer(xl, mod, geom_lat, 1, router_w[1], router_bias[1], expert_w_in, expert_w_out,
                       shared_w_in[1], shared_w_out[1], ln_ffn_g[1], ln_ffn_b[1], alpha)
    return xl.reshape(bsz, seq, d)
```

```python
import functools
import math

import numpy as np
import jax
import jax.numpy as jnp
from jax import lax
from jax.experimental import pallas as pl
from jax.experimental.pallas import tpu as pltpu

F32 = jnp.float32
BF16 = jnp.bfloat16

GRID_W = 64
ROPE_THETA = 10000.0
HEAD_DIM = 128
DIFF_QK_DIM = 64
N_EXPERTS = 64
EXPERT_DIM = 512
TOP_K = 8
N_GROUPS = 8
TOPK_GROUPS = 4
ROUTE_SCALE = 2.5
MLA_Q_RANK = 512
MLA_KV_RANK = 512
MLA_NOPE_DIM = 128
MLA_ROPE_DIM = 64
MLA_V_DIM = 128
LANES = 128

MOD_ROWS = 16
VMEM_LIMIT = 56 * 1024 * 1024
MOE_BLOCK = 256


def _cparams(sem):
    return pltpu.CompilerParams(dimension_semantics=sem, vmem_limit_bytes=VMEM_LIMIT)


def _dot(a, b):
    return jnp.dot(a, b, preferred_element_type=F32)


def _dot_nt(a, b):
    return lax.dot_general(a, b, (((1,), (1,)), ((), ())), preferred_element_type=F32)


def _dot_hi(a, b):
    return lax.dot_general(a, b, (((1,), (0,)), ((), ())), precision=lax.Precision.HIGHEST,
                           preferred_element_type=F32)


def _sigmoid(x):
    return 1.0 / (1.0 + jnp.exp(-x))


def _rope(x, cos, sin_up, sin_dn, shift):
    return (x * cos + pltpu.roll(x, LANES - shift, 1) * sin_up + pltpu.roll(x, shift, 1) * sin_dn)


def _rope_tables(rows, rot_dim):
    t = jnp.arange(rows * GRID_W)
    row = (t // GRID_W).astype(F32)
    col = (t % GRID_W).astype(F32)
    axis_dim = rot_dim // 2
    quarter = rot_dim // 4
    inv_freq = 1.0 / (ROPE_THETA ** (jnp.arange(0, axis_dim, 2, dtype=F32) / axis_dim))
    lane = np.arange(LANES) % rot_dim
    is_col = lane >= axis_dim
    within = lane % axis_dim
    first = within < quarter
    freq = within % quarter
    ang = jnp.where(jnp.asarray(is_col)[None, :], col[:, None], row[:, None]) * inv_freq[freq][None, :]
    cos = jnp.cos(ang).astype(F32)
    sin = jnp.sin(ang).astype(F32)
    first = jnp.asarray(first)[None, :]
    sin_up = jnp.where(first, -sin, 0.0)
    sin_dn = jnp.where(first, 0.0, sin)
    return cos, sin_up, sin_dn, quarter


LOG2E = 1.4426950408889634
HEADS_PER_STEP = 4


def _exp2_rows(s):
    e = jnp.exp2(s - jnp.max(s, axis=-1, keepdims=True))
    return e, 1.0 / jnp.sum(e, axis=-1, keepdims=True)


def _rms(x, g, eps):
    return x * lax.rsqrt(jnp.mean(x * x, axis=-1, keepdims=True) + eps) * g


def _ada_kernel(c_ref, w_ref, b_ref, o_ref):
    c = c_ref[...]
    act = c * _sigmoid(c)
    o_ref[0] = _dot_hi(act, w_ref[0]) + b_ref[0]


def _ada_mod(cvec, ada_w, ada_b):
    depth, d, n = ada_w.shape
    tn = 1536
    return pl.pallas_call(
        _ada_kernel,
        grid=(depth, n // tn),
        in_specs=[pl.BlockSpec((MOD_ROWS, d), lambda l, j: (0, 0)),
                  pl.BlockSpec((1, d, tn), lambda l, j: (l, 0, j)),
                  pl.BlockSpec((1, 1, tn), lambda l, j: (l, 0, j))],
        out_specs=pl.BlockSpec((1, MOD_ROWS, tn), lambda l, j: (l, 0, j)),
        out_shape=jax.ShapeDtypeStruct((depth, MOD_ROWS, n), F32),
        compiler_params=_cparams(("arbitrary", "arbitrary")),
        name="ada_mod",
    )(cvec, ada_w, ada_b.reshape(depth, 1, n))


def _mod_row(i, tm, n_ctx_rows, seq, ctx_row):
    r0 = i * tm
    return jnp.where(r0 < n_ctx_rows, ctx_row, (r0 - n_ctx_rows) // seq)


def _modproj_kernel(x_ref, sc_ref, sh_ref, w_ref, o_ref, h_ref):
    @pl.when(pl.program_id(1) == 0)
    def _():
        h_ref[...] = (x_ref[...] * (1.0 + sc_ref[0]) + sh_ref[0]).astype(h_ref.dtype)

    o_ref[...] = _dot(h_ref[...], w_ref[...]).astype(o_ref.dtype)


def _modproj(x, mod, sc_idx, sh_idx, w, out_dtype, geom, tm=1024, tn=512):
    t, d = x.shape
    n = w.shape[1]
    tn = tn if n % tn == 0 else n
    row = functools.partial(_mod_row, tm=tm, **geom)
    return pl.pallas_call(
        _modproj_kernel,
        grid=(t // tm, n // tn),
        in_specs=[pl.BlockSpec((tm, d), lambda i, j: (i, 0)),
                  pl.BlockSpec((1, 1, d), lambda i, j: (row(i), 0, sc_idx)),
                  pl.BlockSpec((1, 1, d), lambda i, j: (row(i), 0, sh_idx)),
                  pl.BlockSpec((d, tn), lambda i, j: (0, j))],
        out_specs=pl.BlockSpec((tm, tn), lambda i, j: (i, j)),
        out_shape=jax.ShapeDtypeStruct((t, n), out_dtype),
        scratch_shapes=[pltpu.VMEM((tm, d), BF16)],
        compiler_params=_cparams(("arbitrary", "arbitrary")),
        name="modproj",
    )(x, mod, mod, w)


def _normproj_kernel(x_ref, g_ref, w_ref, o_ref, h_ref):
    @pl.when(pl.program_id(1) == 0)
    def _():
        h_ref[...] = _rms(x_ref[...], g_ref[...], 1e-6).astype(h_ref.dtype)

    o_ref[...] = _dot(h_ref[...], w_ref[...]).astype(o_ref.dtype)


def _normproj(x, col_block, g, w, row_off_blocks, n_rows, tm=1024, tn=1024):
    k = w.shape[0]
    n = w.shape[1]
    return pl.pallas_call(
        _normproj_kernel,
        grid=(n_rows // tm, n // tn),
        in_specs=[pl.BlockSpec((tm, k), lambda i, j: (i + row_off_blocks, col_block)),
                  pl.BlockSpec((1, k), lambda i, j: (0, 0)),
                  pl.BlockSpec((k, tn), lambda i, j: (0, j))],
        out_specs=pl.BlockSpec((tm, tn), lambda i, j: (i, j)),
        out_shape=jax.ShapeDtypeStruct((n_rows, n), BF16),
        scratch_shapes=[pltpu.VMEM((tm, k), BF16)],
        compiler_params=_cparams(("arbitrary", "arbitrary")),
        name="normproj",
    )(x, g.reshape(1, k), w)


def _diff_lambda(lam_ref, lam_init):
    lp = lam_ref[...]
    s01 = jnp.sum(lp[0:1] * lp[1:2], axis=-1, keepdims=True)
    s23 = jnp.sum(lp[2:3] * lp[3:4], axis=-1, keepdims=True)
    return jnp.exp(s01) - jnp.exp(s23) + lam_init


def _diff_logits(q, k):
    lane = lax.broadcasted_iota(jnp.int32, q.shape, 1)
    qs = q * (DIFF_QK_DIM ** -0.5 * LOG2E)
    q1 = jnp.where(lane < DIFF_QK_DIM, qs, 0.0).astype(BF16)
    q2 = jnp.where(lane >= DIFF_QK_DIM, qs, 0.0).astype(BF16)
    return _dot_nt(q1, k), _dot_nt(q2, k)


def _diff_combine(s1, s2, v, lam, g, lam_init):
    e1, r1 = _exp2_rows(s1)
    e2, r2 = _exp2_rows(s2)
    a = (e1 - e2 * (lam * r2 / r1)).astype(BF16)
    return _rms(_dot(a, v) * r1, g, 1e-5) * (1.0 - lam_init)


def _head_cols(j):
    return slice(j * LANES, (j + 1) * LANES)


def _diff_attn_kernel(q_ref, kc_ref, kl_ref, vc_ref, vl_ref, cos_ref, su_ref, sd_ref, lam_ref, g_ref,
                      o_ref, k_scr, v_scr, *, n_ctx, tq, shift, lam_init):
    i = pl.program_id(2)
    heads = range(HEADS_PER_STEP)

    @pl.when(i == 0)
    def _prep():
        for j in heads:
            k_scr[j, 0:n_ctx, :] = kc_ref[:, _head_cols(j)]
            kl = kl_ref[:, _head_cols(j)].astype(F32)
            k_scr[j, n_ctx:, :] = _rope(kl, cos_ref[...], su_ref[...], sd_ref[...], shift).astype(BF16)
            v_scr[j, 0:n_ctx, :] = vc_ref[:, _head_cols(j)]
            v_scr[j, n_ctx:, :] = vl_ref[:, _head_cols(j)]

    lam = _diff_lambda(lam_ref, lam_init)
    g = g_ref[...]

    def attend(qs, n_keys):
        logits = [_diff_logits(qs[j], k_scr[j, 0:n_keys, :]) for j in heads]
        for j in heads:
            o = _diff_combine(*logits[j], v_scr[j, 0:n_keys, :], lam, g, lam_init)
            o_ref[:, _head_cols(j)] = o.astype(o_ref.dtype)

    @pl.when(i == 0)
    def _ctx_queries():
        attend([q_ref[:, _head_cols(j)].astype(F32) for j in heads], n_ctx)

    @pl.when(i > 0)
    def _lat_queries():
        r0 = pl.multiple_of((i - 1) * tq, tq)
        cos, su, sd = cos_ref[pl.ds(r0, tq), :], su_ref[pl.ds(r0, tq), :], sd_ref[pl.ds(r0, tq), :]
        attend([_rope(q_ref[:, _head_cols(j)].astype(F32), cos, su, sd, shift) for j in heads],
               k_scr.shape[1])


def _q_block(b, i, bsz, n_ctx, seq, tq):
    return jnp.where(i == 0, b * (n_ctx // tq), (bsz * n_ctx + b * seq) // tq + i - 1)


def _diff_attn(proj, tables, lam_params, subln_g, lam_init, bsz, n_ctx, seq, n_heads, tq=256):
    t = proj.shape[0]
    cos, su, sd, shift = tables
    hp = HEADS_PER_STEP
    wide = hp * LANES
    n_pairs = n_heads // hp
    qb = functools.partial(_q_block, bsz=bsz, n_ctx=n_ctx, seq=seq, tq=tq)
    lat0 = bsz * n_ctx // seq
    kern = functools.partial(_diff_attn_kernel, n_ctx=n_ctx, tq=tq, shift=shift, lam_init=lam_init)
    tab = pl.BlockSpec((seq, LANES), lambda b, h, i: (0, 0))
    return pl.pallas_call(
        kern,
        grid=(bsz, n_pairs, 1 + seq // tq),
        in_specs=[pl.BlockSpec((tq, wide), lambda b, h, i: (qb(b, i), h)),
                  pl.BlockSpec((n_ctx, wide), lambda b, h, i: (b, n_pairs + h)),
                  pl.BlockSpec((seq, wide), lambda b, h, i: (lat0 + b, n_pairs + h)),
                  pl.BlockSpec((n_ctx, wide), lambda b, h, i: (b, 2 * n_pairs + h)),
                  pl.BlockSpec((seq, wide), lambda b, h, i: (lat0 + b, 2 * n_pairs + h)),
                  tab, tab, tab,
                  pl.BlockSpec((4, DIFF_QK_DIM), lambda b, h, i: (0, 0)),
                  pl.BlockSpec((1, LANES), lambda b, h, i: (0, 0))],
        out_specs=pl.BlockSpec((tq, wide), lambda b, h, i: (qb(b, i), h)),
        out_shape=jax.ShapeDtypeStruct((t, n_heads * LANES), BF16),
        scratch_shapes=[pltpu.VMEM((hp, n_ctx + seq, LANES), BF16), pltpu.VMEM((hp, n_ctx + seq, LANES), BF16)],
        compiler_params=_cparams(("arbitrary", "arbitrary", "arbitrary")),
        name="diff_attn",
    )(proj, proj, proj, proj, proj, cos, su, sd, lam_params, subln_g.reshape(1, LANES))


def _softmax_pv(s, v):
    e, r = _exp2_rows(s)
    return _dot(e.astype(BF16), v) * r


def _gqa_attn_kernel(q_ref, kc_ref, kl_ref, vc_ref, vl_ref, cos_ref, su_ref, sd_ref, qg_ref, kg_ref,
                     o_ref, k_scr, v_scr, *, n_ctx, tq, shift):
    g = pl.program_id(2)
    i = pl.program_id(3)
    heads = range(HEADS_PER_STEP)

    @pl.when((g == 0) & (i == 0))
    def _prep():
        kg = kg_ref[...]
        k_scr[0:n_ctx, :] = _rms(kc_ref[...].astype(F32), kg, 1e-6).astype(BF16)
        kl = _rms(kl_ref[...].astype(F32), kg, 1e-6)
        k_scr[n_ctx:, :] = _rope(kl, cos_ref[...], su_ref[...], sd_ref[...], shift).astype(BF16)
        v_scr[0:n_ctx, :] = vc_ref[...]
        v_scr[n_ctx:, :] = vl_ref[...]

    scale = HEAD_DIM ** -0.5 * LOG2E

    def attend(qs, n_keys):
        logits = [_dot_nt((qs[j] * scale).astype(BF16), k_scr[0:n_keys, :]) for j in heads]
        for j in heads:
            o_ref[:, _head_cols(j)] = _softmax_pv(logits[j], v_scr[0:n_keys, :]).astype(o_ref.dtype)

    @pl.when(i == 0)
    def _ctx_queries():
        attend([_rms(q_ref[:, _head_cols(j)].astype(F32), qg_ref[...], 1e-6) for j in heads], n_ctx)

    @pl.when(i > 0)
    def _lat_queries():
        r0 = pl.multiple_of((i - 1) * tq, tq)
        cos, su, sd = cos_ref[pl.ds(r0, tq), :], su_ref[pl.ds(r0, tq), :], sd_ref[pl.ds(r0, tq), :]
        attend([_rope(_rms(q_ref[:, _head_cols(j)].astype(F32), qg_ref[...], 1e-6), cos, su, sd, shift)
                for j in heads], k_scr.shape[0])


def _gqa_attn(proj, tables, q_norm_g, k_norm_g, col0, bsz, n_ctx, seq, n_heads, n_kv, tq=256):
    t = proj.shape[0]
    cos, su, sd, shift = tables
    hp = HEADS_PER_STEP
    wide = hp * LANES
    pairs = n_heads // n_kv // hp
    assert col0 % hp == 0
    qb = functools.partial(_q_block, bsz=bsz, n_ctx=n_ctx, seq=seq, tq=tq)
    lat0 = bsz * n_ctx // seq
    kcol = col0 + n_heads
    vcol = kcol + n_kv
    kern = functools.partial(_gqa_attn_kernel, n_ctx=n_ctx, tq=tq, shift=shift)
    tab = pl.BlockSpec((seq, LANES), lambda b, h, g, i: (0, 0))
    vec = pl.BlockSpec((1, LANES), lambda b, h, g, i: (0, 0))
    return pl.pallas_call(
        kern,
        grid=(bsz, n_kv, pairs, 1 + seq // tq),
        in_specs=[pl.BlockSpec((tq, wide), lambda b, h, g, i: (qb(b, i), col0 // hp + h * pairs + g)),
                  pl.BlockSpec((n_ctx, LANES), lambda b, h, g, i: (b, kcol + h)),
                  pl.BlockSpec((seq, LANES), lambda b, h, g, i: (lat0 + b, kcol + h)),
                  pl.BlockSpec((n_ctx, LANES), lambda b, h, g, i: (b, vcol + h)),
                  pl.BlockSpec((seq, LANES), lambda b, h, g, i: (lat0 + b, vcol + h)),
                  tab, tab, tab, vec, vec],
        out_specs=pl.BlockSpec((tq, wide), lambda b, h, g, i: (qb(b, i), h * pairs + g)),
        out_shape=jax.ShapeDtypeStruct((t, n_heads * LANES), BF16),
        scratch_shapes=[pltpu.VMEM((n_ctx + seq, LANES), BF16), pltpu.VMEM((n_ctx + seq, LANES), BF16)],
        compiler_params=_cparams(("arbitrary", "arbitrary", "arbitrary", "arbitrary")),
        name="gqa_attn",
    )(proj, proj, proj, proj, proj, cos, su, sd, q_norm_g.reshape(1, LANES), k_norm_g.reshape(1, LANES))


def _mla_attn_kernel(qn_ref, qp_ref, kvc_ref, kvl_ref, kpc_ref, kpl_ref, cos_ref, su_ref, sd_ref,
                     o_ref, k_scr, v_scr, *, n_ctx, tq, shift):
    i = pl.program_id(2)

    @pl.when(i == 0)
    def _prep():
        kpc = kpc_ref[...].astype(BF16)
        kpl = _rope(kpl_ref[...], cos_ref[...], su_ref[...], sd_ref[...], shift).astype(BF16)
        for j in range(HEADS_PER_STEP):
            c0 = 2 * j * LANES
            k_scr[j, 0:n_ctx, 0:LANES] = kvc_ref[:, c0:c0 + LANES]
            k_scr[j, n_ctx:, 0:LANES] = kvl_ref[:, c0:c0 + LANES]
            k_scr[j, 0:n_ctx, LANES:] = kpc
            k_scr[j, n_ctx:, LANES:] = kpl
            v_scr[j, 0:n_ctx, :] = kvc_ref[:, c0 + LANES:c0 + 2 * LANES]
            v_scr[j, n_ctx:, :] = kvl_ref[:, c0 + LANES:c0 + 2 * LANES]

    scale = (MLA_NOPE_DIM + MLA_ROPE_DIM) ** -0.5 * LOG2E
    r0 = pl.multiple_of(i * tq, tq)
    cos, su, sd = cos_ref[pl.ds(r0, tq), :], su_ref[pl.ds(r0, tq), :], sd_ref[pl.ds(r0, tq), :]
    logits = []
    for j in range(HEADS_PER_STEP):
        qp = _rope(qp_ref[:, _head_cols(j)].astype(F32), cos, su, sd, shift)
        q = jnp.concatenate([(qn_ref[:, _head_cols(j)].astype(F32) * scale).astype(BF16), (qp * scale).astype(BF16)],
                            axis=-1)
        logits.append(_dot_nt(q, k_scr[j]))
    for j in range(HEADS_PER_STEP):
        o_ref[:, _head_cols(j)] = _softmax_pv(logits[j], v_scr[j]).astype(o_ref.dtype)


def _mla_attn(q, kv, down, tables, bsz, n_ctx, seq, n_heads, kpe_col, tq=256):
    cos, su, sd, shift = tables
    lat0 = bsz * n_ctx // seq
    nq = seq // tq
    hp = HEADS_PER_STEP
    kern = functools.partial(_mla_attn_kernel, n_ctx=n_ctx, tq=tq, shift=shift)
    tab = pl.BlockSpec((seq, LANES), lambda b, h, i: (0, 0))
    return pl.pallas_call(
        kern,
        grid=(bsz, n_heads // hp, nq),
        in_specs=[pl.BlockSpec((tq, hp * LANES), lambda b, h, i: (b * nq + i, h)),
                  pl.BlockSpec((tq, hp * LANES), lambda b, h, i: (b * nq + i, n_heads // hp + h)),
                  pl.BlockSpec((n_ctx, 2 * hp * LANES), lambda b, h, i: (b, h)),
                  pl.BlockSpec((seq, 2 * hp * LANES), lambda b, h, i: (lat0 + b, h)),
                  pl.BlockSpec((n_ctx, LANES), lambda b, h, i: (b, kpe_col)),
                  pl.BlockSpec((seq, LANES), lambda b, h, i: (lat0 + b, kpe_col)),
                  tab, tab, tab],
        out_specs=pl.BlockSpec((tq, hp * LANES), lambda b, h, i: (b * nq + i, h)),
        out_shape=jax.ShapeDtypeStruct((bsz * seq, n_heads * LANES), BF16),
        scratch_shapes=[pltpu.VMEM((hp, n_ctx + seq, 2 * LANES), BF16), pltpu.VMEM((hp, n_ctx + seq, LANES), BF16)],
        compiler_params=_cparams(("arbitrary", "arbitrary", "arbitrary")),
        name="mla_attn",
    )(q, q, kv, kv, down, down, cos, su, sd)


def _layer_norm(z, g, b):
    zc = z - jnp.mean(z, axis=-1, keepdims=True)
    var = jnp.mean(zc * zc, axis=-1, keepdims=True)
    return zc * lax.rsqrt(var + 1e-5) * g + b


def _resln_kernel(*refs, n_act, alpha):
    x_ref, gate_ref = refs[0], refs[1]
    acts = refs[2:2 + n_act]
    ws = refs[2 + n_act:2 + 2 * n_act]
    lng_ref, lnb_ref, o_ref = refs[2 + 2 * n_act:]
    y = _dot(acts[0][...], ws[0][...])
    for a_ref, w_ref in zip(acts[1:], ws[1:]):
        y = y + _dot(a_ref[...], w_ref[...])
    z = alpha * x_ref[...] + gate_ref[0] * y
    o_ref[...] = _layer_norm(z, lng_ref[...], lnb_ref[...])


def _resln(x, mod, gate_idx, acts, ws, ln_g, ln_b, alpha, geom, row_off_blocks, n_rows, act_off_blocks, tm=512):
    d = x.shape[1]
    row = functools.partial(_mod_row, tm=tm, **geom)
    n_act = len(acts)
    in_specs = [pl.BlockSpec((tm, d), lambda i: (i + row_off_blocks, 0)),
                pl.BlockSpec((1, 1, d), lambda i: (row(i + row_off_blocks), 0, gate_idx))]
    in_specs += [pl.BlockSpec((tm, a.shape[1]), lambda i: (i + act_off_blocks, 0)) for a in acts]
    in_specs += [pl.BlockSpec(w.shape, lambda i: (0, 0)) for w in ws]
    in_specs += [pl.BlockSpec((1, d), lambda i: (0, 0))] * 2
    return pl.pallas_call(
        functools.partial(_resln_kernel, n_act=n_act, alpha=alpha),
        grid=(n_rows // tm,),
        in_specs=in_specs,
        out_specs=pl.BlockSpec((tm, d), lambda i: (i, 0)),
        out_shape=jax.ShapeDtypeStruct((n_rows, d), F32),
        compiler_params=_cparams(("arbitrary",)),
        name="proj_resln",
    )(x, mod, *acts, *ws, ln_g.reshape(1, d), ln_b.reshape(1, d))


def _pack_halves(x):
    n = x.shape[1] // 2
    bits = lax.bitcast_convert_type(x.astype(BF16).astype(F32), jnp.uint32)
    return (bits[:, :n] >> 16) | (bits[:, n:] & jnp.uint32(0xFFFF0000))


def _unpack_halves(w):
    lo = lax.bitcast_convert_type(w << 16, F32)
    hi = lax.bitcast_convert_type(w & jnp.uint32(0xFFFF0000), F32)
    return lo, hi


ROW_TILE = 8


def _store_token_tiles(ref, packed):
    m = packed.shape[0]
    for s in range(ROW_TILE):
        ref[pl.ds(s, m, stride=ROW_TILE), :] = packed[:, s * LANES:(s + 1) * LANES]


def _load_token_tiles(ref, m):
    return jnp.concatenate([ref[pl.ds(s, m, stride=ROW_TILE), :] for s in range(ROW_TILE)], axis=-1)


def _token_tile(ref, tok):
    return ref.at[pl.ds(pl.multiple_of(tok * ROW_TILE, ROW_TILE), ROW_TILE)]


def _pick_first_max(cur, idx, axes, sentinel):
    m = cur
    for ax in axes:
        m = jnp.max(m, axis=ax, keepdims=True)
    first = jnp.where(cur == m, idx, sentinel)
    for ax in axes:
        first = jnp.min(first, axis=ax, keepdims=True)
    return m, first


def _route_kernel(x_ref, sc_ref, sh_ref, rwt_ref, bias_ref, tri_ref, etri_ref, h_ref, g_ref, s_ref, cnt_ref,
                  last_ref, ctab_ref, xs_ref, run_ref, cur_ref, free_ref, slot_vmem, slot_smem, cnt_smem, last_smem,
                  stage_ref, sems, *, n_steps, n_chunks):
    tm = x_ref.shape[0]
    n_members = N_EXPERTS // N_GROUPS
    step = pl.program_id(0)

    @pl.when(step == 0)
    def _():
        run_ref[...] = jnp.zeros_like(run_ref)
        cur_ref[...] = jnp.full(cur_ref.shape, -1.0, F32)
        free_ref[...] = jnp.zeros_like(free_ref)
        ctab_ref[...] = jnp.full(ctab_ref.shape, N_EXPERTS, jnp.int32)

    h = x_ref[...] * (1.0 + sc_ref[0]) + sh_ref[0]
    packed = _pack_halves(h)
    _store_token_tiles(h_ref, packed)
    parity = step % 2
    _store_token_tiles(stage_ref.at[parity], packed)
    logits = lax.dot_general(rwt_ref[...], h, (((1,), (1,)), ((), ())), precision=lax.Precision.HIGHEST,
                             preferred_element_type=F32)
    scores = _sigmoid(logits).reshape(N_GROUPS, n_members, tm)
    biased = scores + bias_ref[...]
    neg = -jnp.inf
    member = lax.broadcasted_iota(jnp.int32, biased.shape, 1).astype(F32)
    group = lax.broadcasted_iota(jnp.int32, biased.shape, 0).astype(F32)
    expert = group * n_members + member

    m1, first = _pick_first_max(biased, member, (1,), float(n_members))
    m2 = jnp.max(jnp.where(member == first, neg, biased), axis=1, keepdims=True)
    gscore = m1 + m2
    gidx = lax.broadcasted_iota(jnp.int32, gscore.shape, 0).astype(F32)
    group_ok = jnp.zeros(gscore.shape, jnp.bool_)
    for _ in range(TOPK_GROUPS):
        _, first = _pick_first_max(gscore, gidx, (0,), float(N_GROUPS))
        pick = gidx == first
        group_ok = group_ok | pick
        gscore = jnp.where(pick, neg, gscore)

    cur = jnp.where(group_ok, biased, neg)
    chosen = jnp.zeros(biased.shape, jnp.bool_)
    top_e, gates = [], []
    for _ in range(TOP_K):
        _, first = _pick_first_max(cur, expert, (0, 1), float(N_EXPERTS))
        pick = expert == first
        chosen = chosen | pick
        cur = jnp.where(pick, neg, cur)
        top_e.append(first)
        gates.append(jnp.sum(jnp.sum(jnp.where(pick, scores, 0.0), axis=0, keepdims=True), axis=1, keepdims=True))
    gsum = gates[0]
    for gk in gates[1:]:
        gsum = gsum + gk
    norm = ROUTE_SCALE / gsum

    chosen2d = jnp.where(chosen, 1.0, 0.0).reshape(N_EXPERTS, tm)
    before = _dot(chosen2d.astype(BF16), tri_ref[...])
    run = run_ref[...]
    rank = run + before
    new_run = run + jnp.sum(chosen2d, axis=1, keepdims=True)

    inv_blk = 1.0 / MOE_BLOCK
    chunks_before = jnp.floor((run + (MOE_BLOCK - 1)) * inv_blk)
    newly = jnp.floor((new_run + (MOE_BLOCK - 1)) * inv_blk) - chunks_before
    ahead = _dot(etri_ref[...], jnp.broadcast_to(newly, (N_EXPERTS, LANES)).astype(BF16))[:, 0:1]
    base = free_ref[...] + ahead
    cur = cur_ref[...]
    j = jnp.floor(rank * inv_blk)
    chunk = jnp.where(j == chunks_before - 1.0, cur, base + (j - chunks_before))
    slot = (chunk * MOE_BLOCK + (rank - j * MOE_BLOCK)).reshape(N_GROUPS, n_members, tm)
    cur = jnp.where(newly > 0.0, base + newly - 1.0, cur)
    cur_ref[...] = cur
    free_ref[...] = free_ref[...] + jnp.sum(newly, axis=0, keepdims=True)
    run_ref[...] = new_run
    cnt_ref[...] = new_run.astype(jnp.int32)
    last_ref[...] = cur.astype(jnp.int32)
    cid = lax.broadcasted_iota(jnp.int32, (N_EXPERTS, ctab_ref.shape[1]), 1).astype(F32)
    eid = lax.broadcasted_iota(jnp.int32, (N_EXPERTS, ctab_ref.shape[1]), 0).astype(F32)
    taken = (cid >= base) & (cid < base + newly)
    ctab_ref[...] = ctab_ref[...] + jnp.sum(jnp.where(taken, eid - N_EXPERTS, 0.0), axis=0,
                                             keepdims=True).astype(jnp.int32)
    for k in range(TOP_K):
        pick = expert == top_e[k]
        sk = jnp.sum(jnp.sum(jnp.where(pick, slot, 0.0), axis=0, keepdims=True), axis=1, keepdims=True)
        g_ref[k:k + 1, :] = (gates[k] * norm).reshape(1, tm)
        s_ref[k:k + 1, :] = sk.reshape(1, tm).astype(jnp.int32)
    slot_vmem[...] = s_ref[...]

    load = pltpu.make_async_copy(slot_vmem, slot_smem, sems.at[2])
    load.start()
    load.wait()

    def issue(t, c):
        for k in range(TOP_K):
            pltpu.make_async_copy(_token_tile(stage_ref.at[parity], t), _token_tile(xs_ref, slot_smem[k, t]),
                                  sems.at[parity]).start(priority=k % 2)
        return c

    lax.fori_loop(0, tm, issue, 0)

    def drain(sem):
        for k in range(TOP_K):
            pltpu.make_async_copy(stage_ref.at[0], xs_ref.at[pl.ds(0, tm * ROW_TILE)], sem).wait()

    @pl.when(step > 0)
    def _():
        drain(sems.at[1 - parity])

    @pl.when(step == n_steps - 1)
    def _():
        drain(sems.at[parity])
        for src, dst in ((cnt_ref, cnt_smem), (last_ref, last_smem)):
            c = pltpu.make_async_copy(src, dst, sems.at[2])
            c.start()
            c.wait()
        zero_ref = stage_ref.at[0]
        zero_ref[...] = jnp.zeros(zero_ref.shape, zero_ref.dtype)
        chunk_rows = MOE_BLOCK * ROW_TILE

        def fill_row(slot):
            return pltpu.make_async_copy(zero_ref.at[pl.ds(0, ROW_TILE)], _token_tile(xs_ref, slot), sems.at[2])

        def fill_chunk(c):
            return pltpu.make_async_copy(zero_ref.at[pl.ds(0, chunk_rows)],
                                         xs_ref.at[pl.ds(pl.multiple_of(c * chunk_rows, chunk_rows), chunk_rows)],
                                         sems.at[2])

        def per_expert(e, n_taken):
            n = cnt_smem[e, 0]
            used = n - (n - 1) // MOE_BLOCK * MOE_BLOCK
            row0 = last_smem[e, 0] * MOE_BLOCK

            @pl.when(n > 0)
            def _():
                lax.fori_loop(used, MOE_BLOCK, lambda r, c: (fill_row(row0 + r).start(), c)[1], 0)
                lax.fori_loop(used, MOE_BLOCK, lambda r, c: (fill_row(row0 + r).wait(), c)[1], 0)

            return n_taken + (n + MOE_BLOCK - 1) // MOE_BLOCK

        n_taken = lax.fori_loop(0, N_EXPERTS, per_expert, 0)
        lax.fori_loop(n_taken, n_chunks, lambda c, x: (fill_chunk(c).start(), x)[1], 0)
        lax.fori_loop(n_taken, n_chunks, lambda c, x: (fill_chunk(c).wait(), x)[1], 0)


def _route(x, mod, sc_idx, sh_idx, router_w, router_bias, geom, n_blocks, tm=512):
    t, d = x.shape
    assert d == 2 * ROW_TILE * LANES
    n_steps = t // tm
    row = functools.partial(_mod_row, tm=tm, **geom)
    tri = (np.arange(tm)[:, None] < np.arange(tm)[None, :]).astype(np.float32)
    etri = (np.arange(N_EXPERTS)[None, :] < np.arange(N_EXPERTS)[:, None]).astype(np.float32)
    ctab_w = -(-n_blocks // LANES) * LANES
    kt = pl.BlockSpec((TOP_K, tm), lambda i: (0, i))
    col = pl.BlockSpec((N_EXPERTS, 1), lambda i: (0, 0))
    return pl.pallas_call(
        functools.partial(_route_kernel, n_steps=n_steps, n_chunks=n_blocks),
        grid=(n_steps,),
        in_specs=[pl.BlockSpec((tm, d), lambda i: (i, 0)),
                  pl.BlockSpec((1, 1, d), lambda i: (row(i), 0, sc_idx)),
                  pl.BlockSpec((1, 1, d), lambda i: (row(i), 0, sh_idx)),
                  pl.BlockSpec((N_EXPERTS, d), lambda i: (0, 0)),
                  pl.BlockSpec((N_GROUPS, N_EXPERTS // N_GROUPS, 1), lambda i: (0, 0, 0)),
                  pl.BlockSpec((tm, tm), lambda i: (0, 0)),
                  pl.BlockSpec((N_EXPERTS, N_EXPERTS), lambda i: (0, 0))],
        out_specs=[pl.BlockSpec((tm * ROW_TILE, LANES), lambda i: (i, 0)), kt, kt, col, col,
                   pl.BlockSpec((1, ctab_w), lambda i: (0, 0)),
                   pl.BlockSpec(memory_space=pl.ANY)],
        out_shape=[jax.ShapeDtypeStruct((t * ROW_TILE, LANES), jnp.uint32),
                   jax.ShapeDtypeStruct((TOP_K, t), F32),
                   jax.ShapeDtypeStruct((TOP_K, t), jnp.int32),
                   jax.ShapeDtypeStruct((N_EXPERTS, 1), jnp.int32),
                   jax.ShapeDtypeStruct((N_EXPERTS, 1), jnp.int32),
                   jax.ShapeDtypeStruct((1, ctab_w), jnp.int32),
                   jax.ShapeDtypeStruct((n_blocks * MOE_BLOCK * ROW_TILE, LANES), jnp.uint32)],
        scratch_shapes=[pltpu.VMEM((N_EXPERTS, 1), F32), pltpu.VMEM((N_EXPERTS, 1), F32), pltpu.VMEM((1, 1), F32),
                        pltpu.VMEM((TOP_K, tm), jnp.int32), pltpu.SMEM((TOP_K, tm), jnp.int32),
                        pltpu.SMEM((N_EXPERTS, 1), jnp.int32), pltpu.SMEM((N_EXPERTS, 1), jnp.int32),
                        pltpu.VMEM((2, tm * ROW_TILE, LANES), jnp.uint32), pltpu.SemaphoreType.DMA((3,))],
        compiler_params=_cparams(("arbitrary",)),
        name="moe_route",
    )(x, mod, mod, router_w.T, router_bias.astype(F32).reshape(N_GROUPS, N_EXPERTS // N_GROUPS, 1),
      jnp.asarray(tri, BF16), jnp.asarray(etri, BF16))


def _chunk_plan(counts, chunk_expert, n_blocks):
    chunk_expert = chunk_expert[0, :n_blocks]
    order = jnp.argsort(chunk_expert, stable=True).astype(jnp.int32)
    block_e = jnp.minimum(chunk_expert[order], N_EXPERTS - 1).astype(jnp.int32)
    n_used = jnp.sum((counts + MOE_BLOCK - 1) // MOE_BLOCK).astype(jnp.int32).reshape(1)
    return order, block_e, n_used


def _dest_tiles(dest, tm):
    k, t = dest.shape
    return dest.reshape(k, t // tm, tm).transpose(1, 0, 2).reshape(t // tm, 1, k * tm)


def _experts_kernel(be_ref, nu_ref, first_ref, slot_ref, next_ref, xb_ref, x_ref, wi_ref, wo_ref, o_ref,
                    gbuf, ubuf, obuf, wg_scr, wu_scr, wo_scr, sems, *, layer):
    blk = pl.program_id(0)
    half = ROW_TILE * LANES
    f = wg_scr.shape[1]

    def weight_copies(e, slot):
        return (pltpu.make_async_copy(wi_ref.at[layer, e, :, pl.ds(0, f)], gbuf.at[slot], sems.at[slot, 0]),
                pltpu.make_async_copy(wi_ref.at[layer, e, :, pl.ds(f, f)], ubuf.at[slot], sems.at[slot, 1]),
                pltpu.make_async_copy(wo_ref.at[layer, e], obuf.at[slot], sems.at[slot, 2]))

    @pl.when(blk == 0)
    def _():
        for c in weight_copies(be_ref[0], 0):
            c.start()

    @pl.when(first_ref[blk] == 1)
    def _():
        slot = slot_ref[blk]
        for c in weight_copies(be_ref[blk], slot):
            c.wait()
        wg_scr[...] = gbuf[slot].astype(BF16)
        wu_scr[...] = ubuf[slot].astype(BF16)
        wo_scr[...] = obuf[slot].astype(BF16)

        @pl.when(next_ref[blk] >= 0)
        def _():
            for c in weight_copies(next_ref[blk], 1 - slot):
                c.start()

    @pl.when(blk < nu_ref[0])
    def _():
        lo, hi = _unpack_halves(_load_token_tiles(x_ref, MOE_BLOCK))
        lo, hi = lo.astype(BF16), hi.astype(BF16)
        gate = _dot(lo, wg_scr[0:half, :]) + _dot(hi, wg_scr[half:, :])
        up = _dot(lo, wu_scr[0:half, :]) + _dot(hi, wu_scr[half:, :])
        a = (gate * _sigmoid(gate) * up).astype(BF16)
        _store_token_tiles(o_ref, _pack_halves(_dot(a, wo_scr[...])))

    @pl.when(blk >= nu_ref[0])
    def _():
        o_ref[...] = jnp.zeros_like(o_ref)


def _experts(xs, block_e, n_used, x_block, n_blocks, w_in, w_out, layer):
    d = w_in.shape[2]
    f = w_out.shape[2]
    assert d == 2 * ROW_TILE * LANES
    first = jnp.concatenate([jnp.ones((1,), jnp.int32), (block_e[1:] != block_e[:-1]).astype(jnp.int32)])
    slot = (jnp.cumsum(first) - 1) % 2
    later = jnp.where(block_e[None, :] > block_e[:, None], block_e[None, :], N_EXPERTS)
    nxt = jnp.min(later, axis=1)
    nxt = jnp.where(nxt >= N_EXPERTS, -1, nxt)
    grid_spec = pltpu.PrefetchScalarGridSpec(
        num_scalar_prefetch=6,
        grid=(n_blocks,),
        in_specs=[pl.BlockSpec((MOE_BLOCK * ROW_TILE, LANES), lambda i, be, nu, fi, sl, nx, xb: (xb[i], 0)),
                  pl.BlockSpec(memory_space=pl.ANY),
                  pl.BlockSpec(memory_space=pl.ANY)],
        out_specs=pl.BlockSpec((MOE_BLOCK * ROW_TILE, LANES), lambda i, be, nu, fi, sl, nx, xb: (xb[i], 0)),
        scratch_shapes=[pltpu.VMEM((2, d, f), F32), pltpu.VMEM((2, d, f), F32), pltpu.VMEM((2, f, d), F32),
                        pltpu.VMEM((d, f), BF16), pltpu.VMEM((d, f), BF16), pltpu.VMEM((f, d), BF16),
                        pltpu.SemaphoreType.DMA((2, 3))],
    )
    return pl.pallas_call(
        functools.partial(_experts_kernel, layer=layer),
        grid_spec=grid_spec,
        out_shape=jax.ShapeDtypeStruct((n_blocks * MOE_BLOCK * ROW_TILE, LANES), jnp.uint32),
        compiler_params=_cparams(("arbitrary",)),
        name="moe_experts",
    )(block_e, n_used, first, slot.astype(jnp.int32), nxt.astype(jnp.int32), x_block, xs, w_in, w_out)


def _ffn_out_kernel(dest_ref, dest_next_ref, x_ref, gate_ref, h_ref, rg_ref, wg_ref, wu_ref, wo_ref, lng_ref,
                    lnb_ref, ys_ref, o_ref, dest_smem, rows_ref, sems, *, alpha, tm, n_steps):
    i = pl.program_id(0)
    slot = i % 2
    half = ROW_TILE * LANES

    def gather(tile_ref, into):
        load = pltpu.make_async_copy(tile_ref.at[0, 0], dest_smem, sems.at[0])
        load.start()
        load.wait()

        def issue(t, c):
            for k in range(TOP_K):
                d = dest_smem[k * tm + t]
                pltpu.make_async_copy(_token_tile(ys_ref, d), _token_tile(rows_ref.at[into, k], t),
                                      sems.at[1 + into]).start(priority=k % 2)
            return c

        lax.fori_loop(0, tm, issue, 0)

    @pl.when(i == 0)
    def _():
        gather(dest_ref, 0)

    @pl.when(i + 1 < n_steps)
    def _():
        gather(dest_next_ref, 1 - slot)

    lo, hi = _unpack_halves(_load_token_tiles(h_ref, tm))
    lo, hi = lo.astype(BF16), hi.astype(BF16)
    gate = _dot(lo, wg_ref[0:half, :]) + _dot(hi, wg_ref[half:, :])
    up = _dot(lo, wu_ref[0:half, :]) + _dot(hi, wu_ref[half:, :])
    a = (gate * _sigmoid(gate) * up).astype(BF16)
    y = _dot(a, wo_ref[...])
    y_lo, y_hi = y[:, :half], y[:, half:]

    rg = rg_ref[...]
    for k in range(TOP_K):
        pltpu.make_async_copy(ys_ref.at[pl.ds(0, tm * ROW_TILE)], rows_ref.at[slot, k], sems.at[1 + slot]).wait()
    for k in range(TOP_K):
        r_lo, r_hi = _unpack_halves(_load_token_tiles(rows_ref.at[slot, k], tm))
        y_lo = y_lo + r_lo * rg[:, k:k + 1]
        y_hi = y_hi + r_hi * rg[:, k:k + 1]
    z = alpha * x_ref[...] + gate_ref[0] * jnp.concatenate([y_lo, y_hi], axis=-1)
    o_ref[...] = _layer_norm(z, lng_ref[...], lnb_ref[...])


def _ffn_out(x, mod, gate_idx, h, ys, dest, route_gate, sh_in, sh_out, ln_g, ln_b, alpha, geom, tm=256):
    t, d = x.shape
    f = sh_out.shape[0]
    assert d == 2 * ROW_TILE * LANES
    n_steps = t // tm
    row = functools.partial(_mod_row, tm=tm, **geom)
    tiles = _dest_tiles(dest, tm)
    return pl.pallas_call(
        functools.partial(_ffn_out_kernel, alpha=alpha, tm=tm, n_steps=n_steps),
        grid=(n_steps,),
        in_specs=[pl.BlockSpec((1, 1, TOP_K * tm), lambda i: (i, 0, 0)),
                  pl.BlockSpec((1, 1, TOP_K * tm), lambda i: (jnp.minimum(i + 1, n_steps - 1), 0, 0)),
                  pl.BlockSpec((tm, d), lambda i: (i, 0)),
                  pl.BlockSpec((1, 1, d), lambda i: (row(i), 0, gate_idx)),
                  pl.BlockSpec((tm * ROW_TILE, LANES), lambda i: (i, 0)),
                  pl.BlockSpec((tm, TOP_K), lambda i: (i, 0)),
                  pl.BlockSpec((d, f), lambda i: (0, 0)),
                  pl.BlockSpec((d, f), lambda i: (0, 1)),
                  pl.BlockSpec((f, d), lambda i: (0, 0)),
                  pl.BlockSpec((1, d), lambda i: (0, 0)),
                  pl.BlockSpec((1, d), lambda i: (0, 0)),
                  pl.BlockSpec(memory_space=pl.ANY)],
        out_specs=pl.BlockSpec((tm, d), lambda i: (i, 0)),
        out_shape=jax.ShapeDtypeStruct((t, d), F32),
        scratch_shapes=[pltpu.SMEM((TOP_K * tm,), jnp.int32),
                        pltpu.VMEM((2, TOP_K, tm * ROW_TILE, LANES), jnp.uint32),
                        pltpu.SemaphoreType.DMA((3,))],
        compiler_params=_cparams(("arbitrary",)),
        name="ffn_out",
    )(tiles, tiles, x, mod, h, route_gate, sh_in, sh_in, sh_out, ln_g.reshape(1, d), ln_b.reshape(1, d), ys)


def _moe_sublayer(x, mod, geom, layer, router_w, router_bias, w_in, w_out, sh_in, sh_out, ln_g, ln_b, alpha):
    n_blocks = x.shape[0] * TOP_K // MOE_BLOCK + N_EXPERTS
    h, gate, dest, counts, last, chunk_expert, xs = _route(x, mod, 4, 3, router_w, router_bias, geom, n_blocks)
    del last
    x_block, block_e, n_used = _chunk_plan(counts.reshape(N_EXPERTS), chunk_expert, n_blocks)
    ys = _experts(xs, block_e, n_used, x_block, n_blocks, w_in, w_out, layer)
    return _ffn_out(x, mod, 5, h, ys, dest, gate.T, sh_in.astype(BF16), sh_out.astype(BF16), ln_g, ln_b, alpha,
                    geom)


def kernel(x, c, ctx, c_ctx, ada_w, ada_b, ln_mix_g, ln_mix_b, ln_ffn_g, ln_ffn_b, ab_w_in, ab_w_out, diff_lambda, diff_subln_g, gqa_q_norm_g, gqa_k_norm_g, mla_w_down, mla_q_norm_g, mla_w_uq, mla_kv_norm_g, mla_w_ukv, mla_w_o, router_w, router_bias, expert_w_in, expert_w_out, shared_w_in, shared_w_out):
    bsz, seq, d = x.shape
    n_ctx = ctx.shape[1]
    depth = ada_w.shape[0]
    assert depth == 2 and seq % GRID_W == 0 and bsz + 1 <= MOD_ROWS
    alpha = (2 * depth) ** 0.25
    n_ctx_rows = bsz * n_ctx
    n_lat_rows = bsz * seq
    geom_all = dict(n_ctx_rows=n_ctx_rows, seq=seq, ctx_row=bsz)
    geom_lat = dict(n_ctx_rows=0, seq=seq, ctx_row=bsz)

    tables64 = _rope_tables(seq // GRID_W, 64)
    tables128 = _rope_tables(seq // GRID_W, HEAD_DIM)

    cvec = jnp.zeros((MOD_ROWS, d), F32).at[:bsz].set(c).at[bsz].set(c_ctx)
    mods = _ada_mod(cvec, ada_w, ada_b)
    mods = mods.reshape(depth, MOD_ROWS, 1, 6 * d)

    xt = jnp.concatenate([ctx.reshape(n_ctx_rows, d), x.reshape(n_lat_rows, d)], axis=0)

    mod = mods[0]
    n_diff = d // (2 * HEAD_DIM)
    n_gqa = d // (2 * HEAD_DIM)
    n_gqa_kv = n_gqa // 4
    proj = _modproj(xt, mod, 1, 0, ab_w_in[0].astype(BF16), BF16, geom_all)
    lam_init = 0.8 - 0.6 * math.exp(-0.3 * 0)
    oa = _diff_attn(proj, tables64, diff_lambda[0], diff_subln_g[0], lam_init, bsz, n_ctx, seq, n_diff)
    ob = _gqa_attn(proj, tables128, gqa_q_norm_g[0], gqa_k_norm_g[0], 3 * n_diff, bsz, n_ctx, seq, n_gqa, n_gqa_kv)
    w_out = ab_w_out[0].astype(BF16)
    wa, wb = w_out[:n_diff * HEAD_DIM], w_out[n_diff * HEAD_DIM:]
    t_all = n_ctx_rows + n_lat_rows
    xt = _resln(xt, mod, 2, [oa, ob], [wa, wb], ln_mix_g[0], ln_mix_b[0], alpha, geom_all, 0, t_all, 0)
    xt = _moe_sublayer(xt, mod, geom_all, 0, router_w[0], router_bias[0], expert_w_in, expert_w_out,
                       shared_w_in[0], shared_w_out[0], ln_ffn_g[0], ln_ffn_b[0], alpha)

    mod = mods[1]
    n_mla = d // HEAD_DIM
    qk = MLA_NOPE_DIM + MLA_ROPE_DIM
    w_down = jnp.pad(mla_w_down[0], ((0, 0), (0, LANES - MLA_ROPE_DIM))).astype(BF16)
    down = _modproj(xt, mod, 1, 0, w_down, F32, geom_all)
    w_uq = mla_w_uq[0].reshape(MLA_Q_RANK, n_mla, qk)
    w_uq_rope = jnp.pad(w_uq[:, :, MLA_NOPE_DIM:], ((0, 0), (0, 0), (0, LANES - MLA_ROPE_DIM)))
    w_uq = jnp.concatenate([w_uq[:, :, :MLA_NOPE_DIM].reshape(MLA_Q_RANK, -1),
                            w_uq_rope.reshape(MLA_Q_RANK, -1)], axis=1).astype(BF16)
    lat_blocks = n_ctx_rows // 1024
    q = _normproj(down, 0, mla_q_norm_g[0], w_uq, lat_blocks, n_lat_rows)
    kv = _normproj(down, 1, mla_kv_norm_g[0], mla_w_ukv[0].astype(BF16), 0, t_all)
    kpe_col = (MLA_Q_RANK + MLA_KV_RANK) // LANES
    o = _mla_attn(q, kv, down, tables64, bsz, n_ctx, seq, n_mla, kpe_col)
    xl = _resln(xt, mod, 2, [o], [mla_w_o[0].astype(BF16)], ln_mix_g[1], ln_mix_b[1], alpha, geom_all,
                n_ctx_rows // 512, n_lat_rows, 0)
    xl = _moe_sublayer(xl, mod, geom_lat, 1, router_w[1], router_bias[1], expert_w_in, expert_w_out,
                       shared_w_in[1], shared_w_out[1], ln_ffn_g[1], ln_ffn_b[1], alpha)
    return xl.reshape(bsz, seq, d)
```

```python
import functools
import math

import numpy as np
import jax
import jax.numpy as jnp
from jax import lax
from jax.experimental import pallas as pl
from jax.experimental.pallas import tpu as pltpu

F32 = jnp.float32
BF16 = jnp.bfloat16

GRID_W = 64
ROPE_THETA = 10000.0
HEAD_DIM = 128
DIFF_QK_DIM = 64
N_EXPERTS = 64
EXPERT_DIM = 512
TOP_K = 8
N_GROUPS = 8
TOPK_GROUPS = 4
ROUTE_SCALE = 2.5
MLA_Q_RANK = 512
MLA_KV_RANK = 512
MLA_NOPE_DIM = 128
MLA_ROPE_DIM = 64
MLA_V_DIM = 128
LANES = 128

MOD_ROWS = 16
VMEM_LIMIT = 56 * 1024 * 1024
MOE_BLOCK = 512


def _cparams(sem):
    return pltpu.CompilerParams(dimension_semantics=sem, vmem_limit_bytes=VMEM_LIMIT)


def _dot(a, b):
    return jnp.dot(a, b, preferred_element_type=F32)


def _dot_nt(a, b):
    return lax.dot_general(a, b, (((1,), (1,)), ((), ())), preferred_element_type=F32)


def _dot_hi(a, b):
    return lax.dot_general(a, b, (((1,), (0,)), ((), ())), precision=lax.Precision.HIGHEST,
                           preferred_element_type=F32)


def _sigmoid(x):
    return 1.0 / (1.0 + jnp.exp(-x))


def _rope(x, cos, sin_up, sin_dn, shift):
    return (x * cos + pltpu.roll(x, LANES - shift, 1) * sin_up + pltpu.roll(x, shift, 1) * sin_dn)


def _rope_tables(rows, rot_dim):
    t = jnp.arange(rows * GRID_W)
    row = (t // GRID_W).astype(F32)
    col = (t % GRID_W).astype(F32)
    axis_dim = rot_dim // 2
    quarter = rot_dim // 4
    inv_freq = 1.0 / (ROPE_THETA ** (jnp.arange(0, axis_dim, 2, dtype=F32) / axis_dim))
    lane = np.arange(LANES) % rot_dim
    is_col = lane >= axis_dim
    within = lane % axis_dim
    first = within < quarter
    freq = within % quarter
    ang = jnp.where(jnp.asarray(is_col)[None, :], col[:, None], row[:, None]) * inv_freq[freq][None, :]
    cos = jnp.cos(ang).astype(F32)
    sin = jnp.sin(ang).astype(F32)
    first = jnp.asarray(first)[None, :]
    sin_up = jnp.where(first, -sin, 0.0)
    sin_dn = jnp.where(first, 0.0, sin)
    return cos, sin_up, sin_dn, quarter


LOG2E = 1.4426950408889634
HEADS_PER_STEP = 4


def _exp2_rows(s):
    e = jnp.exp2(s - jnp.max(s, axis=-1, keepdims=True))
    return e, 1.0 / jnp.sum(e, axis=-1, keepdims=True)


def _rms(x, g, eps):
    return x * lax.rsqrt(jnp.mean(x * x, axis=-1, keepdims=True) + eps) * g


def _ada_kernel(c_ref, w_ref, b_ref, o_ref):
    c = c_ref[...]
    act = c * _sigmoid(c)
    o_ref[0] = _dot_hi(act, w_ref[0]) + b_ref[0]


def _ada_mod(cvec, ada_w, ada_b):
    depth, d, n = ada_w.shape
    tn = 1536
    return pl.pallas_call(
        _ada_kernel,
        grid=(depth, n // tn),
        in_specs=[pl.BlockSpec((MOD_ROWS, d), lambda l, j: (0, 0)),
                  pl.BlockSpec((1, d, tn), lambda l, j: (l, 0, j)),
                  pl.BlockSpec((1, 1, tn), lambda l, j: (l, 0, j))],
        out_specs=pl.BlockSpec((1, MOD_ROWS, tn), lambda l, j: (l, 0, j)),
        out_shape=jax.ShapeDtypeStruct((depth, MOD_ROWS, n), F32),
        compiler_params=_cparams(("arbitrary", "arbitrary")),
        name="ada_mod",
    )(cvec, ada_w, ada_b.reshape(depth, 1, n))


def _mod_row(i, tm, n_ctx_rows, seq, ctx_row):
    r0 = i * tm
    return jnp.where(r0 < n_ctx_rows, ctx_row, (r0 - n_ctx_rows) // seq)


def _modproj_kernel(x_ref, sc_ref, sh_ref, w_ref, o_ref, h_ref):
    @pl.when(pl.program_id(1) == 0)
    def _():
        h_ref[...] = (x_ref[...] * (1.0 + sc_ref[0]) + sh_ref[0]).astype(h_ref.dtype)

    o_ref[...] = _dot(h_ref[...], w_ref[...]).astype(o_ref.dtype)


def _modproj(x, mod, sc_idx, sh_idx, w, out_dtype, geom, tm=1024, tn=512):
    t, d = x.shape
    n = w.shape[1]
    tn = tn if n % tn == 0 else n
    row = functools.partial(_mod_row, tm=tm, **geom)
    return pl.pallas_call(
        _modproj_kernel,
        grid=(t // tm, n // tn),
        in_specs=[pl.BlockSpec((tm, d), lambda i, j: (i, 0)),
                  pl.BlockSpec((1, 1, d), lambda i, j: (row(i), 0, sc_idx)),
                  pl.BlockSpec((1, 1, d), lambda i, j: (row(i), 0, sh_idx)),
                  pl.BlockSpec((d, tn), lambda i, j: (0, j))],
        out_specs=pl.BlockSpec((tm, tn), lambda i, j: (i, j)),
        out_shape=jax.ShapeDtypeStruct((t, n), out_dtype),
        scratch_shapes=[pltpu.VMEM((tm, d), BF16)],
        compiler_params=_cparams(("arbitrary", "arbitrary")),
        name="modproj",
    )(x, mod, mod, w)


def _normproj_kernel(x_ref, g_ref, w_ref, o_ref, h_ref):
    @pl.when(pl.program_id(1) == 0)
    def _():
        h_ref[...] = _rms(x_ref[...], g_ref[...], 1e-6).astype(h_ref.dtype)

    o_ref[...] = _dot(h_ref[...], w_ref[...]).astype(o_ref.dtype)


def _normproj(x, col_block, g, w, row_off_blocks, n_rows, tm=1024, tn=1024):
    k = w.shape[0]
    n = w.shape[1]
    return pl.pallas_call(
        _normproj_kernel,
        grid=(n_rows // tm, n // tn),
        in_specs=[pl.BlockSpec((tm, k), lambda i, j: (i + row_off_blocks, col_block)),
                  pl.BlockSpec((1, k), lambda i, j: (0, 0)),
                  pl.BlockSpec((k, tn), lambda i, j: (0, j))],
        out_specs=pl.BlockSpec((tm, tn), lambda i, j: (i, j)),
        out_shape=jax.ShapeDtypeStruct((n_rows, n), BF16),
        scratch_shapes=[pltpu.VMEM((tm, k), BF16)],
        compiler_params=_cparams(("arbitrary", "arbitrary")),
        name="normproj",
    )(x, g.reshape(1, k), w)


def _diff_lambda(lam_ref, lam_init):
    lp = lam_ref[...]
    s01 = jnp.sum(lp[0:1] * lp[1:2], axis=-1, keepdims=True)
    s23 = jnp.sum(lp[2:3] * lp[3:4], axis=-1, keepdims=True)
    return jnp.exp(s01) - jnp.exp(s23) + lam_init


def _diff_logits(q, k):
    lane = lax.broadcasted_iota(jnp.int32, q.shape, 1)
    qs = q * (DIFF_QK_DIM ** -0.5 * LOG2E)
    q1 = jnp.where(lane < DIFF_QK_DIM, qs, 0.0).astype(BF16)
    q2 = jnp.where(lane >= DIFF_QK_DIM, qs, 0.0).astype(BF16)
    return _dot_nt(q1, k), _dot_nt(q2, k)


def _diff_combine(s1, s2, v, lam, g, lam_init):
    e1, r1 = _exp2_rows(s1)
    e2, r2 = _exp2_rows(s2)
    a = (e1 - e2 * (lam * r2 / r1)).astype(BF16)
    return _rms(_dot(a, v) * r1, g, 1e-5) * (1.0 - lam_init)


def _head_cols(j):
    return slice(j * LANES, (j + 1) * LANES)


def _diff_attn_kernel(q_ref, kc_ref, kl_ref, vc_ref, vl_ref, cos_ref, su_ref, sd_ref, lam_ref, g_ref,
                      o_ref, k_scr, v_scr, *, n_ctx, tq, shift, lam_init):
    i = pl.program_id(2)
    heads = range(HEADS_PER_STEP)

    @pl.when(i == 0)
    def _prep():
        for j in heads:
            k_scr[j, 0:n_ctx, :] = kc_ref[:, _head_cols(j)]
            kl = kl_ref[:, _head_cols(j)].astype(F32)
            k_scr[j, n_ctx:, :] = _rope(kl, cos_ref[...], su_ref[...], sd_ref[...], shift).astype(BF16)
            v_scr[j, 0:n_ctx, :] = vc_ref[:, _head_cols(j)]
            v_scr[j, n_ctx:, :] = vl_ref[:, _head_cols(j)]

    lam = _diff_lambda(lam_ref, lam_init)
    g = g_ref[...]

    def attend(qs, n_keys):
        logits = [_diff_logits(qs[j], k_scr[j, 0:n_keys, :]) for j in heads]
        for j in heads:
            o = _diff_combine(*logits[j], v_scr[j, 0:n_keys, :], lam, g, lam_init)
            o_ref[:, _head_cols(j)] = o.astype(o_ref.dtype)

    @pl.when(i == 0)
    def _ctx_queries():
        attend([q_ref[:, _head_cols(j)].astype(F32) for j in heads], n_ctx)

    @pl.when(i > 0)
    def _lat_queries():
        r0 = pl.multiple_of((i - 1) * tq, tq)
        cos, su, sd = cos_ref[pl.ds(r0, tq), :], su_ref[pl.ds(r0, tq), :], sd_ref[pl.ds(r0, tq), :]
        attend([_rope(q_ref[:, _head_cols(j)].astype(F32), cos, su, sd, shift) for j in heads],
               k_scr.shape[1])


def _q_block(b, i, bsz, n_ctx, seq, tq):
    return jnp.where(i == 0, b * (n_ctx // tq), (bsz * n_ctx + b * seq) // tq + i - 1)


def _diff_attn(proj, tables, lam_params, subln_g, lam_init, bsz, n_ctx, seq, n_heads, tq=256):
    t = proj.shape[0]
    cos, su, sd, shift = tables
    hp = HEADS_PER_STEP
    wide = hp * LANES
    n_pairs = n_heads // hp
    qb = functools.partial(_q_block, bsz=bsz, n_ctx=n_ctx, seq=seq, tq=tq)
    lat0 = bsz * n_ctx // seq
    kern = functools.partial(_diff_attn_kernel, n_ctx=n_ctx, tq=tq, shift=shift, lam_init=lam_init)
    tab = pl.BlockSpec((seq, LANES), lambda b, h, i: (0, 0))
    return pl.pallas_call(
        kern,
        grid=(bsz, n_pairs, 1 + seq // tq),
        in_specs=[pl.BlockSpec((tq, wide), lambda b, h, i: (qb(b, i), h)),
                  pl.BlockSpec((n_ctx, wide), lambda b, h, i: (b, n_pairs + h)),
                  pl.BlockSpec((seq, wide), lambda b, h, i: (lat0 + b, n_pairs + h)),
                  pl.BlockSpec((n_ctx, wide), lambda b, h, i: (b, 2 * n_pairs + h)),
                  pl.BlockSpec((seq, wide), lambda b, h, i: (lat0 + b, 2 * n_pairs + h)),
                  tab, tab, tab,
                  pl.BlockSpec((4, DIFF_QK_DIM), lambda b, h, i: (0, 0)),
                  pl.BlockSpec((1, LANES), lambda b, h, i: (0, 0))],
        out_specs=pl.BlockSpec((tq, wide), lambda b, h, i: (qb(b, i), h)),
        out_shape=jax.ShapeDtypeStruct((t, n_heads * LANES), BF16),
        scratch_shapes=[pltpu.VMEM((hp, n_ctx + seq, LANES), BF16), pltpu.VMEM((hp, n_ctx + seq, LANES), BF16)],
        compiler_params=_cparams(("arbitrary", "arbitrary", "arbitrary")),
        name="diff_attn",
    )(proj, proj, proj, proj, proj, cos, su, sd, lam_params, subln_g.reshape(1, LANES))


def _softmax_pv(s, v):
    e, r = _exp2_rows(s)
    return _dot(e.astype(BF16), v) * r


def _gqa_attn_kernel(q_ref, kc_ref, kl_ref, vc_ref, vl_ref, cos_ref, su_ref, sd_ref, qg_ref, kg_ref,
                     o_ref, k_scr, v_scr, *, n_ctx, tq, shift):
    g = pl.program_id(2)
    i = pl.program_id(3)
    heads = range(HEADS_PER_STEP)

    @pl.when((g == 0) & (i == 0))
    def _prep():
        kg = kg_ref[...]
        k_scr[0:n_ctx, :] = _rms(kc_ref[...].astype(F32), kg, 1e-6).astype(BF16)
        kl = _rms(kl_ref[...].astype(F32), kg, 1e-6)
        k_scr[n_ctx:, :] = _rope(kl, cos_ref[...], su_ref[...], sd_ref[...], shift).astype(BF16)
        v_scr[0:n_ctx, :] = vc_ref[...]
        v_scr[n_ctx:, :] = vl_ref[...]

    scale = HEAD_DIM ** -0.5 * LOG2E

    def attend(qs, n_keys):
        logits = [_dot_nt((qs[j] * scale).astype(BF16), k_scr[0:n_keys, :]) for j in heads]
        for j in heads:
            o_ref[:, _head_cols(j)] = _softmax_pv(logits[j], v_scr[0:n_keys, :]).astype(o_ref.dtype)

    @pl.when(i == 0)
    def _ctx_queries():
        attend([_rms(q_ref[:, _head_cols(j)].astype(F32), qg_ref[...], 1e-6) for j in heads], n_ctx)

    @pl.when(i > 0)
    def _lat_queries():
        r0 = pl.multiple_of((i - 1) * tq, tq)
        cos, su, sd = cos_ref[pl.ds(r0, tq), :], su_ref[pl.ds(r0, tq), :], sd_ref[pl.ds(r0, tq), :]
        attend([_rope(_rms(q_ref[:, _head_cols(j)].astype(F32), qg_ref[...], 1e-6), cos, su, sd, shift)
                for j in heads], k_scr.shape[0])


def _gqa_attn(proj, tables, q_norm_g, k_norm_g, col0, bsz, n_ctx, seq, n_heads, n_kv, tq=256):
    t = proj.shape[0]
    cos, su, sd, shift = tables
    hp = HEADS_PER_STEP
    wide = hp * LANES
    pairs = n_heads // n_kv // hp
    assert col0 % hp == 0
    qb = functools.partial(_q_block, bsz=bsz, n_ctx=n_ctx, seq=seq, tq=tq)
    lat0 = bsz * n_ctx // seq
    kcol = col0 + n_heads
    vcol = kcol + n_kv
    kern = functools.partial(_gqa_attn_kernel, n_ctx=n_ctx, tq=tq, shift=shift)
    tab = pl.BlockSpec((seq, LANES), lambda b, h, g, i: (0, 0))
    vec = pl.BlockSpec((1, LANES), lambda b, h, g, i: (0, 0))
    return pl.pallas_call(
        kern,
        grid=(bsz, n_kv, pairs, 1 + seq // tq),
        in_specs=[pl.BlockSpec((tq, wide), lambda b, h, g, i: (qb(b, i), col0 // hp + h * pairs + g)),
                  pl.BlockSpec((n_ctx, LANES), lambda b, h, g, i: (b, kcol + h)),
                  pl.BlockSpec((seq, LANES), lambda b, h, g, i: (lat0 + b, kcol + h)),
                  pl.BlockSpec((n_ctx, LANES), lambda b, h, g, i: (b, vcol + h)),
                  pl.BlockSpec((seq, LANES), lambda b, h, g, i: (lat0 + b, vcol + h)),
                  tab, tab, tab, vec, vec],
        out_specs=pl.BlockSpec((tq, wide), lambda b, h, g, i: (qb(b, i), h * pairs + g)),
        out_shape=jax.ShapeDtypeStruct((t, n_heads * LANES), BF16),
        scratch_shapes=[pltpu.VMEM((n_ctx + seq, LANES), BF16), pltpu.VMEM((n_ctx + seq, LANES), BF16)],
        compiler_params=_cparams(("arbitrary", "arbitrary", "arbitrary", "arbitrary")),
        name="gqa_attn",
    )(proj, proj, proj, proj, proj, cos, su, sd, q_norm_g.reshape(1, LANES), k_norm_g.reshape(1, LANES))


def _mla_attn_kernel(qn_ref, qp_ref, kvc_ref, kvl_ref, kpc_ref, kpl_ref, cos_ref, su_ref, sd_ref,
                     o_ref, k_scr, v_scr, *, n_ctx, tq, shift):
    i = pl.program_id(2)

    @pl.when(i == 0)
    def _prep():
        kpc = kpc_ref[...].astype(BF16)
        kpl = _rope(kpl_ref[...], cos_ref[...], su_ref[...], sd_ref[...], shift).astype(BF16)
        for j in range(HEADS_PER_STEP):
            c0 = 2 * j * LANES
            k_scr[j, 0:n_ctx, 0:LANES] = kvc_ref[:, c0:c0 + LANES]
            k_scr[j, n_ctx:, 0:LANES] = kvl_ref[:, c0:c0 + LANES]
            k_scr[j, 0:n_ctx, LANES:] = kpc
            k_scr[j, n_ctx:, LANES:] = kpl
            v_scr[j, 0:n_ctx, :] = kvc_ref[:, c0 + LANES:c0 + 2 * LANES]
            v_scr[j, n_ctx:, :] = kvl_ref[:, c0 + LANES:c0 + 2 * LANES]

    scale = (MLA_NOPE_DIM + MLA_ROPE_DIM) ** -0.5 * LOG2E
    r0 = pl.multiple_of(i * tq, tq)
    cos, su, sd = cos_ref[pl.ds(r0, tq), :], su_ref[pl.ds(r0, tq), :], sd_ref[pl.ds(r0, tq), :]
    logits = []
    for j in range(HEADS_PER_STEP):
        qp = _rope(qp_ref[:, _head_cols(j)].astype(F32), cos, su, sd, shift)
        q = jnp.concatenate([(qn_ref[:, _head_cols(j)].astype(F32) * scale).astype(BF16), (qp * scale).astype(BF16)],
                            axis=-1)
        logits.append(_dot_nt(q, k_scr[j]))
    for j in range(HEADS_PER_STEP):
        o_ref[:, _head_cols(j)] = _softmax_pv(logits[j], v_scr[j]).astype(o_ref.dtype)


def _mla_attn(q, kv, down, tables, bsz, n_ctx, seq, n_heads, kpe_col, tq=256):
    cos, su, sd, shift = tables
    lat0 = bsz * n_ctx // seq
    nq = seq // tq
    hp = HEADS_PER_STEP
    kern = functools.partial(_mla_attn_kernel, n_ctx=n_ctx, tq=tq, shift=shift)
    tab = pl.BlockSpec((seq, LANES), lambda b, h, i: (0, 0))
    return pl.pallas_call(
        kern,
        grid=(bsz, n_heads // hp, nq),
        in_specs=[pl.BlockSpec((tq, hp * LANES), lambda b, h, i: (b * nq + i, h)),
                  pl.BlockSpec((tq, hp * LANES), lambda b, h, i: (b * nq + i, n_heads // hp + h)),
                  pl.BlockSpec((n_ctx, 2 * hp * LANES), lambda b, h, i: (b, h)),
                  pl.BlockSpec((seq, 2 * hp * LANES), lambda b, h, i: (lat0 + b, h)),
                  pl.BlockSpec((n_ctx, LANES), lambda b, h, i: (b, kpe_col)),
                  pl.BlockSpec((seq, LANES), lambda b, h, i: (lat0 + b, kpe_col)),
                  tab, tab, tab],
        out_specs=pl.BlockSpec((tq, hp * LANES), lambda b, h, i: (b * nq + i, h)),
        out_shape=jax.ShapeDtypeStruct((bsz * seq, n_heads * LANES), BF16),
        scratch_shapes=[pltpu.VMEM((hp, n_ctx + seq, 2 * LANES), BF16), pltpu.VMEM((hp, n_ctx + seq, LANES), BF16)],
        compiler_params=_cparams(("arbitrary", "arbitrary", "arbitrary")),
        name="mla_attn",
    )(q, q, kv, kv, down, down, cos, su, sd)


def _layer_norm(z, g, b):
    zc = z - jnp.mean(z, axis=-1, keepdims=True)
    var = jnp.mean(zc * zc, axis=-1, keepdims=True)
    return zc * lax.rsqrt(var + 1e-5) * g + b


def _resln_kernel(*refs, n_act, alpha):
    x_ref, gate_ref = refs[0], refs[1]
    acts = refs[2:2 + n_act]
    ws = refs[2 + n_act:2 + 2 * n_act]
    lng_ref, lnb_ref, o_ref = refs[2 + 2 * n_act:]
    y = _dot(acts[0][...], ws[0][...])
    for a_ref, w_ref in zip(acts[1:], ws[1:]):
        y = y + _dot(a_ref[...], w_ref[...])
    z = alpha * x_ref[...] + gate_ref[0] * y
    o_ref[...] = _layer_norm(z, lng_ref[...], lnb_ref[...])


def _resln(x, mod, gate_idx, acts, ws, ln_g, ln_b, alpha, geom, row_off_blocks, n_rows, act_off_blocks, tm=512):
    d = x.shape[1]
    row = functools.partial(_mod_row, tm=tm, **geom)
    n_act = len(acts)
    in_specs = [pl.BlockSpec((tm, d), lambda i: (i + row_off_blocks, 0)),
                pl.BlockSpec((1, 1, d), lambda i: (row(i + row_off_blocks), 0, gate_idx))]
    in_specs += [pl.BlockSpec((tm, a.shape[1]), lambda i: (i + act_off_blocks, 0)) for a in acts]
    in_specs += [pl.BlockSpec(w.shape, lambda i: (0, 0)) for w in ws]
    in_specs += [pl.BlockSpec((1, d), lambda i: (0, 0))] * 2
    return pl.pallas_call(
        functools.partial(_resln_kernel, n_act=n_act, alpha=alpha),
        grid=(n_rows // tm,),
        in_specs=in_specs,
        out_specs=pl.BlockSpec((tm, d), lambda i: (i, 0)),
        out_shape=jax.ShapeDtypeStruct((n_rows, d), F32),
        compiler_params=_cparams(("arbitrary",)),
        name="proj_resln",
    )(x, mod, *acts, *ws, ln_g.reshape(1, d), ln_b.reshape(1, d))


def _pack_halves(x):
    n = x.shape[1] // 2
    bits = lax.bitcast_convert_type(x.astype(BF16).astype(F32), jnp.uint32)
    return (bits[:, :n] >> 16) | (bits[:, n:] & jnp.uint32(0xFFFF0000))


def _unpack_halves(w):
    lo = lax.bitcast_convert_type(w << 16, F32)
    hi = lax.bitcast_convert_type(w & jnp.uint32(0xFFFF0000), F32)
    return lo, hi


ROW_TILE = 8


def _store_token_tiles(ref, packed):
    m = packed.shape[0]
    for s in range(ROW_TILE):
        ref[pl.ds(s, m, stride=ROW_TILE), :] = packed[:, s * LANES:(s + 1) * LANES]


def _load_token_tiles(ref, m):
    return jnp.concatenate([ref[pl.ds(s, m, stride=ROW_TILE), :] for s in range(ROW_TILE)], axis=-1)


def _token_tile(ref, tok):
    return ref.at[pl.ds(pl.multiple_of(tok * ROW_TILE, ROW_TILE), ROW_TILE)]


def _pick_first_max(cur, idx, axes, sentinel):
    m = cur
    for ax in axes:
        m = jnp.max(m, axis=ax, keepdims=True)
    first = jnp.where(cur == m, idx, sentinel)
    for ax in axes:
        first = jnp.min(first, axis=ax, keepdims=True)
    return m, first


def _route_kernel(x_ref, sc_ref, sh_ref, rwt_ref, bias_ref, tri_ref, h_ref, e_ref, g_ref, r_ref, cnt_ref, run_ref):
    tm = x_ref.shape[0]
    n_members = N_EXPERTS // N_GROUPS

    @pl.when(pl.program_id(0) == 0)
    def _():
        run_ref[...] = jnp.zeros_like(run_ref)

    h = x_ref[...] * (1.0 + sc_ref[0]) + sh_ref[0]
    _store_token_tiles(h_ref, _pack_halves(h))
    logits = lax.dot_general(rwt_ref[...], h, (((1,), (1,)), ((), ())), precision=lax.Precision.HIGHEST,
                             preferred_element_type=F32)
    scores = _sigmoid(logits).reshape(N_GROUPS, n_members, tm)
    biased = scores + bias_ref[...]
    neg = -jnp.inf
    member = lax.broadcasted_iota(jnp.int32, biased.shape, 1).astype(F32)
    group = lax.broadcasted_iota(jnp.int32, biased.shape, 0).astype(F32)
    expert = group * n_members + member

    m1, first = _pick_first_max(biased, member, (1,), float(n_members))
    m2 = jnp.max(jnp.where(member == first, neg, biased), axis=1, keepdims=True)
    gscore = m1 + m2
    gidx = lax.broadcasted_iota(jnp.int32, gscore.shape, 0).astype(F32)
    group_ok = jnp.zeros(gscore.shape, jnp.bool_)
    for _ in range(TOPK_GROUPS):
        _, first = _pick_first_max(gscore, gidx, (0,), float(N_GROUPS))
        pick = gidx == first
        group_ok = group_ok | pick
        gscore = jnp.where(pick, neg, gscore)

    cur = jnp.where(group_ok, biased, neg)
    chosen = jnp.zeros(biased.shape, jnp.bool_)
    top_e, gates = [], []
    for _ in range(TOP_K):
        _, first = _pick_first_max(cur, expert, (0, 1), float(N_EXPERTS))
        pick = expert == first
        chosen = chosen | pick
        cur = jnp.where(pick, neg, cur)
        top_e.append(first)
        gates.append(jnp.sum(jnp.sum(jnp.where(pick, scores, 0.0), axis=0, keepdims=True), axis=1, keepdims=True))
    gsum = gates[0]
    for gk in gates[1:]:
        gsum = gsum + gk
    norm = ROUTE_SCALE / gsum

    chosen2d = jnp.where(chosen, 1.0, 0.0).reshape(N_EXPERTS, tm)
    before = _dot(chosen2d.astype(BF16), tri_ref[...])
    rank = (run_ref[...] + before).reshape(N_GROUPS, n_members, tm)
    run_ref[...] = run_ref[...] + jnp.sum(chosen2d, axis=1, keepdims=True)
    cnt_ref[...] = run_ref[...].astype(jnp.int32)
    for k in range(TOP_K):
        pick = expert == top_e[k]
        rk = jnp.sum(jnp.sum(jnp.where(pick, rank, 0.0), axis=0, keepdims=True), axis=1, keepdims=True)
        e_ref[k:k + 1, :] = top_e[k].reshape(1, tm).astype(jnp.int32)
        g_ref[k:k + 1, :] = (gates[k] * norm).reshape(1, tm)
        r_ref[k:k + 1, :] = rk.reshape(1, tm).astype(jnp.int32)


def _route(x, mod, sc_idx, sh_idx, router_w, router_bias, geom, tm=512):
    t, d = x.shape
    row = functools.partial(_mod_row, tm=tm, **geom)
    tri = (np.arange(tm)[:, None] < np.arange(tm)[None, :]).astype(np.float32)
    kt = pl.BlockSpec((TOP_K, tm), lambda i: (0, i))
    return pl.pallas_call(
        _route_kernel,
        grid=(t // tm,),
        in_specs=[pl.BlockSpec((tm, d), lambda i: (i, 0)),
                  pl.BlockSpec((1, 1, d), lambda i: (row(i), 0, sc_idx)),
                  pl.BlockSpec((1, 1, d), lambda i: (row(i), 0, sh_idx)),
                  pl.BlockSpec((N_EXPERTS, d), lambda i: (0, 0)),
                  pl.BlockSpec((N_GROUPS, N_EXPERTS // N_GROUPS, 1), lambda i: (0, 0, 0)),
                  pl.BlockSpec((tm, tm), lambda i: (0, 0))],
        out_specs=[pl.BlockSpec((tm * ROW_TILE, LANES), lambda i: (i, 0)), kt, kt, kt,
                   pl.BlockSpec((N_EXPERTS, 1), lambda i: (0, 0))],
        out_shape=[jax.ShapeDtypeStruct((t * ROW_TILE, LANES), jnp.uint32),
                   jax.ShapeDtypeStruct((TOP_K, t), jnp.int32),
                   jax.ShapeDtypeStruct((TOP_K, t), F32),
                   jax.ShapeDtypeStruct((TOP_K, t), jnp.int32),
                   jax.ShapeDtypeStruct((N_EXPERTS, 1), jnp.int32)],
        scratch_shapes=[pltpu.VMEM((N_EXPERTS, 1), F32)],
        compiler_params=_cparams(("arbitrary",)),
        name="moe_route",
    )(x, mod, mod, router_w.T, router_bias.astype(F32).reshape(N_GROUPS, N_EXPERTS // N_GROUPS, 1),
      jnp.asarray(tri, BF16))


def _dispatch_plan(top_e, rank, counts, n_blocks):
    counts = counts.reshape(N_EXPERTS)
    padded = (counts + MOE_BLOCK - 1) // MOE_BLOCK * MOE_BLOCK
    pad_ends = jnp.cumsum(padded)
    pad_starts = pad_ends - padded
    onehot = top_e[:, :, None] == jnp.arange(N_EXPERTS, dtype=jnp.int32)
    dest = jnp.sum(jnp.where(onehot, pad_starts, 0), axis=-1) + rank
    block_start = jnp.arange(n_blocks, dtype=jnp.int32) * MOE_BLOCK
    block_e = jnp.minimum(jnp.sum(pad_ends[None, :] <= block_start[:, None], axis=1), N_EXPERTS - 1)
    n_used = (pad_ends[-1] // MOE_BLOCK).astype(jnp.int32).reshape(1)
    return dest, pad_ends.astype(jnp.int32), block_e.astype(jnp.int32), n_used


def _dest_tiles(dest, tm):
    k, t = dest.shape
    return dest.reshape(k, t // tm, tm).transpose(1, 0, 2).reshape(t // tm, 1, k * tm)


def _dispatch_kernel(pe_ref, nu_ref, dest_ref, h_ref, xs_ref, dest_smem, zero_ref, sems, *, tm, n_blocks):
    i = pl.program_id(0)

    def zero_fill(slot0):
        rows = MOE_BLOCK * ROW_TILE
        row0 = slot0 * ROW_TILE if isinstance(slot0, int) else pl.multiple_of(slot0 * ROW_TILE, rows)
        return pltpu.make_async_copy(zero_ref, xs_ref.at[pl.ds(row0, rows)], sems.at[1])

    @pl.when(i == 0)
    def _():
        zero_ref[...] = jnp.zeros_like(zero_ref)
        for e in range(N_EXPERTS):
            start = pe_ref[e - 1] if e else 0

            @pl.when(pe_ref[e] > start)
            def _():
                zero_fill(pe_ref[e] - MOE_BLOCK).start()

        def start_unused(b, c):
            zero_fill(b * MOE_BLOCK).start()
            return c

        lax.fori_loop(nu_ref[0], n_blocks, start_unused, 0)
        for e in range(N_EXPERTS):
            start = pe_ref[e - 1] if e else 0

            @pl.when(pe_ref[e] > start)
            def _():
                zero_fill(0).wait()

        def wait_unused(b, c):
            zero_fill(0).wait()
            return c

        lax.fori_loop(nu_ref[0], n_blocks, wait_unused, 0)

    load = pltpu.make_async_copy(dest_ref.at[0, 0], dest_smem, sems.at[0])
    load.start()
    load.wait()

    def issue(t, c):
        for k in range(TOP_K):
            d = dest_smem[k * tm + t]
            pltpu.make_async_copy(_token_tile(h_ref, t), _token_tile(xs_ref, d), sems.at[2]).start(priority=k % 2)
        return c

    lax.fori_loop(0, tm, issue, 0)
    for k in range(TOP_K):
        pltpu.make_async_copy(h_ref, xs_ref.at[pl.ds(0, tm * ROW_TILE)], sems.at[2]).wait()


def _dispatch(h, dest, pad_ends, n_used, n_blocks, tm=512):
    t = h.shape[0] // ROW_TILE
    grid_spec = pltpu.PrefetchScalarGridSpec(
        num_scalar_prefetch=2,
        grid=(t // tm,),
        in_specs=[pl.BlockSpec((1, 1, TOP_K * tm), lambda i, pe, nu: (i, 0, 0)),
                  pl.BlockSpec((tm * ROW_TILE, LANES), lambda i, pe, nu: (i, 0))],
        out_specs=pl.BlockSpec(memory_space=pl.ANY),
        scratch_shapes=[pltpu.SMEM((TOP_K * tm,), jnp.int32),
                        pltpu.VMEM((MOE_BLOCK * ROW_TILE, LANES), jnp.uint32),
                        pltpu.SemaphoreType.DMA((3,))],
    )
    return pl.pallas_call(
        functools.partial(_dispatch_kernel, tm=tm, n_blocks=n_blocks),
        grid_spec=grid_spec,
        out_shape=jax.ShapeDtypeStruct((n_blocks * MOE_BLOCK * ROW_TILE, LANES), jnp.uint32),
        compiler_params=_cparams(("arbitrary",)),
        name="moe_dispatch",
    )(pad_ends, n_used, _dest_tiles(dest, tm), h)


def _experts_kernel(be_ref, nu_ref, first_ref, slot_ref, next_ref, x_ref, wi_ref, wo_ref, o_ref,
                    gbuf, ubuf, obuf, wg_scr, wu_scr, wo_scr, sems, *, layer):
    blk = pl.program_id(0)
    half = ROW_TILE * LANES
    f = wg_scr.shape[1]

    def weight_copies(e, slot):
        return (pltpu.make_async_copy(wi_ref.at[layer, e, :, pl.ds(0, f)], gbuf.at[slot], sems.at[slot, 0]),
                pltpu.make_async_copy(wi_ref.at[layer, e, :, pl.ds(f, f)], ubuf.at[slot], sems.at[slot, 1]),
                pltpu.make_async_copy(wo_ref.at[layer, e], obuf.at[slot], sems.at[slot, 2]))

    @pl.when(blk == 0)
    def _():
        for c in weight_copies(be_ref[0], 0):
            c.start()

    @pl.when(first_ref[blk] == 1)
    def _():
        slot = slot_ref[blk]
        for c in weight_copies(be_ref[blk], slot):
            c.wait()
        wg_scr[...] = gbuf[slot].astype(BF16)
        wu_scr[...] = ubuf[slot].astype(BF16)
        wo_scr[...] = obuf[slot].astype(BF16)

        @pl.when(next_ref[blk] >= 0)
        def _():
            for c in weight_copies(next_ref[blk], 1 - slot):
                c.start()

    @pl.when(blk < nu_ref[0])
    def _():
        lo, hi = _unpack_halves(_load_token_tiles(x_ref, MOE_BLOCK))
        lo, hi = lo.astype(BF16), hi.astype(BF16)
        gate = _dot(lo, wg_scr[0:half, :]) + _dot(hi, wg_scr[half:, :])
        up = _dot(lo, wu_scr[0:half, :]) + _dot(hi, wu_scr[half:, :])
        a = (gate * _sigmoid(gate) * up).astype(BF16)
        _store_token_tiles(o_ref, _pack_halves(_dot(a, wo_scr[...])))

    @pl.when(blk >= nu_ref[0])
    def _():
        o_ref[...] = jnp.zeros_like(o_ref)


def _experts(xs, block_e, n_used, w_in, w_out, layer):
    d = w_in.shape[2]
    f = w_out.shape[2]
    assert d == 2 * ROW_TILE * LANES
    n_blocks = xs.shape[0] // (MOE_BLOCK * ROW_TILE)
    first = jnp.concatenate([jnp.ones((1,), jnp.int32), (block_e[1:] != block_e[:-1]).astype(jnp.int32)])
    slot = (jnp.cumsum(first) - 1) % 2
    later = jnp.where(block_e[None, :] > block_e[:, None], block_e[None, :], N_EXPERTS)
    nxt = jnp.min(later, axis=1)
    nxt = jnp.where(nxt >= N_EXPERTS, -1, nxt)
    grid_spec = pltpu.PrefetchScalarGridSpec(
        num_scalar_prefetch=5,
        grid=(n_blocks,),
        in_specs=[pl.BlockSpec((MOE_BLOCK * ROW_TILE, LANES), lambda i, *_: (i, 0)),
                  pl.BlockSpec(memory_space=pl.ANY),
                  pl.BlockSpec(memory_space=pl.ANY)],
        out_specs=pl.BlockSpec((MOE_BLOCK * ROW_TILE, LANES), lambda i, *_: (i, 0)),
        scratch_shapes=[pltpu.VMEM((2, d, f), F32), pltpu.VMEM((2, d, f), F32), pltpu.VMEM((2, f, d), F32),
                        pltpu.VMEM((d, f), BF16), pltpu.VMEM((d, f), BF16), pltpu.VMEM((f, d), BF16),
                        pltpu.SemaphoreType.DMA((2, 3))],
    )
    return pl.pallas_call(
        functools.partial(_experts_kernel, layer=layer),
        grid_spec=grid_spec,
        out_shape=jax.ShapeDtypeStruct(xs.shape, jnp.uint32),
        compiler_params=_cparams(("arbitrary",)),
        name="moe_experts",
    )(block_e, n_used, first, slot.astype(jnp.int32), nxt.astype(jnp.int32), xs, w_in, w_out)


def _ffn_out_kernel(dest_ref, dest_next_ref, x_ref, gate_ref, h_ref, rg_ref, wg_ref, wu_ref, wo_ref, lng_ref,
                    lnb_ref, ys_ref, o_ref, dest_smem, rows_ref, sems, *, alpha, tm, n_steps):
    i = pl.program_id(0)
    slot = i % 2
    half = ROW_TILE * LANES

    def gather(tile_ref, into):
        load = pltpu.make_async_copy(tile_ref.at[0, 0], dest_smem, sems.at[0])
        load.start()
        load.wait()

        def issue(t, c):
            for k in range(TOP_K):
                d = dest_smem[k * tm + t]
                pltpu.make_async_copy(_token_tile(ys_ref, d), _token_tile(rows_ref.at[into, k], t),
                                      sems.at[1 + into]).start(priority=k % 2)
            return c

        lax.fori_loop(0, tm, issue, 0)

    @pl.when(i == 0)
    def _():
        gather(dest_ref, 0)

    @pl.when(i + 1 < n_steps)
    def _():
        gather(dest_next_ref, 1 - slot)

    lo, hi = _unpack_halves(_load_token_tiles(h_ref, tm))
    lo, hi = lo.astype(BF16), hi.astype(BF16)
    gate = _dot(lo, wg_ref[0:half, :]) + _dot(hi, wg_ref[half:, :])
    up = _dot(lo, wu_ref[0:half, :]) + _dot(hi, wu_ref[half:, :])
    a = (gate * _sigmoid(gate) * up).astype(BF16)
    y = _dot(a, wo_ref[...])
    y_lo, y_hi = y[:, :half], y[:, half:]

    rg = rg_ref[...]
    for k in range(TOP_K):
        pltpu.make_async_copy(ys_ref.at[pl.ds(0, tm * ROW_TILE)], rows_ref.at[slot, k], sems.at[1 + slot]).wait()
    for k in range(TOP_K):
        r_lo, r_hi = _unpack_halves(_load_token_tiles(rows_ref.at[slot, k], tm))
        y_lo = y_lo + r_lo * rg[:, k:k + 1]
        y_hi = y_hi + r_hi * rg[:, k:k + 1]
    z = alpha * x_ref[...] + gate_ref[0] * jnp.concatenate([y_lo, y_hi], axis=-1)
    o_ref[...] = _layer_norm(z, lng_ref[...], lnb_ref[...])


def _ffn_out(x, mod, gate_idx, h, ys, dest, route_gate, sh_in, sh_out, ln_g, ln_b, alpha, geom, tm=256):
    t, d = x.shape
    f = sh_out.shape[0]
    assert d == 2 * ROW_TILE * LANES
    n_steps = t // tm
    row = functools.partial(_mod_row, tm=tm, **geom)
    tiles = _dest_tiles(dest, tm)
    return pl.pallas_call(
        functools.partial(_ffn_out_kernel, alpha=alpha, tm=tm, n_steps=n_steps),
        grid=(n_steps,),
        in_specs=[pl.BlockSpec((1, 1, TOP_K * tm), lambda i: (i, 0, 0)),
                  pl.BlockSpec((1, 1, TOP_K * tm), lambda i: (jnp.minimum(i + 1, n_steps - 1), 0, 0)),
                  pl.BlockSpec((tm, d), lambda i: (i, 0)),
                  pl.BlockSpec((1, 1, d), lambda i: (row(i), 0, gate_idx)),
                  pl.BlockSpec((tm * ROW_TILE, LANES), lambda i: (i, 0)),
                  pl.BlockSpec((tm, TOP_K), lambda i: (i, 0)),
                  pl.BlockSpec((d, f), lambda i: (0, 0)),
                  pl.BlockSpec((d, f), lambda i: (0, 1)),
                  pl.BlockSpec((f, d), lambda i: (0, 0)),
                  pl.BlockSpec((1, d), lambda i: (0, 0)),
                  pl.BlockSpec((1, d), lambda i: (0, 0)),
                  pl.BlockSpec(memory_space=pl.ANY)],
        out_specs=pl.BlockSpec((tm, d), lambda i: (i, 0)),
        out_shape=jax.ShapeDtypeStruct((t, d), F32),
        scratch_shapes=[pltpu.SMEM((TOP_K * tm,), jnp.int32),
                        pltpu.VMEM((2, TOP_K, tm * ROW_TILE, LANES), jnp.uint32),
                        pltpu.SemaphoreType.DMA((3,))],
        compiler_params=_cparams(("arbitrary",)),
        name="ffn_out",
    )(tiles, tiles, x, mod, h, route_gate, sh_in, sh_in, sh_out, ln_g.reshape(1, d), ln_b.reshape(1, d), ys)


def _moe_sublayer(x, mod, geom, layer, router_w, router_bias, w_in, w_out, sh_in, sh_out, ln_g, ln_b, alpha):
    h, top_e, gate, rank, counts = _route(x, mod, 4, 3, router_w, router_bias, geom)
    n_blocks = x.shape[0] * TOP_K // MOE_BLOCK + N_EXPERTS
    dest, pad_ends, block_e, n_used = _dispatch_plan(top_e, rank, counts, n_blocks)
    xs = _dispatch(h, dest, pad_ends, n_used, n_blocks)
    ys = _experts(xs, block_e, n_used, w_in, w_out, layer)
    return _ffn_out(x, mod, 5, h, ys, dest, gate.T, sh_in.astype(BF16), sh_out.astype(BF16), ln_g, ln_b, alpha,
                    geom)


def kernel(x, c, ctx, c_ctx, ada_w, ada_b, ln_mix_g, ln_mix_b, ln_ffn_g, ln_ffn_b, ab_w_in, ab_w_out, diff_lambda, diff_subln_g, gqa_q_norm_g, gqa_k_norm_g, mla_w_down, mla_q_norm_g, mla_w_uq, mla_kv_norm_g, mla_w_ukv, mla_w_o, router_w, router_bias, expert_w_in, expert_w_out, shared_w_in, shared_w_out):
    bsz, seq, d = x.shape
    n_ctx = ctx.shape[1]
    depth = ada_w.shape[0]
    assert depth == 2 and seq % GRID_W == 0 and bsz + 1 <= MOD_ROWS
    alpha = (2 * depth) ** 0.25
    n_ctx_rows = bsz * n_ctx
    n_lat_rows = bsz * seq
    geom_all = dict(n_ctx_rows=n_ctx_rows, seq=seq, ctx_row=bsz)
    geom_lat = dict(n_ctx_rows=0, seq=seq, ctx_row=bsz)

    tables64 = _rope_tables(seq // GRID_W, 64)
    tables128 = _rope_tables(seq // GRID_W, HEAD_DIM)

    cvec = jnp.zeros((MOD_ROWS, d), F32).at[:bsz].set(c).at[bsz].set(c_ctx)
    mods = _ada_mod(cvec, ada_w, ada_b)
    mods = mods.reshape(depth, MOD_ROWS, 1, 6 * d)

    xt = jnp.concatenate([ctx.reshape(n_ctx_rows, d), x.reshape(n_lat_rows, d)], axis=0)

    mod = mods[0]
    n_diff = d // (2 * HEAD_DIM)
    n_gqa = d // (2 * HEAD_DIM)
    n_gqa_kv = n_gqa // 4
    proj = _modproj(xt, mod, 1, 0, ab_w_in[0].astype(BF16), BF16, geom_all)
    lam_init = 0.8 - 0.6 * math.exp(-0.3 * 0)
    oa = _diff_attn(proj, tables64, diff_lambda[0], diff_subln_g[0], lam_init, bsz, n_ctx, seq, n_diff)
    ob = _gqa_attn(proj, tables128, gqa_q_norm_g[0], gqa_k_norm_g[0], 3 * n_diff, bsz, n_ctx, seq, n_gqa, n_gqa_kv)
    w_out = ab_w_out[0].astype(BF16)
    wa, wb = w_out[:n_diff * HEAD_DIM], w_out[n_diff * HEAD_DIM:]
    t_all = n_ctx_rows + n_lat_rows
    xt = _resln(xt, mod, 2, [oa, ob], [wa, wb], ln_mix_g[0], ln_mix_b[0], alpha, geom_all, 0, t_all, 0)
    xt = _moe_sublayer(xt, mod, geom_all, 0, router_w[0], router_bias[0], expert_w_in, expert_w_out,
                       shared_w_in[0], shared_w_out[0], ln_ffn_g[0], ln_ffn_b[0], alpha)

    mod = mods[1]
    n_mla = d // HEAD_DIM
    qk = MLA_NOPE_DIM + MLA_ROPE_DIM
    w_down = jnp.pad(mla_w_down[0], ((0, 0), (0, LANES - MLA_ROPE_DIM))).astype(BF16)
    down = _modproj(xt, mod, 1, 0, w_down, F32, geom_all)
    w_uq = mla_w_uq[0].reshape(MLA_Q_RANK, n_mla, qk)
    w_uq_rope = jnp.pad(w_uq[:, :, MLA_NOPE_DIM:], ((0, 0), (0, 0), (0, LANES - MLA_ROPE_DIM)))
    w_uq = jnp.concatenate([w_uq[:, :, :MLA_NOPE_DIM].reshape(MLA_Q_RANK, -1),
                            w_uq_rope.reshape(MLA_Q_RANK, -1)], axis=1).astype(BF16)
    lat_blocks = n_ctx_rows // 1024
    q = _normproj(down, 0, mla_q_norm_g[0], w_uq, lat_blocks, n_lat_rows)
    kv = _normproj(down, 1, mla_kv_norm_g[0], mla_w_ukv[0].astype(BF16), 0, t_all)
    kpe_col = (MLA_Q_RANK + MLA_KV_RANK) // LANES
    o = _mla_attn(q, kv, down, tables64, bsz, n_ctx, seq, n_mla, kpe_col)
    xl = _resln(xt, mod, 2, [o], [mla_w_o[0].astype(BF16)], ln_mix_g[1], ln_mix_b[1], alpha, geom_all,
                n_ctx_rows // 512, n_lat_rows, 0)
    xl = _moe_sublayer(xl, mod, geom_lat, 1, router_w[1], router_bias[1], expert_w_in, expert_w_out,
                       shared_w_in[1], shared_w_out[1], ln_ffn_g[1], ln_ffn_b[1], alpha)
    return xl.reshape(bsz, seq, d)
```

```python
import functools
import math

import numpy as np
import jax
import jax.numpy as jnp
from jax import lax
from jax.experimental import pallas as pl
from jax.experimental.pallas import tpu as pltpu

F32 = jnp.float32
BF16 = jnp.bfloat16

GRID_W = 64
ROPE_THETA = 10000.0
HEAD_DIM = 128
DIFF_QK_DIM = 64
N_EXPERTS = 64
EXPERT_DIM = 512
TOP_K = 8
N_GROUPS = 8
TOPK_GROUPS = 4
ROUTE_SCALE = 2.5
MLA_Q_RANK = 512
MLA_KV_RANK = 512
MLA_NOPE_DIM = 128
MLA_ROPE_DIM = 64
MLA_V_DIM = 128
LANES = 128

MOD_ROWS = 16
VMEM_LIMIT = 56 * 1024 * 1024
MOE_BLOCK = 512


def _cparams(sem):
    return pltpu.CompilerParams(dimension_semantics=sem, vmem_limit_bytes=VMEM_LIMIT)


def _dot(a, b):
    return jnp.dot(a, b, preferred_element_type=F32)


def _dot_nt(a, b):
    return lax.dot_general(a, b, (((1,), (1,)), ((), ())), preferred_element_type=F32)


def _dot_hi(a, b):
    return lax.dot_general(a, b, (((1,), (0,)), ((), ())), precision=lax.Precision.HIGHEST,
                           preferred_element_type=F32)


def _sigmoid(x):
    return 1.0 / (1.0 + jnp.exp(-x))


def _rope(x, cos, sin_up, sin_dn, shift):
    return (x * cos + pltpu.roll(x, LANES - shift, 1) * sin_up + pltpu.roll(x, shift, 1) * sin_dn)


def _rope_tables(rows, rot_dim):
    t = jnp.arange(rows * GRID_W)
    row = (t // GRID_W).astype(F32)
    col = (t % GRID_W).astype(F32)
    axis_dim = rot_dim // 2
    quarter = rot_dim // 4
    inv_freq = 1.0 / (ROPE_THETA ** (jnp.arange(0, axis_dim, 2, dtype=F32) / axis_dim))
    lane = np.arange(LANES) % rot_dim
    is_col = lane >= axis_dim
    within = lane % axis_dim
    first = within < quarter
    freq = within % quarter
    ang = jnp.where(jnp.asarray(is_col)[None, :], col[:, None], row[:, None]) * inv_freq[freq][None, :]
    cos = jnp.cos(ang).astype(F32)
    sin = jnp.sin(ang).astype(F32)
    first = jnp.asarray(first)[None, :]
    sin_up = jnp.where(first, -sin, 0.0)
    sin_dn = jnp.where(first, 0.0, sin)
    return cos, sin_up, sin_dn, quarter


LOG2E = 1.4426950408889634
HEADS_PER_STEP = 4


def _exp2_rows(s):
    e = jnp.exp2(s - jnp.max(s, axis=-1, keepdims=True))
    return e, 1.0 / jnp.sum(e, axis=-1, keepdims=True)


def _rms(x, g, eps):
    return x * lax.rsqrt(jnp.mean(x * x, axis=-1, keepdims=True) + eps) * g


def _ada_kernel(c_ref, w_ref, b_ref, o_ref):
    c = c_ref[...]
    act = c * _sigmoid(c)
    o_ref[0] = _dot_hi(act, w_ref[0]) + b_ref[0]


def _ada_mod(cvec, ada_w, ada_b):
    depth, d, n = ada_w.shape
    tn = 1536
    return pl.pallas_call(
        _ada_kernel,
        grid=(depth, n // tn),
        in_specs=[pl.BlockSpec((MOD_ROWS, d), lambda l, j: (0, 0)),
                  pl.BlockSpec((1, d, tn), lambda l, j: (l, 0, j)),
                  pl.BlockSpec((1, 1, tn), lambda l, j: (l, 0, j))],
        out_specs=pl.BlockSpec((1, MOD_ROWS, tn), lambda l, j: (l, 0, j)),
        out_shape=jax.ShapeDtypeStruct((depth, MOD_ROWS, n), F32),
        compiler_params=_cparams(("arbitrary", "arbitrary")),
        name="ada_mod",
    )(cvec, ada_w, ada_b.reshape(depth, 1, n))


def _mod_row(i, tm, n_ctx_rows, seq, ctx_row):
    r0 = i * tm
    return jnp.where(r0 < n_ctx_rows, ctx_row, (r0 - n_ctx_rows) // seq)


def _modproj_kernel(x_ref, sc_ref, sh_ref, w_ref, o_ref, h_ref):
    @pl.when(pl.program_id(1) == 0)
    def _():
        h_ref[...] = (x_ref[...] * (1.0 + sc_ref[0]) + sh_ref[0]).astype(h_ref.dtype)

    o_ref[...] = _dot(h_ref[...], w_ref[...]).astype(o_ref.dtype)


def _modproj(x, mod, sc_idx, sh_idx, w, out_dtype, geom, tm=1024, tn=1536):
    t, d = x.shape
    n = w.shape[1]
    tn = tn if n % tn == 0 else n
    row = functools.partial(_mod_row, tm=tm, **geom)
    return pl.pallas_call(
        _modproj_kernel,
        grid=(t // tm, n // tn),
        in_specs=[pl.BlockSpec((tm, d), lambda i, j: (i, 0)),
                  pl.BlockSpec((1, 1, d), lambda i, j: (row(i), 0, sc_idx)),
                  pl.BlockSpec((1, 1, d), lambda i, j: (row(i), 0, sh_idx)),
                  pl.BlockSpec((d, tn), lambda i, j: (0, j))],
        out_specs=pl.BlockSpec((tm, tn), lambda i, j: (i, j)),
        out_shape=jax.ShapeDtypeStruct((t, n), out_dtype),
        scratch_shapes=[pltpu.VMEM((tm, d), BF16)],
        compiler_params=_cparams(("arbitrary", "arbitrary")),
        name="modproj",
    )(x, mod, mod, w)


def _normproj_kernel(x_ref, g_ref, w_ref, o_ref, h_ref):
    @pl.when(pl.program_id(1) == 0)
    def _():
        h_ref[...] = _rms(x_ref[...], g_ref[...], 1e-6).astype(h_ref.dtype)

    o_ref[...] = _dot(h_ref[...], w_ref[...]).astype(o_ref.dtype)


def _normproj(x, col_block, g, w, row_off_blocks, n_rows, tm=1024, tn=2048):
    k = w.shape[0]
    n = w.shape[1]
    return pl.pallas_call(
        _normproj_kernel,
        grid=(n_rows // tm, n // tn),
        in_specs=[pl.BlockSpec((tm, k), lambda i, j: (i + row_off_blocks, col_block)),
                  pl.BlockSpec((1, k), lambda i, j: (0, 0)),
                  pl.BlockSpec((k, tn), lambda i, j: (0, j))],
        out_specs=pl.BlockSpec((tm, tn), lambda i, j: (i, j)),
        out_shape=jax.ShapeDtypeStruct((n_rows, n), BF16),
        scratch_shapes=[pltpu.VMEM((tm, k), BF16)],
        compiler_params=_cparams(("arbitrary", "arbitrary")),
        name="normproj",
    )(x, g.reshape(1, k), w)


def _diff_lambda(lam_ref, lam_init):
    lp = lam_ref[...]
    s01 = jnp.sum(lp[0:1] * lp[1:2], axis=-1, keepdims=True)
    s23 = jnp.sum(lp[2:3] * lp[3:4], axis=-1, keepdims=True)
    return jnp.exp(s01) - jnp.exp(s23) + lam_init


def _diff_logits(q, k):
    lane = lax.broadcasted_iota(jnp.int32, q.shape, 1)
    qs = q * (DIFF_QK_DIM ** -0.5 * LOG2E)
    q1 = jnp.where(lane < DIFF_QK_DIM, qs, 0.0).astype(BF16)
    q2 = jnp.where(lane >= DIFF_QK_DIM, qs, 0.0).astype(BF16)
    return _dot_nt(q1, k), _dot_nt(q2, k)


def _diff_combine(s1, s2, v, lam, g, lam_init):
    e1, r1 = _exp2_rows(s1)
    e2, r2 = _exp2_rows(s2)
    a = (e1 - e2 * (lam * r2 / r1)).astype(BF16)
    return _rms(_dot(a, v) * r1, g, 1e-5) * (1.0 - lam_init)


def _head_cols(j):
    return slice(j * LANES, (j + 1) * LANES)


def _diff_attn_kernel(q_ref, kc_ref, kl_ref, vc_ref, vl_ref, cos_ref, su_ref, sd_ref, lam_ref, g_ref,
                      o_ref, k_scr, v_scr, *, n_ctx, tq, shift, lam_init):
    i = pl.program_id(2)
    heads = range(HEADS_PER_STEP)

    @pl.when(i == 0)
    def _prep():
        for j in heads:
            k_scr[j, 0:n_ctx, :] = kc_ref[:, _head_cols(j)]
            kl = kl_ref[:, _head_cols(j)].astype(F32)
            k_scr[j, n_ctx:, :] = _rope(kl, cos_ref[...], su_ref[...], sd_ref[...], shift).astype(BF16)
            v_scr[j, 0:n_ctx, :] = vc_ref[:, _head_cols(j)]
            v_scr[j, n_ctx:, :] = vl_ref[:, _head_cols(j)]

    lam = _diff_lambda(lam_ref, lam_init)
    g = g_ref[...]

    def attend(qs, n_keys):
        logits = [_diff_logits(qs[j], k_scr[j, 0:n_keys, :]) for j in heads]
        for j in heads:
            o = _diff_combine(*logits[j], v_scr[j, 0:n_keys, :], lam, g, lam_init)
            o_ref[:, _head_cols(j)] = o.astype(o_ref.dtype)

    @pl.when(i == 0)
    def _ctx_queries():
        attend([q_ref[:, _head_cols(j)].astype(F32) for j in heads], n_ctx)

    @pl.when(i > 0)
    def _lat_queries():
        r0 = pl.multiple_of((i - 1) * tq, tq)
        cos, su, sd = cos_ref[pl.ds(r0, tq), :], su_ref[pl.ds(r0, tq), :], sd_ref[pl.ds(r0, tq), :]
        attend([_rope(q_ref[:, _head_cols(j)].astype(F32), cos, su, sd, shift) for j in heads],
               k_scr.shape[1])


def _q_block(b, i, bsz, n_ctx, seq, tq):
    return jnp.where(i == 0, b * (n_ctx // tq), (bsz * n_ctx + b * seq) // tq + i - 1)


def _diff_attn(proj, tables, lam_params, subln_g, lam_init, bsz, n_ctx, seq, n_heads, tq=256):
    t = proj.shape[0]
    cos, su, sd, shift = tables
    hp = HEADS_PER_STEP
    wide = hp * LANES
    n_pairs = n_heads // hp
    qb = functools.partial(_q_block, bsz=bsz, n_ctx=n_ctx, seq=seq, tq=tq)
    lat0 = bsz * n_ctx // seq
    kern = functools.partial(_diff_attn_kernel, n_ctx=n_ctx, tq=tq, shift=shift, lam_init=lam_init)
    tab = pl.BlockSpec((seq, LANES), lambda b, h, i: (0, 0))
    return pl.pallas_call(
        kern,
        grid=(bsz, n_pairs, 1 + seq // tq),
        in_specs=[pl.BlockSpec((tq, wide), lambda b, h, i: (qb(b, i), h)),
                  pl.BlockSpec((n_ctx, wide), lambda b, h, i: (b, n_pairs + h)),
                  pl.BlockSpec((seq, wide), lambda b, h, i: (lat0 + b, n_pairs + h)),
                  pl.BlockSpec((n_ctx, wide), lambda b, h, i: (b, 2 * n_pairs + h)),
                  pl.BlockSpec((seq, wide), lambda b, h, i: (lat0 + b, 2 * n_pairs + h)),
                  tab, tab, tab,
                  pl.BlockSpec((4, DIFF_QK_DIM), lambda b, h, i: (0, 0)),
                  pl.BlockSpec((1, LANES), lambda b, h, i: (0, 0))],
        out_specs=pl.BlockSpec((tq, wide), lambda b, h, i: (qb(b, i), h)),
        out_shape=jax.ShapeDtypeStruct((t, n_heads * LANES), BF16),
        scratch_shapes=[pltpu.VMEM((hp, n_ctx + seq, LANES), BF16), pltpu.VMEM((hp, n_ctx + seq, LANES), BF16)],
        compiler_params=_cparams(("arbitrary", "arbitrary", "arbitrary")),
        name="diff_attn",
    )(proj, proj, proj, proj, proj, cos, su, sd, lam_params, subln_g.reshape(1, LANES))


def _softmax_pv(s, v):
    e, r = _exp2_rows(s)
    return _dot(e.astype(BF16), v) * r


def _gqa_attn_kernel(q_ref, kc_ref, kl_ref, vc_ref, vl_ref, cos_ref, su_ref, sd_ref, qg_ref, kg_ref,
                     o_ref, k_scr, v_scr, *, n_ctx, tq, shift):
    g = pl.program_id(2)
    i = pl.program_id(3)
    heads = range(HEADS_PER_STEP)

    @pl.when((g == 0) & (i == 0))
    def _prep():
        kg = kg_ref[...]
        k_scr[0:n_ctx, :] = _rms(kc_ref[...].astype(F32), kg, 1e-6).astype(BF16)
        kl = _rms(kl_ref[...].astype(F32), kg, 1e-6)
        k_scr[n_ctx:, :] = _rope(kl, cos_ref[...], su_ref[...], sd_ref[...], shift).astype(BF16)
        v_scr[0:n_ctx, :] = vc_ref[...]
        v_scr[n_ctx:, :] = vl_ref[...]

    scale = HEAD_DIM ** -0.5 * LOG2E

    def attend(qs, n_keys):
        logits = [_dot_nt((qs[j] * scale).astype(BF16), k_scr[0:n_keys, :]) for j in heads]
        for j in heads:
            o_ref[:, _head_cols(j)] = _softmax_pv(logits[j], v_scr[0:n_keys, :]).astype(o_ref.dtype)

    @pl.when(i == 0)
    def _ctx_queries():
        attend([_rms(q_ref[:, _head_cols(j)].astype(F32), qg_ref[...], 1e-6) for j in heads], n_ctx)

    @pl.when(i > 0)
    def _lat_queries():
        r0 = pl.multiple_of((i - 1) * tq, tq)
        cos, su, sd = cos_ref[pl.ds(r0, tq), :], su_ref[pl.ds(r0, tq), :], sd_ref[pl.ds(r0, tq), :]
        attend([_rope(_rms(q_ref[:, _head_cols(j)].astype(F32), qg_ref[...], 1e-6), cos, su, sd, shift)
                for j in heads], k_scr.shape[0])


def _gqa_attn(proj, tables, q_norm_g, k_norm_g, col0, bsz, n_ctx, seq, n_heads, n_kv, tq=256):
    t = proj.shape[0]
    cos, su, sd, shift = tables
    hp = HEADS_PER_STEP
    wide = hp * LANES
    pairs = n_heads // n_kv // hp
    assert col0 % hp == 0
    qb = functools.partial(_q_block, bsz=bsz, n_ctx=n_ctx, seq=seq, tq=tq)
    lat0 = bsz * n_ctx // seq
    kcol = col0 + n_heads
    vcol = kcol + n_kv
    kern = functools.partial(_gqa_attn_kernel, n_ctx=n_ctx, tq=tq, shift=shift)
    tab = pl.BlockSpec((seq, LANES), lambda b, h, g, i: (0, 0))
    vec = pl.BlockSpec((1, LANES), lambda b, h, g, i: (0, 0))
    return pl.pallas_call(
        kern,
        grid=(bsz, n_kv, pairs, 1 + seq // tq),
        in_specs=[pl.BlockSpec((tq, wide), lambda b, h, g, i: (qb(b, i), col0 // hp + h * pairs + g)),
                  pl.BlockSpec((n_ctx, LANES), lambda b, h, g, i: (b, kcol + h)),
                  pl.BlockSpec((seq, LANES), lambda b, h, g, i: (lat0 + b, kcol + h)),
                  pl.BlockSpec((n_ctx, LANES), lambda b, h, g, i: (b, vcol + h)),
                  pl.BlockSpec((seq, LANES), lambda b, h, g, i: (lat0 + b, vcol + h)),
                  tab, tab, tab, vec, vec],
        out_specs=pl.BlockSpec((tq, wide), lambda b, h, g, i: (qb(b, i), h * pairs + g)),
        out_shape=jax.ShapeDtypeStruct((t, n_heads * LANES), BF16),
        scratch_shapes=[pltpu.VMEM((n_ctx + seq, LANES), BF16), pltpu.VMEM((n_ctx + seq, LANES), BF16)],
        compiler_params=_cparams(("arbitrary", "arbitrary", "arbitrary", "arbitrary")),
        name="gqa_attn",
    )(proj, proj, proj, proj, proj, cos, su, sd, q_norm_g.reshape(1, LANES), k_norm_g.reshape(1, LANES))


def _mla_attn_kernel(qn_ref, qp_ref, kvc_ref, kvl_ref, kpc_ref, kpl_ref, cos_ref, su_ref, sd_ref,
                     o_ref, k_scr, v_scr, *, n_ctx, tq, shift):
    i = pl.program_id(2)

    @pl.when(i == 0)
    def _prep():
        kpc = kpc_ref[...].astype(BF16)
        kpl = _rope(kpl_ref[...], cos_ref[...], su_ref[...], sd_ref[...], shift).astype(BF16)
        for j in range(HEADS_PER_STEP):
            c0 = 2 * j * LANES
            k_scr[j, 0:n_ctx, 0:LANES] = kvc_ref[:, c0:c0 + LANES]
            k_scr[j, n_ctx:, 0:LANES] = kvl_ref[:, c0:c0 + LANES]
            k_scr[j, 0:n_ctx, LANES:] = kpc
            k_scr[j, n_ctx:, LANES:] = kpl
            v_scr[j, 0:n_ctx, :] = kvc_ref[:, c0 + LANES:c0 + 2 * LANES]
            v_scr[j, n_ctx:, :] = kvl_ref[:, c0 + LANES:c0 + 2 * LANES]

    scale = (MLA_NOPE_DIM + MLA_ROPE_DIM) ** -0.5 * LOG2E
    r0 = pl.multiple_of(i * tq, tq)
    cos, su, sd = cos_ref[pl.ds(r0, tq), :], su_ref[pl.ds(r0, tq), :], sd_ref[pl.ds(r0, tq), :]
    logits = []
    for j in range(HEADS_PER_STEP):
        qp = _rope(qp_ref[:, _head_cols(j)].astype(F32), cos, su, sd, shift)
        q = jnp.concatenate([(qn_ref[:, _head_cols(j)].astype(F32) * scale).astype(BF16), (qp * scale).astype(BF16)],
                            axis=-1)
        logits.append(_dot_nt(q, k_scr[j]))
    for j in range(HEADS_PER_STEP):
        o_ref[:, _head_cols(j)] = _softmax_pv(logits[j], v_scr[j]).astype(o_ref.dtype)


def _mla_attn(q, kv, down, tables, bsz, n_ctx, seq, n_heads, kpe_col, tq=256):
    cos, su, sd, shift = tables
    lat0 = bsz * n_ctx // seq
    nq = seq // tq
    hp = HEADS_PER_STEP
    kern = functools.partial(_mla_attn_kernel, n_ctx=n_ctx, tq=tq, shift=shift)
    tab = pl.BlockSpec((seq, LANES), lambda b, h, i: (0, 0))
    return pl.pallas_call(
        kern,
        grid=(bsz, n_heads // hp, nq),
        in_specs=[pl.BlockSpec((tq, hp * LANES), lambda b, h, i: (b * nq + i, h)),
                  pl.BlockSpec((tq, hp * LANES), lambda b, h, i: (b * nq + i, n_heads // hp + h)),
                  pl.BlockSpec((n_ctx, 2 * hp * LANES), lambda b, h, i: (b, h)),
                  pl.BlockSpec((seq, 2 * hp * LANES), lambda b, h, i: (lat0 + b, h)),
                  pl.BlockSpec((n_ctx, LANES), lambda b, h, i: (b, kpe_col)),
                  pl.BlockSpec((seq, LANES), lambda b, h, i: (lat0 + b, kpe_col)),
                  tab, tab, tab],
        out_specs=pl.BlockSpec((tq, hp * LANES), lambda b, h, i: (b * nq + i, h)),
        out_shape=jax.ShapeDtypeStruct((bsz * seq, n_heads * LANES), BF16),
        scratch_shapes=[pltpu.VMEM((hp, n_ctx + seq, 2 * LANES), BF16), pltpu.VMEM((hp, n_ctx + seq, LANES), BF16)],
        compiler_params=_cparams(("arbitrary", "arbitrary", "arbitrary")),
        name="mla_attn",
    )(q, q, kv, kv, down, down, cos, su, sd)


def _layer_norm(z, g, b):
    zc = z - jnp.mean(z, axis=-1, keepdims=True)
    var = jnp.mean(zc * zc, axis=-1, keepdims=True)
    return zc * lax.rsqrt(var + 1e-5) * g + b


def _resln_kernel(*refs, n_act, alpha):
    x_ref, gate_ref = refs[0], refs[1]
    acts = refs[2:2 + n_act]
    ws = refs[2 + n_act:2 + 2 * n_act]
    lng_ref, lnb_ref, o_ref = refs[2 + 2 * n_act:]
    y = _dot(acts[0][...], ws[0][...])
    for a_ref, w_ref in zip(acts[1:], ws[1:]):
        y = y + _dot(a_ref[...], w_ref[...])
    z = alpha * x_ref[...] + gate_ref[0] * y
    o_ref[...] = _layer_norm(z, lng_ref[...], lnb_ref[...])


def _resln(x, mod, gate_idx, acts, ws, ln_g, ln_b, alpha, geom, row_off_blocks, n_rows, act_off_blocks, tm=512):
    d = x.shape[1]
    row = functools.partial(_mod_row, tm=tm, **geom)
    n_act = len(acts)
    in_specs = [pl.BlockSpec((tm, d), lambda i: (i + row_off_blocks, 0)),
                pl.BlockSpec((1, 1, d), lambda i: (row(i + row_off_blocks), 0, gate_idx))]
    in_specs += [pl.BlockSpec((tm, a.shape[1]), lambda i: (i + act_off_blocks, 0)) for a in acts]
    in_specs += [pl.BlockSpec(w.shape, lambda i: (0, 0)) for w in ws]
    in_specs += [pl.BlockSpec((1, d), lambda i: (0, 0))] * 2
    return pl.pallas_call(
        functools.partial(_resln_kernel, n_act=n_act, alpha=alpha),
        grid=(n_rows // tm,),
        in_specs=in_specs,
        out_specs=pl.BlockSpec((tm, d), lambda i: (i, 0)),
        out_shape=jax.ShapeDtypeStruct((n_rows, d), F32),
        compiler_params=_cparams(("arbitrary",)),
        name="proj_resln",
    )(x, mod, *acts, *ws, ln_g.reshape(1, d), ln_b.reshape(1, d))


def _pack_halves(x):
    n = x.shape[1] // 2
    bits = lax.bitcast_convert_type(x.astype(BF16).astype(F32), jnp.uint32)
    return (bits[:, :n] >> 16) | (bits[:, n:] & jnp.uint32(0xFFFF0000))


def _unpack_halves(w):
    lo = lax.bitcast_convert_type(w << 16, F32)
    hi = lax.bitcast_convert_type(w & jnp.uint32(0xFFFF0000), F32)
    return lo, hi


ROW_TILE = 8


def _store_token_tiles(ref, packed):
    m = packed.shape[0]
    for s in range(ROW_TILE):
        ref[pl.ds(s, m, stride=ROW_TILE), :] = packed[:, s * LANES:(s + 1) * LANES]


def _load_token_tiles(ref, m):
    return jnp.concatenate([ref[pl.ds(s, m, stride=ROW_TILE), :] for s in range(ROW_TILE)], axis=-1)


def _token_tile(ref, tok):
    return ref.at[pl.ds(pl.multiple_of(tok * ROW_TILE, ROW_TILE), ROW_TILE)]


def _pick_first_max(cur, idx, axes, sentinel):
    m = cur
    for ax in axes:
        m = jnp.max(m, axis=ax, keepdims=True)
    first = jnp.where(cur == m, idx, sentinel)
    for ax in axes:
        first = jnp.min(first, axis=ax, keepdims=True)
    return m, first


def _route_kernel(x_ref, sc_ref, sh_ref, rwt_ref, bias_ref, tri_ref, h_ref, e_ref, g_ref, r_ref, cnt_ref, run_ref):
    tm = x_ref.shape[0]
    n_members = N_EXPERTS // N_GROUPS

    @pl.when(pl.program_id(0) == 0)
    def _():
        run_ref[...] = jnp.zeros_like(run_ref)

    h = x_ref[...] * (1.0 + sc_ref[0]) + sh_ref[0]
    _store_token_tiles(h_ref, _pack_halves(h))
    logits = lax.dot_general(rwt_ref[...], h, (((1,), (1,)), ((), ())), precision=lax.Precision.HIGHEST,
                             preferred_element_type=F32)
    scores = _sigmoid(logits).reshape(N_GROUPS, n_members, tm)
    biased = scores + bias_ref[...]
    neg = -jnp.inf
    member = lax.broadcasted_iota(jnp.int32, biased.shape, 1).astype(F32)
    group = lax.broadcasted_iota(jnp.int32, biased.shape, 0).astype(F32)
    expert = group * n_members + member

    m1, first = _pick_first_max(biased, member, (1,), float(n_members))
    m2 = jnp.max(jnp.where(member == first, neg, biased), axis=1, keepdims=True)
    gscore = m1 + m2
    gidx = lax.broadcasted_iota(jnp.int32, gscore.shape, 0).astype(F32)
    group_ok = jnp.zeros(gscore.shape, jnp.bool_)
    for _ in range(TOPK_GROUPS):
        _, first = _pick_first_max(gscore, gidx, (0,), float(N_GROUPS))
        pick = gidx == first
        group_ok = group_ok | pick
        gscore = jnp.where(pick, neg, gscore)

    cur = jnp.where(group_ok, biased, neg)
    chosen = jnp.zeros(biased.shape, jnp.bool_)
    top_e, gates = [], []
    for _ in range(TOP_K):
        _, first = _pick_first_max(cur, expert, (0, 1), float(N_EXPERTS))
        pick = expert == first
        chosen = chosen | pick
        cur = jnp.where(pick, neg, cur)
        top_e.append(first)
        gates.append(jnp.sum(jnp.sum(jnp.where(pick, scores, 0.0), axis=0, keepdims=True), axis=1, keepdims=True))
    gsum = gates[0]
    for gk in gates[1:]:
        gsum = gsum + gk
    norm = ROUTE_SCALE / gsum

    chosen2d = jnp.where(chosen, 1.0, 0.0).reshape(N_EXPERTS, tm)
    before = _dot(chosen2d.astype(BF16), tri_ref[...])
    rank = (run_ref[...] + before).reshape(N_GROUPS, n_members, tm)
    run_ref[...] = run_ref[...] + jnp.sum(chosen2d, axis=1, keepdims=True)
    cnt_ref[...] = run_ref[...].astype(jnp.int32)
    for k in range(TOP_K):
        pick = expert == top_e[k]
        rk = jnp.sum(jnp.sum(jnp.where(pick, rank, 0.0), axis=0, keepdims=True), axis=1, keepdims=True)
        e_ref[k:k + 1, :] = top_e[k].reshape(1, tm).astype(jnp.int32)
        g_ref[k:k + 1, :] = (gates[k] * norm).reshape(1, tm)
        r_ref[k:k + 1, :] = rk.reshape(1, tm).astype(jnp.int32)


def _route(x, mod, sc_idx, sh_idx, router_w, router_bias, geom, tm=512):
    t, d = x.shape
    row = functools.partial(_mod_row, tm=tm, **geom)
    tri = (np.arange(tm)[:, None] < np.arange(tm)[None, :]).astype(np.float32)
    kt = pl.BlockSpec((TOP_K, tm), lambda i: (0, i))
    return pl.pallas_call(
        _route_kernel,
        grid=(t // tm,),
        in_specs=[pl.BlockSpec((tm, d), lambda i: (i, 0)),
                  pl.BlockSpec((1, 1, d), lambda i: (row(i), 0, sc_idx)),
                  pl.BlockSpec((1, 1, d), lambda i: (row(i), 0, sh_idx)),
                  pl.BlockSpec((N_EXPERTS, d), lambda i: (0, 0)),
                  pl.BlockSpec((N_GROUPS, N_EXPERTS // N_GROUPS, 1), lambda i: (0, 0, 0)),
                  pl.BlockSpec((tm, tm), lambda i: (0, 0))],
        out_specs=[pl.BlockSpec((tm * ROW_TILE, LANES), lambda i: (i, 0)), kt, kt, kt,
                   pl.BlockSpec((N_EXPERTS, 1), lambda i: (0, 0))],
        out_shape=[jax.ShapeDtypeStruct((t * ROW_TILE, LANES), jnp.uint32),
                   jax.ShapeDtypeStruct((TOP_K, t), jnp.int32),
                   jax.ShapeDtypeStruct((TOP_K, t), F32),
                   jax.ShapeDtypeStruct((TOP_K, t), jnp.int32),
                   jax.ShapeDtypeStruct((N_EXPERTS, 1), jnp.int32)],
        scratch_shapes=[pltpu.VMEM((N_EXPERTS, 1), F32)],
        compiler_params=_cparams(("arbitrary",)),
        name="moe_route",
    )(x, mod, mod, router_w.T, router_bias.astype(F32).reshape(N_GROUPS, N_EXPERTS // N_GROUPS, 1),
      jnp.asarray(tri, BF16))


def _dispatch_plan(top_e, rank, counts, n_blocks):
    counts = counts.reshape(N_EXPERTS)
    padded = (counts + MOE_BLOCK - 1) // MOE_BLOCK * MOE_BLOCK
    pad_ends = jnp.cumsum(padded)
    pad_starts = pad_ends - padded
    onehot = top_e[:, :, None] == jnp.arange(N_EXPERTS, dtype=jnp.int32)
    dest = jnp.sum(jnp.where(onehot, pad_starts, 0), axis=-1) + rank
    block_start = jnp.arange(n_blocks, dtype=jnp.int32) * MOE_BLOCK
    block_e = jnp.minimum(jnp.sum(pad_ends[None, :] <= block_start[:, None], axis=1), N_EXPERTS - 1)
    n_used = (pad_ends[-1] // MOE_BLOCK).astype(jnp.int32).reshape(1)
    return dest, pad_ends.astype(jnp.int32), block_e.astype(jnp.int32), n_used


def _dest_tiles(dest, tm):
    k, t = dest.shape
    return dest.reshape(k, t // tm, tm).transpose(1, 0, 2).reshape(t // tm, 1, k * tm)


def _dispatch_kernel(pe_ref, nu_ref, dest_ref, h_ref, xs_ref, dest_smem, zero_ref, sems, *, tm, n_blocks):
    i = pl.program_id(0)

    def zero_fill(slot0):
        rows = MOE_BLOCK * ROW_TILE
        row0 = slot0 * ROW_TILE if isinstance(slot0, int) else pl.multiple_of(slot0 * ROW_TILE, rows)
        return pltpu.make_async_copy(zero_ref, xs_ref.at[pl.ds(row0, rows)], sems.at[1])

    @pl.when(i == 0)
    def _():
        zero_ref[...] = jnp.zeros_like(zero_ref)
        for e in range(N_EXPERTS):
            start = pe_ref[e - 1] if e else 0

            @pl.when(pe_ref[e] > start)
            def _():
                zero_fill(pe_ref[e] - MOE_BLOCK).start()

        def start_unused(b, c):
            zero_fill(b * MOE_BLOCK).start()
            return c

        lax.fori_loop(nu_ref[0], n_blocks, start_unused, 0)
        for e in range(N_EXPERTS):
            start = pe_ref[e - 1] if e else 0

            @pl.when(pe_ref[e] > start)
            def _():
                zero_fill(0).wait()

        def wait_unused(b, c):
            zero_fill(0).wait()
            return c

        lax.fori_loop(nu_ref[0], n_blocks, wait_unused, 0)

    load = pltpu.make_async_copy(dest_ref.at[0, 0], dest_smem, sems.at[0])
    load.start()
    load.wait()

    def issue(t, c):
        for k in range(TOP_K):
            d = dest_smem[k * tm + t]
            pltpu.make_async_copy(_token_tile(h_ref, t), _token_tile(xs_ref, d), sems.at[2]).start(priority=k % 2)
        return c

    lax.fori_loop(0, tm, issue, 0)
    for k in range(TOP_K):
        pltpu.make_async_copy(h_ref, xs_ref.at[pl.ds(0, tm * ROW_TILE)], sems.at[2]).wait()


def _dispatch(h, dest, pad_ends, n_used, n_blocks, tm=512):
    t = h.shape[0] // ROW_TILE
    grid_spec = pltpu.PrefetchScalarGridSpec(
        num_scalar_prefetch=2,
        grid=(t // tm,),
        in_specs=[pl.BlockSpec((1, 1, TOP_K * tm), lambda i, pe, nu: (i, 0, 0)),
                  pl.BlockSpec((tm * ROW_TILE, LANES), lambda i, pe, nu: (i, 0))],
        out_specs=pl.BlockSpec(memory_space=pl.ANY),
        scratch_shapes=[pltpu.SMEM((TOP_K * tm,), jnp.int32),
                        pltpu.VMEM((MOE_BLOCK * ROW_TILE, LANES), jnp.uint32),
                        pltpu.SemaphoreType.DMA((3,))],
    )
    return pl.pallas_call(
        functools.partial(_dispatch_kernel, tm=tm, n_blocks=n_blocks),
        grid_spec=grid_spec,
        out_shape=jax.ShapeDtypeStruct((n_blocks * MOE_BLOCK * ROW_TILE, LANES), jnp.uint32),
        compiler_params=_cparams(("arbitrary",)),
        name="moe_dispatch",
    )(pad_ends, n_used, _dest_tiles(dest, tm), h)


def _experts_kernel(be_ref, nu_ref, first_ref, slot_ref, next_ref, x_ref, wi_ref, wo_ref, o_ref,
                    gbuf, ubuf, obuf, wg_scr, wu_scr, wo_scr, sems, *, layer):
    blk = pl.program_id(0)
    half = ROW_TILE * LANES
    f = wg_scr.shape[1]

    def weight_copies(e, slot):
        return (pltpu.make_async_copy(wi_ref.at[layer, e, :, pl.ds(0, f)], gbuf.at[slot], sems.at[slot, 0]),
                pltpu.make_async_copy(wi_ref.at[layer, e, :, pl.ds(f, f)], ubuf.at[slot], sems.at[slot, 1]),
                pltpu.make_async_copy(wo_ref.at[layer, e], obuf.at[slot], sems.at[slot, 2]))

    @pl.when(blk == 0)
    def _():
        for c in weight_copies(be_ref[0], 0):
            c.start()

    @pl.when(first_ref[blk] == 1)
    def _():
        slot = slot_ref[blk]
        for c in weight_copies(be_ref[blk], slot):
            c.wait()
        wg_scr[...] = gbuf[slot].astype(BF16)
        wu_scr[...] = ubuf[slot].astype(BF16)
        wo_scr[...] = obuf[slot].astype(BF16)

        @pl.when(next_ref[blk] >= 0)
        def _():
            for c in weight_copies(next_ref[blk], 1 - slot):
                c.start()

    @pl.when(blk < nu_ref[0])
    def _():
        lo, hi = _unpack_halves(_load_token_tiles(x_ref, MOE_BLOCK))
        lo, hi = lo.astype(BF16), hi.astype(BF16)
        gate = _dot(lo, wg_scr[0:half, :]) + _dot(hi, wg_scr[half:, :])
        up = _dot(lo, wu_scr[0:half, :]) + _dot(hi, wu_scr[half:, :])
        a = (gate * _sigmoid(gate) * up).astype(BF16)
        _store_token_tiles(o_ref, _pack_halves(_dot(a, wo_scr[...])))

    @pl.when(blk >= nu_ref[0])
    def _():
        o_ref[...] = jnp.zeros_like(o_ref)


def _experts(xs, block_e, n_used, w_in, w_out, layer):
    d = w_in.shape[2]
    f = w_out.shape[2]
    assert d == 2 * ROW_TILE * LANES
    n_blocks = xs.shape[0] // (MOE_BLOCK * ROW_TILE)
    first = jnp.concatenate([jnp.ones((1,), jnp.int32), (block_e[1:] != block_e[:-1]).astype(jnp.int32)])
    slot = (jnp.cumsum(first) - 1) % 2
    later = jnp.where(block_e[None, :] > block_e[:, None], block_e[None, :], N_EXPERTS)
    nxt = jnp.min(later, axis=1)
    nxt = jnp.where(nxt >= N_EXPERTS, -1, nxt)
    grid_spec = pltpu.PrefetchScalarGridSpec(
        num_scalar_prefetch=5,
        grid=(n_blocks,),
        in_specs=[pl.BlockSpec((MOE_BLOCK * ROW_TILE, LANES), lambda i, *_: (i, 0)),
                  pl.BlockSpec(memory_space=pl.ANY),
                  pl.BlockSpec(memory_space=pl.ANY)],
        out_specs=pl.BlockSpec((MOE_BLOCK * ROW_TILE, LANES), lambda i, *_: (i, 0)),
        scratch_shapes=[pltpu.VMEM((2, d, f), F32), pltpu.VMEM((2, d, f), F32), pltpu.VMEM((2, f, d), F32),
                        pltpu.VMEM((d, f), BF16), pltpu.VMEM((d, f), BF16), pltpu.VMEM((f, d), BF16),
                        pltpu.SemaphoreType.DMA((2, 3))],
    )
    return pl.pallas_call(
        functools.partial(_experts_kernel, layer=layer),
        grid_spec=grid_spec,
        out_shape=jax.ShapeDtypeStruct(xs.shape, jnp.uint32),
        compiler_params=_cparams(("arbitrary",)),
        name="moe_experts",
    )(block_e, n_used, first, slot.astype(jnp.int32), nxt.astype(jnp.int32), xs, w_in, w_out)


def _ffn_out_kernel(dest_ref, dest_next_ref, x_ref, gate_ref, h_ref, rg_ref, wg_ref, wu_ref, wo_ref, lng_ref,
                    lnb_ref, ys_ref, o_ref, dest_smem, rows_ref, sems, *, alpha, tm, n_steps):
    i = pl.program_id(0)
    slot = i % 2
    half = ROW_TILE * LANES

    def gather(tile_ref, into):
        load = pltpu.make_async_copy(tile_ref.at[0, 0], dest_smem, sems.at[0])
        load.start()
        load.wait()

        def issue(t, c):
            for k in range(TOP_K):
                d = dest_smem[k * tm + t]
                pltpu.make_async_copy(_token_tile(ys_ref, d), _token_tile(rows_ref.at[into, k], t),
                                      sems.at[1 + into]).start(priority=k % 2)
            return c

        lax.fori_loop(0, tm, issue, 0)

    @pl.when(i == 0)
    def _():
        gather(dest_ref, 0)

    @pl.when(i + 1 < n_steps)
    def _():
        gather(dest_next_ref, 1 - slot)

    lo, hi = _unpack_halves(_load_token_tiles(h_ref, tm))
    lo, hi = lo.astype(BF16), hi.astype(BF16)
    gate = _dot(lo, wg_ref[0:half, :]) + _dot(hi, wg_ref[half:, :])
    up = _dot(lo, wu_ref[0:half, :]) + _dot(hi, wu_ref[half:, :])
    a = (gate * _sigmoid(gate) * up).astype(BF16)
    y = _dot(a, wo_ref[...])
    y_lo, y_hi = y[:, :half], y[:, half:]

    rg = rg_ref[...]
    for k in range(TOP_K):
        pltpu.make_async_copy(ys_ref.at[pl.ds(0, tm * ROW_TILE)], rows_ref.at[slot, k], sems.at[1 + slot]).wait()
    for k in range(TOP_K):
        r_lo, r_hi = _unpack_halves(_load_token_tiles(rows_ref.at[slot, k], tm))
        y_lo = y_lo + r_lo * rg[:, k:k + 1]
        y_hi = y_hi + r_hi * rg[:, k:k + 1]
    z = alpha * x_ref[...] + gate_ref[0] * jnp.concatenate([y_lo, y_hi], axis=-1)
    o_ref[...] = _layer_norm(z, lng_ref[...], lnb_ref[...])


def _ffn_out(x, mod, gate_idx, h, ys, dest, route_gate, sh_in, sh_out, ln_g, ln_b, alpha, geom, tm=256):
    t, d = x.shape
    f = sh_out.shape[0]
    assert d == 2 * ROW_TILE * LANES
    n_steps = t // tm
    row = functools.partial(_mod_row, tm=tm, **geom)
    tiles = _dest_tiles(dest, tm)
    return pl.pallas_call(
        functools.partial(_ffn_out_kernel, alpha=alpha, tm=tm, n_steps=n_steps),
        grid=(n_steps,),
        in_specs=[pl.BlockSpec((1, 1, TOP_K * tm), lambda i: (i, 0, 0)),
                  pl.BlockSpec((1, 1, TOP_K * tm), lambda i: (jnp.minimum(i + 1, n_steps - 1), 0, 0)),
                  pl.BlockSpec((tm, d), lambda i: (i, 0)),
                  pl.BlockSpec((1, 1, d), lambda i: (row(i), 0, gate_idx)),
                  pl.BlockSpec((tm * ROW_TILE, LANES), lambda i: (i, 0)),
                  pl.BlockSpec((tm, TOP_K), lambda i: (i, 0)),
                  pl.BlockSpec((d, f), lambda i: (0, 0)),
                  pl.BlockSpec((d, f), lambda i: (0, 1)),
                  pl.BlockSpec((f, d), lambda i: (0, 0)),
                  pl.BlockSpec((1, d), lambda i: (0, 0)),
                  pl.BlockSpec((1, d), lambda i: (0, 0)),
                  pl.BlockSpec(memory_space=pl.ANY)],
        out_specs=pl.BlockSpec((tm, d), lambda i: (i, 0)),
        out_shape=jax.ShapeDtypeStruct((t, d), F32),
        scratch_shapes=[pltpu.SMEM((TOP_K * tm,), jnp.int32),
                        pltpu.VMEM((2, TOP_K, tm * ROW_TILE, LANES), jnp.uint32),
                        pltpu.SemaphoreType.DMA((3,))],
        compiler_params=_cparams(("arbitrary",)),
        name="ffn_out",
    )(tiles, tiles, x, mod, h, route_gate, sh_in, sh_in, sh_out, ln_g.reshape(1, d), ln_b.reshape(1, d), ys)


def _moe_sublayer(x, mod, geom, layer, router_w, router_bias, w_in, w_out, sh_in, sh_out, ln_g, ln_b, alpha):
    h, top_e, gate, rank, counts = _route(x, mod, 4, 3, router_w, router_bias, geom)
    n_blocks = x.shape[0] * TOP_K // MOE_BLOCK + N_EXPERTS
    dest, pad_ends, block_e, n_used = _dispatch_plan(top_e, rank, counts, n_blocks)
    xs = _dispatch(h, dest, pad_ends, n_used, n_blocks)
    ys = _experts(xs, block_e, n_used, w_in, w_out, layer)
    return _ffn_out(x, mod, 5, h, ys, dest, gate.T, sh_in.astype(BF16), sh_out.astype(BF16), ln_g, ln_b, alpha,
                    geom)


def kernel(x, c, ctx, c_ctx, ada_w, ada_b, ln_mix_g, ln_mix_b, ln_ffn_g, ln_ffn_b, ab_w_in, ab_w_out, diff_lambda, diff_subln_g, gqa_q_norm_g, gqa_k_norm_g, mla_w_down, mla_q_norm_g, mla_w_uq, mla_kv_norm_g, mla_w_ukv, mla_w_o, router_w, router_bias, expert_w_in, expert_w_out, shared_w_in, shared_w_out):
    bsz, seq, d = x.shape
    n_ctx = ctx.shape[1]
    depth = ada_w.shape[0]
    assert depth == 2 and seq % GRID_W == 0 and bsz + 1 <= MOD_ROWS
    alpha = (2 * depth) ** 0.25
    n_ctx_rows = bsz * n_ctx
    n_lat_rows = bsz * seq
    geom_all = dict(n_ctx_rows=n_ctx_rows, seq=seq, ctx_row=bsz)
    geom_lat = dict(n_ctx_rows=0, seq=seq, ctx_row=bsz)

    tables64 = _rope_tables(seq // GRID_W, 64)
    tables128 = _rope_tables(seq // GRID_W, HEAD_DIM)

    cvec = jnp.zeros((MOD_ROWS, d), F32).at[:bsz].set(c).at[bsz].set(c_ctx)
    mods = _ada_mod(cvec, ada_w, ada_b)
    mods = mods.reshape(depth, MOD_ROWS, 1, 6 * d)

    xt = jnp.concatenate([ctx.reshape(n_ctx_rows, d), x.reshape(n_lat_rows, d)], axis=0)

    mod = mods[0]
    n_diff = d // (2 * HEAD_DIM)
    n_gqa = d // (2 * HEAD_DIM)
    n_gqa_kv = n_gqa // 4
    proj = _modproj(xt, mod, 1, 0, ab_w_in[0].astype(BF16), BF16, geom_all)
    lam_init = 0.8 - 0.6 * math.exp(-0.3 * 0)
    oa = _diff_attn(proj, tables64, diff_lambda[0], diff_subln_g[0], lam_init, bsz, n_ctx, seq, n_diff)
    ob = _gqa_attn(proj, tables128, gqa_q_norm_g[0], gqa_k_norm_g[0], 3 * n_diff, bsz, n_ctx, seq, n_gqa, n_gqa_kv)
    w_out = ab_w_out[0].astype(BF16)
    wa, wb = w_out[:n_diff * HEAD_DIM], w_out[n_diff * HEAD_DIM:]
    t_all = n_ctx_rows + n_lat_rows
    xt = _resln(xt, mod, 2, [oa, ob], [wa, wb], ln_mix_g[0], ln_mix_b[0], alpha, geom_all, 0, t_all, 0)
    xt = _moe_sublayer(xt, mod, geom_all, 0, router_w[0], router_bias[0], expert_w_in, expert_w_out,
                       shared_w_in[0], shared_w_out[0], ln_ffn_g[0], ln_ffn_b[0], alpha)

    mod = mods[1]
    n_mla = d // HEAD_DIM
    qk = MLA_NOPE_DIM + MLA_ROPE_DIM
    w_down = jnp.pad(mla_w_down[0], ((0, 0), (0, LANES - MLA_ROPE_DIM))).astype(BF16)
    down = _modproj(xt, mod, 1, 0, w_down, F32, geom_all)
    w_uq = mla_w_uq[0].reshape(MLA_Q_RANK, n_mla, qk)
    w_uq_rope = jnp.pad(w_uq[:, :, MLA_NOPE_DIM:], ((0, 0), (0, 0), (0, LANES - MLA_ROPE_DIM)))
    w_uq = jnp.concatenate([w_uq[:, :, :MLA_NOPE_DIM].reshape(MLA_Q_RANK, -1),
                            w_uq_rope.reshape(MLA_Q_RANK, -1)], axis=1).astype(BF16)
    lat_blocks = n_ctx_rows // 1024
    q = _normproj(down, 0, mla_q_norm_g[0], w_uq, lat_blocks, n_lat_rows)
    kv = _normproj(down, 1, mla_kv_norm_g[0], mla_w_ukv[0].astype(BF16), 0, t_all)
    kpe_col = (MLA_Q_RANK + MLA_KV_RANK) // LANES
    o = _mla_attn(q, kv, down, tables64, bsz, n_ctx, seq, n_mla, kpe_col)
    xl = _resln(xt, mod, 2, [o], [mla_w_o[0].astype(BF16)], ln_mix_g[1], ln_mix_b[1], alpha, geom_all,
                n_ctx_rows // 512, n_lat_rows, 0)
    xl = _moe_sublayer(xl, mod, geom_lat, 1, router_w[1], router_bias[1], expert_w_in, expert_w_out,
                       shared_w_in[1], shared_w_out[1], ln_ffn_g[1], ln_ffn_b[1], alpha)
    return xl.reshape(bsz, seq, d)
```

```python
import functools
import math

import numpy as np
import jax
import jax.numpy as jnp
from jax import lax
from jax.experimental import pallas as pl
from jax.experimental.pallas import tpu as pltpu

F32 = jnp.float32
BF16 = jnp.bfloat16

GRID_W = 64
ROPE_THETA = 10000.0
HEAD_DIM = 128
DIFF_QK_DIM = 64
N_EXPERTS = 64
EXPERT_DIM = 512
TOP_K = 8
N_GROUPS = 8
TOPK_GROUPS = 4
ROUTE_SCALE = 2.5
MLA_Q_RANK = 512
MLA_KV_RANK = 512
MLA_NOPE_DIM = 128
MLA_ROPE_DIM = 64
MLA_V_DIM = 128
LANES = 128

MOD_ROWS = 16
VMEM_LIMIT = 56 * 1024 * 1024
MOE_BLOCK = 512


def _cparams(sem):
    return pltpu.CompilerParams(dimension_semantics=sem, vmem_limit_bytes=VMEM_LIMIT)


def _dot(a, b):
    return jnp.dot(a, b, preferred_element_type=F32)


def _dot_nt(a, b):
    return lax.dot_general(a, b, (((1,), (1,)), ((), ())), preferred_element_type=F32)


def _dot_hi(a, b):
    return lax.dot_general(a, b, (((1,), (0,)), ((), ())), precision=lax.Precision.HIGHEST,
                           preferred_element_type=F32)


def _sigmoid(x):
    return 1.0 / (1.0 + jnp.exp(-x))


def _rope(x, cos, sin_up, sin_dn, shift):
    return (x * cos + pltpu.roll(x, LANES - shift, 1) * sin_up + pltpu.roll(x, shift, 1) * sin_dn)


def _rope_tables(rows, rot_dim):
    t = jnp.arange(rows * GRID_W)
    row = (t // GRID_W).astype(F32)
    col = (t % GRID_W).astype(F32)
    axis_dim = rot_dim // 2
    quarter = rot_dim // 4
    inv_freq = 1.0 / (ROPE_THETA ** (jnp.arange(0, axis_dim, 2, dtype=F32) / axis_dim))
    lane = np.arange(LANES) % rot_dim
    is_col = lane >= axis_dim
    within = lane % axis_dim
    first = within < quarter
    freq = within % quarter
    ang = jnp.where(jnp.asarray(is_col)[None, :], col[:, None], row[:, None]) * inv_freq[freq][None, :]
    cos = jnp.cos(ang).astype(F32)
    sin = jnp.sin(ang).astype(F32)
    first = jnp.asarray(first)[None, :]
    sin_up = jnp.where(first, -sin, 0.0)
    sin_dn = jnp.where(first, 0.0, sin)
    return cos, sin_up, sin_dn, quarter


LOG2E = 1.4426950408889634
HEADS_PER_STEP = 4


def _exp2_rows(s):
    e = jnp.exp2(s - jnp.max(s, axis=-1, keepdims=True))
    return e, 1.0 / jnp.sum(e, axis=-1, keepdims=True)


def _rms(x, g, eps):
    return x * lax.rsqrt(jnp.mean(x * x, axis=-1, keepdims=True) + eps) * g


def _ada_kernel(c_ref, w_ref, b_ref, o_ref):
    c = c_ref[...]
    act = c * _sigmoid(c)
    o_ref[0] = _dot_hi(act, w_ref[0]) + b_ref[0]


def _ada_mod(cvec, ada_w, ada_b):
    depth, d, n = ada_w.shape
    tn = 1536
    return pl.pallas_call(
        _ada_kernel,
        grid=(depth, n // tn),
        in_specs=[pl.BlockSpec((MOD_ROWS, d), lambda l, j: (0, 0)),
                  pl.BlockSpec((1, d, tn), lambda l, j: (l, 0, j)),
                  pl.BlockSpec((1, 1, tn), lambda l, j: (l, 0, j))],
        out_specs=pl.BlockSpec((1, MOD_ROWS, tn), lambda l, j: (l, 0, j)),
        out_shape=jax.ShapeDtypeStruct((depth, MOD_ROWS, n), F32),
        compiler_params=_cparams(("arbitrary", "arbitrary")),
        name="ada_mod",
    )(cvec, ada_w, ada_b.reshape(depth, 1, n))


def _mod_row(i, tm, n_ctx_rows, seq, ctx_row):
    r0 = i * tm
    return jnp.where(r0 < n_ctx_rows, ctx_row, (r0 - n_ctx_rows) // seq)


def _modproj_kernel(x_ref, sc_ref, sh_ref, w_ref, o_ref, h_ref):
    @pl.when(pl.program_id(1) == 0)
    def _():
        h_ref[...] = (x_ref[...] * (1.0 + sc_ref[0]) + sh_ref[0]).astype(h_ref.dtype)

    o_ref[...] = _dot(h_ref[...], w_ref[...]).astype(o_ref.dtype)


def _modproj(x, mod, sc_idx, sh_idx, w, out_dtype, geom, tm=1024, tn=1536):
    t, d = x.shape
    n = w.shape[1]
    tn = tn if n % tn == 0 else n
    row = functools.partial(_mod_row, tm=tm, **geom)
    return pl.pallas_call(
        _modproj_kernel,
        grid=(t // tm, n // tn),
        in_specs=[pl.BlockSpec((tm, d), lambda i, j: (i, 0)),
                  pl.BlockSpec((1, 1, d), lambda i, j: (row(i), 0, sc_idx)),
                  pl.BlockSpec((1, 1, d), lambda i, j: (row(i), 0, sh_idx)),
                  pl.BlockSpec((d, tn), lambda i, j: (0, j))],
        out_specs=pl.BlockSpec((tm, tn), lambda i, j: (i, j)),
        out_shape=jax.ShapeDtypeStruct((t, n), out_dtype),
        scratch_shapes=[pltpu.VMEM((tm, d), BF16)],
        compiler_params=_cparams(("arbitrary", "arbitrary")),
        name="modproj",
    )(x, mod, mod, w)


def _normproj_kernel(x_ref, g_ref, w_ref, o_ref, h_ref):
    @pl.when(pl.program_id(1) == 0)
    def _():
        h_ref[...] = _rms(x_ref[...], g_ref[...], 1e-6).astype(h_ref.dtype)

    o_ref[...] = _dot(h_ref[...], w_ref[...]).astype(o_ref.dtype)


def _normproj(x, col_block, g, w, row_off_blocks, n_rows, tm=1024, tn=2048):
    k = w.shape[0]
    n = w.shape[1]
    return pl.pallas_call(
        _normproj_kernel,
        grid=(n_rows // tm, n // tn),
        in_specs=[pl.BlockSpec((tm, k), lambda i, j: (i + row_off_blocks, col_block)),
                  pl.BlockSpec((1, k), lambda i, j: (0, 0)),
                  pl.BlockSpec((k, tn), lambda i, j: (0, j))],
        out_specs=pl.BlockSpec((tm, tn), lambda i, j: (i, j)),
        out_shape=jax.ShapeDtypeStruct((n_rows, n), BF16),
        scratch_shapes=[pltpu.VMEM((tm, k), BF16)],
        compiler_params=_cparams(("arbitrary", "arbitrary")),
        name="normproj",
    )(x, g.reshape(1, k), w)


def _diff_lambda(lam_ref, lam_init):
    lp = lam_ref[...]
    s01 = jnp.sum(lp[0:1] * lp[1:2], axis=-1, keepdims=True)
    s23 = jnp.sum(lp[2:3] * lp[3:4], axis=-1, keepdims=True)
    return jnp.exp(s01) - jnp.exp(s23) + lam_init


def _diff_logits(q, k):
    lane = lax.broadcasted_iota(jnp.int32, q.shape, 1)
    qs = q * (DIFF_QK_DIM ** -0.5 * LOG2E)
    q1 = jnp.where(lane < DIFF_QK_DIM, qs, 0.0).astype(BF16)
    q2 = jnp.where(lane >= DIFF_QK_DIM, qs, 0.0).astype(BF16)
    return _dot_nt(q1, k), _dot_nt(q2, k)


def _diff_combine(s1, s2, v, lam, g, lam_init):
    e1, r1 = _exp2_rows(s1)
    e2, r2 = _exp2_rows(s2)
    a = (e1 - e2 * (lam * r2 / r1)).astype(BF16)
    return _rms(_dot(a, v) * r1, g, 1e-5) * (1.0 - lam_init)


def _head_cols(j):
    return slice(j * LANES, (j + 1) * LANES)


def _diff_attn_kernel(q_ref, kc_ref, kl_ref, vc_ref, vl_ref, cos_ref, su_ref, sd_ref, lam_ref, g_ref,
                      o_ref, k_scr, v_scr, *, n_ctx, tq, shift, lam_init):
    i = pl.program_id(2)
    heads = range(HEADS_PER_STEP)

    @pl.when(i == 0)
    def _prep():
        for j in heads:
            k_scr[j, 0:n_ctx, :] = kc_ref[:, _head_cols(j)]
            kl = kl_ref[:, _head_cols(j)].astype(F32)
            k_scr[j, n_ctx:, :] = _rope(kl, cos_ref[...], su_ref[...], sd_ref[...], shift).astype(BF16)
            v_scr[j, 0:n_ctx, :] = vc_ref[:, _head_cols(j)]
            v_scr[j, n_ctx:, :] = vl_ref[:, _head_cols(j)]

    lam = _diff_lambda(lam_ref, lam_init)
    g = g_ref[...]

    def attend(qs, n_keys):
        logits = [_diff_logits(qs[j], k_scr[j, 0:n_keys, :]) for j in heads]
        for j in heads:
            o = _diff_combine(*logits[j], v_scr[j, 0:n_keys, :], lam, g, lam_init)
            o_ref[:, _head_cols(j)] = o.astype(o_ref.dtype)

    @pl.when(i == 0)
    def _ctx_queries():
        attend([q_ref[:, _head_cols(j)].astype(F32) for j in heads], n_ctx)

    @pl.when(i > 0)
    def _lat_queries():
        r0 = pl.multiple_of((i - 1) * tq, tq)
        cos, su, sd = cos_ref[pl.ds(r0, tq), :], su_ref[pl.ds(r0, tq), :], sd_ref[pl.ds(r0, tq), :]
        attend([_rope(q_ref[:, _head_cols(j)].astype(F32), cos, su, sd, shift) for j in heads],
               k_scr.shape[1])


def _q_block(b, i, bsz, n_ctx, seq, tq):
    return jnp.where(i == 0, b * (n_ctx // tq), (bsz * n_ctx + b * seq) // tq + i - 1)


def _diff_attn(proj, tables, lam_params, subln_g, lam_init, bsz, n_ctx, seq, n_heads, tq=256):
    t = proj.shape[0]
    cos, su, sd, shift = tables
    hp = HEADS_PER_STEP
    wide = hp * LANES
    n_pairs = n_heads // hp
    qb = functools.partial(_q_block, bsz=bsz, n_ctx=n_ctx, seq=seq, tq=tq)
    lat0 = bsz * n_ctx // seq
    kern = functools.partial(_diff_attn_kernel, n_ctx=n_ctx, tq=tq, shift=shift, lam_init=lam_init)
    tab = pl.BlockSpec((seq, LANES), lambda b, h, i: (0, 0))
    return pl.pallas_call(
        kern,
        grid=(bsz, n_pairs, 1 + seq // tq),
        in_specs=[pl.BlockSpec((tq, wide), lambda b, h, i: (qb(b, i), h)),
                  pl.BlockSpec((n_ctx, wide), lambda b, h, i: (b, n_pairs + h)),
                  pl.BlockSpec((seq, wide), lambda b, h, i: (lat0 + b, n_pairs + h)),
                  pl.BlockSpec((n_ctx, wide), lambda b, h, i: (b, 2 * n_pairs + h)),
                  pl.BlockSpec((seq, wide), lambda b, h, i: (lat0 + b, 2 * n_pairs + h)),
                  tab, tab, tab,
                  pl.BlockSpec((4, DIFF_QK_DIM), lambda b, h, i: (0, 0)),
                  pl.BlockSpec((1, LANES), lambda b, h, i: (0, 0))],
        out_specs=pl.BlockSpec((tq, wide), lambda b, h, i: (qb(b, i), h)),
        out_shape=jax.ShapeDtypeStruct((t, n_heads * LANES), BF16),
        scratch_shapes=[pltpu.VMEM((hp, n_ctx + seq, LANES), BF16), pltpu.VMEM((hp, n_ctx + seq, LANES), BF16)],
        compiler_params=_cparams(("arbitrary", "arbitrary", "arbitrary")),
        name="diff_attn",
    )(proj, proj, proj, proj, proj, cos, su, sd, lam_params, subln_g.reshape(1, LANES))


def _softmax_pv(s, v):
    e, r = _exp2_rows(s)
    return _dot(e.astype(BF16), v) * r


def _gqa_attn_kernel(q_ref, kc_ref, kl_ref, vc_ref, vl_ref, cos_ref, su_ref, sd_ref, qg_ref, kg_ref,
                     o_ref, k_scr, v_scr, *, n_ctx, tq, shift):
    g = pl.program_id(2)
    i = pl.program_id(3)
    heads = range(HEADS_PER_STEP)

    @pl.when((g == 0) & (i == 0))
    def _prep():
        kg = kg_ref[...]
        k_scr[0:n_ctx, :] = _rms(kc_ref[...].astype(F32), kg, 1e-6).astype(BF16)
        kl = _rms(kl_ref[...].astype(F32), kg, 1e-6)
        k_scr[n_ctx:, :] = _rope(kl, cos_ref[...], su_ref[...], sd_ref[...], shift).astype(BF16)
        v_scr[0:n_ctx, :] = vc_ref[...]
        v_scr[n_ctx:, :] = vl_ref[...]

    scale = HEAD_DIM ** -0.5 * LOG2E

    def attend(qs, n_keys):
        logits = [_dot_nt((qs[j] * scale).astype(BF16), k_scr[0:n_keys, :]) for j in heads]
        for j in heads:
            o_ref[:, _head_cols(j)] = _softmax_pv(logits[j], v_scr[0:n_keys, :]).astype(o_ref.dtype)

    @pl.when(i == 0)
    def _ctx_queries():
        attend([_rms(q_ref[:, _head_cols(j)].astype(F32), qg_ref[...], 1e-6) for j in heads], n_ctx)

    @pl.when(i > 0)
    def _lat_queries():
        r0 = pl.multiple_of((i - 1) * tq, tq)
        cos, su, sd = cos_ref[pl.ds(r0, tq), :], su_ref[pl.ds(r0, tq), :], sd_ref[pl.ds(r0, tq), :]
        attend([_rope(_rms(q_ref[:, _head_cols(j)].astype(F32), qg_ref[...], 1e-6), cos, su, sd, shift)
                for j in heads], k_scr.shape[0])


def _gqa_attn(proj, tables, q_norm_g, k_norm_g, col0, bsz, n_ctx, seq, n_heads, n_kv, tq=256):
    t = proj.shape[0]
    cos, su, sd, shift = tables
    hp = HEADS_PER_STEP
    wide = hp * LANES
    pairs = n_heads // n_kv // hp
    assert col0 % hp == 0
    qb = functools.partial(_q_block, bsz=bsz, n_ctx=n_ctx, seq=seq, tq=tq)
    lat0 = bsz * n_ctx // seq
    kcol = col0 + n_heads
    vcol = kcol + n_kv
    kern = functools.partial(_gqa_attn_kernel, n_ctx=n_ctx, tq=tq, shift=shift)
    tab = pl.BlockSpec((seq, LANES), lambda b, h, g, i: (0, 0))
    vec = pl.BlockSpec((1, LANES), lambda b, h, g, i: (0, 0))
    return pl.pallas_call(
        kern,
        grid=(bsz, n_kv, pairs, 1 + seq // tq),
        in_specs=[pl.BlockSpec((tq, wide), lambda b, h, g, i: (qb(b, i), col0 // hp + h * pairs + g)),
                  pl.BlockSpec((n_ctx, LANES), lambda b, h, g, i: (b, kcol + h)),
                  pl.BlockSpec((seq, LANES), lambda b, h, g, i: (lat0 + b, kcol + h)),
                  pl.BlockSpec((n_ctx, LANES), lambda b, h, g, i: (b, vcol + h)),
                  pl.BlockSpec((seq, LANES), lambda b, h, g, i: (lat0 + b, vcol + h)),
                  tab, tab, tab, vec, vec],
        out_specs=pl.BlockSpec((tq, wide), lambda b, h, g, i: (qb(b, i), h * pairs + g)),
        out_shape=jax.ShapeDtypeStruct((t, n_heads * LANES), BF16),
        scratch_shapes=[pltpu.VMEM((n_ctx + seq, LANES), BF16), pltpu.VMEM((n_ctx + seq, LANES), BF16)],
        compiler_params=_cparams(("arbitrary", "arbitrary", "arbitrary", "arbitrary")),
        name="gqa_attn",
    )(proj, proj, proj, proj, proj, cos, su, sd, q_norm_g.reshape(1, LANES), k_norm_g.reshape(1, LANES))


def _mla_attn_kernel(qn_ref, qp_ref, kvc_ref, kvl_ref, kpc_ref, kpl_ref, cos_ref, su_ref, sd_ref,
                     o_ref, k_scr, v_scr, *, n_ctx, tq, shift):
    i = pl.program_id(2)

    @pl.when(i == 0)
    def _prep():
        kpc = kpc_ref[...].astype(BF16)
        kpl = _rope(kpl_ref[...], cos_ref[...], su_ref[...], sd_ref[...], shift).astype(BF16)
        for j in range(HEADS_PER_STEP):
            c0 = 2 * j * LANES
            k_scr[j, 0:n_ctx, 0:LANES] = kvc_ref[:, c0:c0 + LANES]
            k_scr[j, n_ctx:, 0:LANES] = kvl_ref[:, c0:c0 + LANES]
            k_scr[j, 0:n_ctx, LANES:] = kpc
            k_scr[j, n_ctx:, LANES:] = kpl
            v_scr[j, 0:n_ctx, :] = kvc_ref[:, c0 + LANES:c0 + 2 * LANES]
            v_scr[j, n_ctx:, :] = kvl_ref[:, c0 + LANES:c0 + 2 * LANES]

    scale = (MLA_NOPE_DIM + MLA_ROPE_DIM) ** -0.5 * LOG2E
    r0 = pl.multiple_of(i * tq, tq)
    cos, su, sd = cos_ref[pl.ds(r0, tq), :], su_ref[pl.ds(r0, tq), :], sd_ref[pl.ds(r0, tq), :]
    logits = []
    for j in range(HEADS_PER_STEP):
        qp = _rope(qp_ref[:, _head_cols(j)].astype(F32), cos, su, sd, shift)
        q = jnp.concatenate([(qn_ref[:, _head_cols(j)].astype(F32) * scale).astype(BF16), (qp * scale).astype(BF16)],
                            axis=-1)
        logits.append(_dot_nt(q, k_scr[j]))
    for j in range(HEADS_PER_STEP):
        o_ref[:, _head_cols(j)] = _softmax_pv(logits[j], v_scr[j]).astype(o_ref.dtype)


def _mla_attn(q, kv, down, tables, bsz, n_ctx, seq, n_heads, kpe_col, tq=256):
    cos, su, sd, shift = tables
    lat0 = bsz * n_ctx // seq
    nq = seq // tq
    hp = HEADS_PER_STEP
    kern = functools.partial(_mla_attn_kernel, n_ctx=n_ctx, tq=tq, shift=shift)
    tab = pl.BlockSpec((seq, LANES), lambda b, h, i: (0, 0))
    return pl.pallas_call(
        kern,
        grid=(bsz, n_heads // hp, nq),
        in_specs=[pl.BlockSpec((tq, hp * LANES), lambda b, h, i: (b * nq + i, h)),
                  pl.BlockSpec((tq, hp * LANES), lambda b, h, i: (b * nq + i, n_heads // hp + h)),
                  pl.BlockSpec((n_ctx, 2 * hp * LANES), lambda b, h, i: (b, h)),
                  pl.BlockSpec((seq, 2 * hp * LANES), lambda b, h, i: (lat0 + b, h)),
                  pl.BlockSpec((n_ctx, LANES), lambda b, h, i: (b, kpe_col)),
                  pl.BlockSpec((seq, LANES), lambda b, h, i: (lat0 + b, kpe_col)),
                  tab, tab, tab],
        out_specs=pl.BlockSpec((tq, hp * LANES), lambda b, h, i: (b * nq + i, h)),
        out_shape=jax.ShapeDtypeStruct((bsz * seq, n_heads * LANES), BF16),
        scratch_shapes=[pltpu.VMEM((hp, n_ctx + seq, 2 * LANES), BF16), pltpu.VMEM((hp, n_ctx + seq, LANES), BF16)],
        compiler_params=_cparams(("arbitrary", "arbitrary", "arbitrary")),
        name="mla_attn",
    )(q, q, kv, kv, down, down, cos, su, sd)


def _layer_norm(z, g, b):
    zc = z - jnp.mean(z, axis=-1, keepdims=True)
    var = jnp.mean(zc * zc, axis=-1, keepdims=True)
    return zc * lax.rsqrt(var + 1e-5) * g + b


def _resln_kernel(*refs, n_act, alpha):
    x_ref, gate_ref = refs[0], refs[1]
    acts = refs[2:2 + n_act]
    ws = refs[2 + n_act:2 + 2 * n_act]
    lng_ref, lnb_ref, o_ref = refs[2 + 2 * n_act:]
    y = _dot(acts[0][...], ws[0][...])
    for a_ref, w_ref in zip(acts[1:], ws[1:]):
        y = y + _dot(a_ref[...], w_ref[...])
    z = alpha * x_ref[...] + gate_ref[0] * y
    o_ref[...] = _layer_norm(z, lng_ref[...], lnb_ref[...])


def _resln(x, mod, gate_idx, acts, ws, ln_g, ln_b, alpha, geom, row_off_blocks, n_rows, act_off_blocks, tm=512):
    d = x.shape[1]
    row = functools.partial(_mod_row, tm=tm, **geom)
    n_act = len(acts)
    in_specs = [pl.BlockSpec((tm, d), lambda i: (i + row_off_blocks, 0)),
                pl.BlockSpec((1, 1, d), lambda i: (row(i + row_off_blocks), 0, gate_idx))]
    in_specs += [pl.BlockSpec((tm, a.shape[1]), lambda i: (i + act_off_blocks, 0)) for a in acts]
    in_specs += [pl.BlockSpec(w.shape, lambda i: (0, 0)) for w in ws]
    in_specs += [pl.BlockSpec((1, d), lambda i: (0, 0))] * 2
    return pl.pallas_call(
        functools.partial(_resln_kernel, n_act=n_act, alpha=alpha),
        grid=(n_rows // tm,),
        in_specs=in_specs,
        out_specs=pl.BlockSpec((tm, d), lambda i: (i, 0)),
        out_shape=jax.ShapeDtypeStruct((n_rows, d), F32),
        compiler_params=_cparams(("arbitrary",)),
        name="proj_resln",
    )(x, mod, *acts, *ws, ln_g.reshape(1, d), ln_b.reshape(1, d))


def _pack_halves(x):
    n = x.shape[1] // 2
    bits = lax.bitcast_convert_type(x.astype(BF16).astype(F32), jnp.uint32)
    return (bits[:, :n] >> 16) | (bits[:, n:] & jnp.uint32(0xFFFF0000))


def _unpack_halves(w):
    lo = lax.bitcast_convert_type(w << 16, F32)
    hi = lax.bitcast_convert_type(w & jnp.uint32(0xFFFF0000), F32)
    return lo, hi


ROW_TILE = 8


def _store_token_tiles(ref, packed):
    m = packed.shape[0]
    for s in range(ROW_TILE):
        ref[pl.ds(s, m, stride=ROW_TILE), :] = packed[:, s * LANES:(s + 1) * LANES]


def _load_token_tiles(ref, m):
    return jnp.concatenate([ref[pl.ds(s, m, stride=ROW_TILE), :] for s in range(ROW_TILE)], axis=-1)


def _token_tile(ref, tok):
    return ref.at[pl.ds(pl.multiple_of(tok * ROW_TILE, ROW_TILE), ROW_TILE)]


def _pick_first_max(cur, idx, axes, sentinel):
    m = cur
    for ax in axes:
        m = jnp.max(m, axis=ax, keepdims=True)
    first = jnp.where(cur == m, idx, sentinel)
    for ax in axes:
        first = jnp.min(first, axis=ax, keepdims=True)
    return m, first


def _route_kernel(x_ref, sc_ref, sh_ref, rwt_ref, bias_ref, tri_ref, h_ref, e_ref, g_ref, r_ref, cnt_ref, run_ref):
    tm = x_ref.shape[0]
    n_members = N_EXPERTS // N_GROUPS

    @pl.when(pl.program_id(0) == 0)
    def _():
        run_ref[...] = jnp.zeros_like(run_ref)

    h = x_ref[...] * (1.0 + sc_ref[0]) + sh_ref[0]
    _store_token_tiles(h_ref, _pack_halves(h))
    logits = lax.dot_general(rwt_ref[...], h, (((1,), (1,)), ((), ())), precision=lax.Precision.HIGHEST,
                             preferred_element_type=F32)
    scores = _sigmoid(logits).reshape(N_GROUPS, n_members, tm)
    biased = scores + bias_ref[...]
    neg = -jnp.inf
    member = lax.broadcasted_iota(jnp.int32, biased.shape, 1).astype(F32)
    group = lax.broadcasted_iota(jnp.int32, biased.shape, 0).astype(F32)
    expert = group * n_members + member

    m1, first = _pick_first_max(biased, member, (1,), float(n_members))
    m2 = jnp.max(jnp.where(member == first, neg, biased), axis=1, keepdims=True)
    gscore = m1 + m2
    gidx = lax.broadcasted_iota(jnp.int32, gscore.shape, 0).astype(F32)
    group_ok = jnp.zeros(gscore.shape, jnp.bool_)
    for _ in range(TOPK_GROUPS):
        _, first = _pick_first_max(gscore, gidx, (0,), float(N_GROUPS))
        pick = gidx == first
        group_ok = group_ok | pick
        gscore = jnp.where(pick, neg, gscore)

    cur = jnp.where(group_ok, biased, neg)
    chosen = jnp.zeros(biased.shape, jnp.bool_)
    top_e, gates = [], []
    for _ in range(TOP_K):
        _, first = _pick_first_max(cur, expert, (0, 1), float(N_EXPERTS))
        pick = expert == first
        chosen = chosen | pick
        cur = jnp.where(pick, neg, cur)
        top_e.append(first)
        gates.append(jnp.sum(jnp.sum(jnp.where(pick, scores, 0.0), axis=0, keepdims=True), axis=1, keepdims=True))
    gsum = gates[0]
    for gk in gates[1:]:
        gsum = gsum + gk
    norm = ROUTE_SCALE / gsum

    chosen2d = jnp.where(chosen, 1.0, 0.0).reshape(N_EXPERTS, tm)
    before = _dot(chosen2d.astype(BF16), tri_ref[...])
    rank = (run_ref[...] + before).reshape(N_GROUPS, n_members, tm)
    run_ref[...] = run_ref[...] + jnp.sum(chosen2d, axis=1, keepdims=True)
    cnt_ref[...] = run_ref[...].astype(jnp.int32)
    for k in range(TOP_K):
        pick = expert == top_e[k]
        rk = jnp.sum(jnp.sum(jnp.where(pick, rank, 0.0), axis=0, keepdims=True), axis=1, keepdims=True)
        e_ref[k:k + 1, :] = top_e[k].reshape(1, tm).astype(jnp.int32)
        g_ref[k:k + 1, :] = (gates[k] * norm).reshape(1, tm)
        r_ref[k:k + 1, :] = rk.reshape(1, tm).astype(jnp.int32)


def _route(x, mod, sc_idx, sh_idx, router_w, router_bias, geom, tm=512):
    t, d = x.shape
    row = functools.partial(_mod_row, tm=tm, **geom)
    tri = (np.arange(tm)[:, None] < np.arange(tm)[None, :]).astype(np.float32)
    kt = pl.BlockSpec((TOP_K, tm), lambda i: (0, i))
    return pl.pallas_call(
        _route_kernel,
        grid=(t // tm,),
        in_specs=[pl.BlockSpec((tm, d), lambda i: (i, 0)),
                  pl.BlockSpec((1, 1, d), lambda i: (row(i), 0, sc_idx)),
                  pl.BlockSpec((1, 1, d), lambda i: (row(i), 0, sh_idx)),
                  pl.BlockSpec((N_EXPERTS, d), lambda i: (0, 0)),
                  pl.BlockSpec((N_GROUPS, N_EXPERTS // N_GROUPS, 1), lambda i: (0, 0, 0)),
                  pl.BlockSpec((tm, tm), lambda i: (0, 0))],
        out_specs=[pl.BlockSpec((tm * ROW_TILE, LANES), lambda i: (i, 0)), kt, kt, kt,
                   pl.BlockSpec((N_EXPERTS, 1), lambda i: (0, 0))],
        out_shape=[jax.ShapeDtypeStruct((t * ROW_TILE, LANES), jnp.uint32),
                   jax.ShapeDtypeStruct((TOP_K, t), jnp.int32),
                   jax.ShapeDtypeStruct((TOP_K, t), F32),
                   jax.ShapeDtypeStruct((TOP_K, t), jnp.int32),
                   jax.ShapeDtypeStruct((N_EXPERTS, 1), jnp.int32)],
        scratch_shapes=[pltpu.VMEM((N_EXPERTS, 1), F32)],
        compiler_params=_cparams(("arbitrary",)),
        name="moe_route",
    )(x, mod, mod, router_w.T, router_bias.astype(F32).reshape(N_GROUPS, N_EXPERTS // N_GROUPS, 1),
      jnp.asarray(tri, BF16))


def _dispatch_plan(top_e, rank, counts, n_blocks):
    counts = counts.reshape(N_EXPERTS)
    padded = (counts + MOE_BLOCK - 1) // MOE_BLOCK * MOE_BLOCK
    pad_ends = jnp.cumsum(padded)
    pad_starts = pad_ends - padded
    onehot = top_e[:, :, None] == jnp.arange(N_EXPERTS, dtype=jnp.int32)
    dest = jnp.sum(jnp.where(onehot, pad_starts, 0), axis=-1) + rank
    block_start = jnp.arange(n_blocks, dtype=jnp.int32) * MOE_BLOCK
    block_e = jnp.minimum(jnp.sum(pad_ends[None, :] <= block_start[:, None], axis=1), N_EXPERTS - 1)
    n_used = (pad_ends[-1] // MOE_BLOCK).astype(jnp.int32).reshape(1)
    return dest, pad_ends.astype(jnp.int32), block_e.astype(jnp.int32), n_used


def _dest_tiles(dest, tm):
    k, t = dest.shape
    return dest.reshape(k, t // tm, tm).transpose(1, 0, 2).reshape(t // tm, 1, k * tm)


def _dispatch_kernel(pe_ref, nu_ref, dest_ref, h_ref, xs_ref, dest_smem, zero_ref, sems, *, tm, n_blocks):
    i = pl.program_id(0)

    def zero_fill(slot0):
        rows = MOE_BLOCK * ROW_TILE
        row0 = slot0 * ROW_TILE if isinstance(slot0, int) else pl.multiple_of(slot0 * ROW_TILE, rows)
        return pltpu.make_async_copy(zero_ref, xs_ref.at[pl.ds(row0, rows)], sems.at[1])

    @pl.when(i == 0)
    def _():
        zero_ref[...] = jnp.zeros_like(zero_ref)
        for e in range(N_EXPERTS):
            start = pe_ref[e - 1] if e else 0

            @pl.when(pe_ref[e] > start)
            def _():
                zero_fill(pe_ref[e] - MOE_BLOCK).start()

        def start_unused(b, c):
            zero_fill(b * MOE_BLOCK).start()
            return c

        lax.fori_loop(nu_ref[0], n_blocks, start_unused, 0)
        for e in range(N_EXPERTS):
            start = pe_ref[e - 1] if e else 0

            @pl.when(pe_ref[e] > start)
            def _():
                zero_fill(0).wait()

        def wait_unused(b, c):
            zero_fill(0).wait()
            return c

        lax.fori_loop(nu_ref[0], n_blocks, wait_unused, 0)

    load = pltpu.make_async_copy(dest_ref.at[0, 0], dest_smem, sems.at[0])
    load.start()
    load.wait()

    def issue(t, c):
        for k in range(TOP_K):
            d = dest_smem[k * tm + t]
            pltpu.make_async_copy(_token_tile(h_ref, t), _token_tile(xs_ref, d), sems.at[2]).start(priority=k % 2)
        return c

    lax.fori_loop(0, tm, issue, 0)
    for k in range(TOP_K):
        pltpu.make_async_copy(h_ref, xs_ref.at[pl.ds(0, tm * ROW_TILE)], sems.at[2]).wait()


def _dispatch(h, dest, pad_ends, n_used, n_blocks, tm=512):
    t = h.shape[0] // ROW_TILE
    grid_spec = pltpu.PrefetchScalarGridSpec(
        num_scalar_prefetch=2,
        grid=(t // tm,),
        in_specs=[pl.BlockSpec((1, 1, TOP_K * tm), lambda i, pe, nu: (i, 0, 0)),
                  pl.BlockSpec((tm * ROW_TILE, LANES), lambda i, pe, nu: (i, 0))],
        out_specs=pl.BlockSpec(memory_space=pl.ANY),
        scratch_shapes=[pltpu.SMEM((TOP_K * tm,), jnp.int32),
                        pltpu.VMEM((MOE_BLOCK * ROW_TILE, LANES), jnp.uint32),
                        pltpu.SemaphoreType.DMA((3,))],
    )
    return pl.pallas_call(
        functools.partial(_dispatch_kernel, tm=tm, n_blocks=n_blocks),
        grid_spec=grid_spec,
        out_shape=jax.ShapeDtypeStruct((n_blocks * MOE_BLOCK * ROW_TILE, LANES), jnp.uint32),
        compiler_params=_cparams(("arbitrary",)),
        name="moe_dispatch",
    )(pad_ends, n_used, _dest_tiles(dest, tm), h)


def _experts_kernel(be_ref, nu_ref, first_ref, slot_ref, next_ref, x_ref, wi_ref, wo_ref, o_ref,
                    gbuf, ubuf, obuf, wg_scr, wu_scr, wo_scr, sems, *, layer):
    blk = pl.program_id(0)
    half = ROW_TILE * LANES
    f = wg_scr.shape[1]

    def weight_copies(e, slot):
        return (pltpu.make_async_copy(wi_ref.at[layer, e, :, pl.ds(0, f)], gbuf.at[slot], sems.at[slot, 0]),
                pltpu.make_async_copy(wi_ref.at[layer, e, :, pl.ds(f, f)], ubuf.at[slot], sems.at[slot, 1]),
                pltpu.make_async_copy(wo_ref.at[layer, e], obuf.at[slot], sems.at[slot, 2]))

    @pl.when(blk == 0)
    def _():
        for c in weight_copies(be_ref[0], 0):
            c.start()

    @pl.when(first_ref[blk] == 1)
    def _():
        slot = slot_ref[blk]
        for c in weight_copies(be_ref[blk], slot):
            c.wait()
        wg_scr[...] = gbuf[slot].astype(BF16)
        wu_scr[...] = ubuf[slot].astype(BF16)
        wo_scr[...] = obuf[slot].astype(BF16)

        @pl.when(next_ref[blk] >= 0)
        def _():
            for c in weight_copies(next_ref[blk], 1 - slot):
                c.start()

    @pl.when(blk < nu_ref[0])
    def _():
        lo, hi = _unpack_halves(_load_token_tiles(x_ref, MOE_BLOCK))
        lo, hi = lo.astype(BF16), hi.astype(BF16)
        gate = _dot(lo, wg_scr[0:half, :]) + _dot(hi, wg_scr[half:, :])
        up = _dot(lo, wu_scr[0:half, :]) + _dot(hi, wu_scr[half:, :])
        a = (gate * _sigmoid(gate) * up).astype(BF16)
        _store_token_tiles(o_ref, _pack_halves(_dot(a, wo_scr[...])))

    @pl.when(blk >= nu_ref[0])
    def _():
        o_ref[...] = jnp.zeros_like(o_ref)


def _experts(xs, block_e, n_used, w_in, w_out, layer):
    d = w_in.shape[2]
    f = w_out.shape[2]
    assert d == 2 * ROW_TILE * LANES
    n_blocks = xs.shape[0] // (MOE_BLOCK * ROW_TILE)
    first = jnp.concatenate([jnp.ones((1,), jnp.int32), (block_e[1:] != block_e[:-1]).astype(jnp.int32)])
    slot = (jnp.cumsum(first) - 1) % 2
    later = jnp.where(block_e[None, :] > block_e[:, None], block_e[None, :], N_EXPERTS)
    nxt = jnp.min(later, axis=1)
    nxt = jnp.where(nxt >= N_EXPERTS, -1, nxt)
    grid_spec = pltpu.PrefetchScalarGridSpec(
        num_scalar_prefetch=5,
        grid=(n_blocks,),
        in_specs=[pl.BlockSpec((MOE_BLOCK * ROW_TILE, LANES), lambda i, *_: (i, 0)),
                  pl.BlockSpec(memory_space=pl.ANY),
                  pl.BlockSpec(memory_space=pl.ANY)],
        out_specs=pl.BlockSpec((MOE_BLOCK * ROW_TILE, LANES), lambda i, *_: (i, 0)),
        scratch_shapes=[pltpu.VMEM((2, d, f), F32), pltpu.VMEM((2, d, f), F32), pltpu.VMEM((2, f, d), F32),
                        pltpu.VMEM((d, f), BF16), pltpu.VMEM((d, f), BF16), pltpu.VMEM((f, d), BF16),
                        pltpu.SemaphoreType.DMA((2, 3))],
    )
    return pl.pallas_call(
        functools.partial(_experts_kernel, layer=layer),
        grid_spec=grid_spec,
        out_shape=jax.ShapeDtypeStruct(xs.shape, jnp.uint32),
        compiler_params=_cparams(("arbitrary",)),
        name="moe_experts",
    )(block_e, n_used, first, slot.astype(jnp.int32), nxt.astype(jnp.int32), xs, w_in, w_out)


def _ffn_out_kernel(dest_ref, dest_next_ref, x_ref, gate_ref, h_ref, rg_ref, wg_ref, wu_ref, wo_ref, lng_ref,
                    lnb_ref, ys_ref, o_ref, dest_smem, rows_ref, sems, *, alpha, tm, n_steps):
    i = pl.program_id(0)
    slot = i % 2
    half = ROW_TILE * LANES

    def gather(tile_ref, into):
        load = pltpu.make_async_copy(tile_ref.at[0, 0], dest_smem, sems.at[0])
        load.start()
        load.wait()

        def issue(t, c):
            for k in range(TOP_K):
                d = dest_smem[k * tm + t]
                pltpu.make_async_copy(_token_tile(ys_ref, d), _token_tile(rows_ref.at[into, k], t),
                                      sems.at[1 + into]).start(priority=k % 2)
            return c

        lax.fori_loop(0, tm, issue, 0)

    @pl.when(i == 0)
    def _():
        gather(dest_ref, 0)

    @pl.when(i + 1 < n_steps)
    def _():
        gather(dest_next_ref, 1 - slot)

    lo, hi = _unpack_halves(_load_token_tiles(h_ref, tm))
    lo, hi = lo.astype(BF16), hi.astype(BF16)
    gate = _dot(lo, wg_ref[0:half, :]) + _dot(hi, wg_ref[half:, :])
    up = _dot(lo, wu_ref[0:half, :]) + _dot(hi, wu_ref[half:, :])
    a = (gate * _sigmoid(gate) * up).astype(BF16)
    y = _dot(a, wo_ref[...])
    y_lo, y_hi = y[:, :half], y[:, half:]

    rg = rg_ref[...]
    for k in range(TOP_K):
        pltpu.make_async_copy(ys_ref.at[pl.ds(0, tm * ROW_TILE)], rows_ref.at[slot, k], sems.at[1 + slot]).wait()
    for k in range(TOP_K):
        r_lo, r_hi = _unpack_halves(_load_token_tiles(rows_ref.at[slot, k], tm))
        y_lo = y_lo + r_lo * rg[:, k:k + 1]
        y_hi = y_hi + r_hi * rg[:, k:k + 1]
    z = alpha * x_ref[...] + gate_ref[0] * jnp.concatenate([y_lo, y_hi], axis=-1)
    o_ref[...] = _layer_norm(z, lng_ref[...], lnb_ref[...])


def _ffn_out(x, mod, gate_idx, h, ys, dest, route_gate, sh_in, sh_out, ln_g, ln_b, alpha, geom, tm=256):
    t, d = x.shape
    f = sh_out.shape[0]
    assert d == 2 * ROW_TILE * LANES
    n_steps = t // tm
    row = functools.partial(_mod_row, tm=tm, **geom)
    tiles = _dest_tiles(dest, tm)
    return pl.pallas_call(
        functools.partial(_ffn_out_kernel, alpha=alpha, tm=tm, n_steps=n_steps),
        grid=(n_steps,),
        in_specs=[pl.BlockSpec((1, 1, TOP_K * tm), lambda i: (i, 0, 0)),
                  pl.BlockSpec((1, 1, TOP_K * tm), lambda i: (jnp.minimum(i + 1, n_steps - 1), 0, 0)),
                  pl.BlockSpec((tm, d), lambda i: (i, 0)),
                  pl.BlockSpec((1, 1, d), lambda i: (row(i), 0, gate_idx)),
                  pl.BlockSpec((tm * ROW_TILE, LANES), lambda i: (i, 0)),
                  pl.BlockSpec((tm, TOP_K), lambda i: (i, 0)),
                  pl.BlockSpec((d, f), lambda i: (0, 0)),
                  pl.BlockSpec((d, f), lambda i: (0, 1)),
                  pl.BlockSpec((f, d), lambda i: (0, 0)),
                  pl.BlockSpec((1, d), lambda i: (0, 0)),
                  pl.BlockSpec((1, d), lambda i: (0, 0)),
                  pl.BlockSpec(memory_space=pl.ANY)],
        out_specs=pl.BlockSpec((tm, d), lambda i: (i, 0)),
        out_shape=jax.ShapeDtypeStruct((t, d), F32),
        scratch_shapes=[pltpu.SMEM((TOP_K * tm,), jnp.int32),
                        pltpu.VMEM((2, TOP_K, tm * ROW_TILE, LANES), jnp.uint32),
                        pltpu.SemaphoreType.DMA((3,))],
        compiler_params=_cparams(("arbitrary",)),
        name="ffn_out",
    )(tiles, tiles, x, mod, h, route_gate, sh_in, sh_in, sh_out, ln_g.reshape(1, d), ln_b.reshape(1, d), ys)


def _moe_sublayer(x, mod, geom, layer, router_w, router_bias, w_in, w_out, sh_in, sh_out, ln_g, ln_b, alpha):
    h, top_e, gate, rank, counts = _route(x, mod, 4, 3, router_w, router_bias, geom)
    n_blocks = x.shape[0] * TOP_K // MOE_BLOCK + N_EXPERTS
    dest, pad_ends, block_e, n_used = _dispatch_plan(top_e, rank, counts, n_blocks)
    xs = _dispatch(h, dest, pad_ends, n_used, n_blocks)
    ys = _experts(xs, block_e, n_used, w_in, w_out, layer)
    return _ffn_out(x, mod, 5, h, ys, dest, gate.T, sh_in.astype(BF16), sh_out.astype(BF16), ln_g, ln_b, alpha,
                    geom)


def kernel(x, c, ctx, c_ctx, ada_w, ada_b, ln_mix_g, ln_mix_b, ln_ffn_g, ln_ffn_b, ab_w_in, ab_w_out, diff_lambda, diff_subln_g, gqa_q_norm_g, gqa_k_norm_g, mla_w_down, mla_q_norm_g, mla_w_uq, mla_kv_norm_g, mla_w_ukv, mla_w_o, router_w, router_bias, expert_w_in, expert_w_out, shared_w_in, shared_w_out):
    bsz, seq, d = x.shape
    n_ctx = ctx.shape[1]
    depth = ada_w.shape[0]
    assert depth == 2 and seq % GRID_W == 0 and bsz + 1 <= MOD_ROWS
    alpha = (2 * depth) ** 0.25
    n_ctx_rows = bsz * n_ctx
    n_lat_rows = bsz * seq
    geom_all = dict(n_ctx_rows=n_ctx_rows, seq=seq, ctx_row=bsz)
    geom_lat = dict(n_ctx_rows=0, seq=seq, ctx_row=bsz)

    tables64 = _rope_tables(seq // GRID_W, 64)
    tables128 = _rope_tables(seq // GRID_W, HEAD_DIM)

    cvec = jnp.zeros((MOD_ROWS, d), F32).at[:bsz].set(c).at[bsz].set(c_ctx)
    mods = _ada_mod(cvec, ada_w, ada_b)
    mods = mods.reshape(depth, MOD_ROWS, 1, 6 * d)

    xt = jnp.concatenate([ctx.reshape(n_ctx_rows, d), x.reshape(n_lat_rows, d)], axis=0)

    mod = mods[0]
    n_diff = d // (2 * HEAD_DIM)
    n_gqa = d // (2 * HEAD_DIM)
    n_gqa_kv = n_gqa // 4
    proj = _modproj(xt, mod, 1, 0, ab_w_in[0].astype(BF16), BF16, geom_all)
    lam_init = 0.8 - 0.6 * math.exp(-0.3 * 0)
    oa = _diff_attn(proj, tables64, diff_lambda[0], diff_subln_g[0], lam_init, bsz, n_ctx, seq, n_diff)
    ob = _gqa_attn(proj, tables128, gqa_q_norm_g[0], gqa_k_norm_g[0], 3 * n_diff, bsz, n_ctx, seq, n_gqa, n_gqa_kv)
    w_out = ab_w_out[0].astype(BF16)
    wa, wb = w_out[:n_diff * HEAD_DIM], w_out[n_diff * HEAD_DIM:]
    t_all = n_ctx_rows + n_lat_rows
    xt = _resln(xt, mod, 2, [oa, ob], [wa, wb], ln_mix_g[0], ln_mix_b[0], alpha, geom_all, 0, t_all, 0)
    xt = _moe_sublayer(xt, mod, geom_all, 0, router_w[0], router_bias[0], expert_w_in, expert_w_out,
                       shared_w_in[0], shared_w_out[0], ln_ffn_g[0], ln_ffn_b[0], alpha)

    mod = mods[1]
    n_mla = d // HEAD_DIM
    qk = MLA_NOPE_DIM + MLA_ROPE_DIM
    w_down = jnp.pad(mla_w_down[0].astype(BF16), ((0, 0), (0, LANES - MLA_ROPE_DIM)))
    down = _modproj(xt, mod, 1, 0, w_down, F32, geom_all)
    w_uq = mla_w_uq[0].reshape(MLA_Q_RANK, n_mla, qk)
    w_uq_rope = jnp.pad(w_uq[:, :, MLA_NOPE_DIM:], ((0, 0), (0, 0), (0, LANES - MLA_ROPE_DIM)))
    w_uq = jnp.concatenate([w_uq[:, :, :MLA_NOPE_DIM].reshape(MLA_Q_RANK, -1),
                            w_uq_rope.reshape(MLA_Q_RANK, -1)], axis=1).astype(BF16)
    lat_blocks = n_ctx_rows // 1024
    q = _normproj(down, 0, mla_q_norm_g[0], w_uq, lat_blocks, n_lat_rows)
    kv = _normproj(down, 1, mla_kv_norm_g[0], mla_w_ukv[0].astype(BF16), 0, t_all)
    kpe_col = (MLA_Q_RANK + MLA_KV_RANK) // LANES
    o = _mla_attn(q, kv, down, tables64, bsz, n_ctx, seq, n_mla, kpe_col)
    xl = _resln(xt, mod, 2, [o], [mla_w_o[0].astype(BF16)], ln_mix_g[1], ln_mix_b[1], alpha, geom_all,
                n_ctx_rows // 512, n_lat_rows, 0)
    xl = _moe_sublayer(xl, mod, geom_lat, 1, router_w[1], router_bias[1], expert_w_in, expert_w_out,
                       shared_w_in[1], shared_w_out[1], ln_ffn_g[1], ln_ffn_b[1], alpha)
    return xl.reshape(bsz, seq, d)
```

```python
import functools
import math

import numpy as np
import jax
import jax.numpy as jnp
from jax import lax
from jax.experimental import pallas as pl
from jax.experimental.pallas import tpu as pltpu

F32 = jnp.float32
BF16 = jnp.bfloat16

GRID_W = 64
ROPE_THETA = 10000.0
HEAD_DIM = 128
DIFF_QK_DIM = 64
N_EXPERTS = 64
EXPERT_DIM = 512
TOP_K = 8
N_GROUPS = 8
TOPK_GROUPS = 4
ROUTE_SCALE = 2.5
MLA_Q_RANK = 512
MLA_KV_RANK = 512
MLA_NOPE_DIM = 128
MLA_ROPE_DIM = 64
MLA_V_DIM = 128
LANES = 128

MOD_ROWS = 16
VMEM_LIMIT = 56 * 1024 * 1024
MOE_BLOCK = 512


def _cparams(sem):
    return pltpu.CompilerParams(dimension_semantics=sem, vmem_limit_bytes=VMEM_LIMIT)


def _dot(a, b):
    return jnp.dot(a, b, preferred_element_type=F32)


def _dot_nt(a, b):
    return lax.dot_general(a, b, (((1,), (1,)), ((), ())), preferred_element_type=F32)


def _dot_hi(a, b):
    return lax.dot_general(a, b, (((1,), (0,)), ((), ())), precision=lax.Precision.HIGHEST,
                           preferred_element_type=F32)


def _sigmoid(x):
    return 1.0 / (1.0 + jnp.exp(-x))


def _rope(x, cos, sin_up, sin_dn, shift):
    return (x * cos + pltpu.roll(x, LANES - shift, 1) * sin_up + pltpu.roll(x, shift, 1) * sin_dn)


def _rope_tables(rows, rot_dim):
    t = jnp.arange(rows * GRID_W)
    row = (t // GRID_W).astype(F32)
    col = (t % GRID_W).astype(F32)
    axis_dim = rot_dim // 2
    quarter = rot_dim // 4
    inv_freq = 1.0 / (ROPE_THETA ** (jnp.arange(0, axis_dim, 2, dtype=F32) / axis_dim))
    lane = np.arange(LANES) % rot_dim
    is_col = lane >= axis_dim
    within = lane % axis_dim
    first = within < quarter
    freq = within % quarter
    ang = jnp.where(jnp.asarray(is_col)[None, :], col[:, None], row[:, None]) * inv_freq[freq][None, :]
    cos = jnp.cos(ang).astype(F32)
    sin = jnp.sin(ang).astype(F32)
    first = jnp.asarray(first)[None, :]
    sin_up = jnp.where(first, -sin, 0.0)
    sin_dn = jnp.where(first, 0.0, sin)
    return cos, sin_up, sin_dn, quarter


LOG2E = 1.4426950408889634
HEADS_PER_STEP = 4


def _exp2_rows(s):
    e = jnp.exp2(s - jnp.max(s, axis=-1, keepdims=True))
    return e, 1.0 / jnp.sum(e, axis=-1, keepdims=True)


def _rms(x, g, eps):
    return x * lax.rsqrt(jnp.mean(x * x, axis=-1, keepdims=True) + eps) * g


def _ada_kernel(c_ref, w_ref, b_ref, o_ref):
    c = c_ref[...]
    act = c * _sigmoid(c)
    o_ref[0] = _dot_hi(act, w_ref[0]) + b_ref[0]


def _ada_mod(cvec, ada_w, ada_b):
    depth, d, n = ada_w.shape
    tn = 1536
    return pl.pallas_call(
        _ada_kernel,
        grid=(depth, n // tn),
        in_specs=[pl.BlockSpec((MOD_ROWS, d), lambda l, j: (0, 0)),
                  pl.BlockSpec((1, d, tn), lambda l, j: (l, 0, j)),
                  pl.BlockSpec((1, 1, tn), lambda l, j: (l, 0, j))],
        out_specs=pl.BlockSpec((1, MOD_ROWS, tn), lambda l, j: (l, 0, j)),
        out_shape=jax.ShapeDtypeStruct((depth, MOD_ROWS, n), F32),
        compiler_params=_cparams(("arbitrary", "arbitrary")),
        name="ada_mod",
    )(cvec, ada_w, ada_b.reshape(depth, 1, n))


def _mod_row(i, tm, n_ctx_rows, seq, ctx_row):
    r0 = i * tm
    return jnp.where(r0 < n_ctx_rows, ctx_row, (r0 - n_ctx_rows) // seq)


def _modproj_kernel(x_ref, sc_ref, sh_ref, w_ref, o_ref, h_ref):
    @pl.when(pl.program_id(1) == 0)
    def _():
        h_ref[...] = (x_ref[...] * (1.0 + sc_ref[0]) + sh_ref[0]).astype(h_ref.dtype)

    o_ref[...] = _dot(h_ref[...], w_ref[...]).astype(o_ref.dtype)


def _modproj(x, mod, sc_idx, sh_idx, w, out_dtype, geom, tm=1024, tn=1536):
    t, d = x.shape
    n = w.shape[1]
    tn = tn if n % tn == 0 else n
    row = functools.partial(_mod_row, tm=tm, **geom)
    return pl.pallas_call(
        _modproj_kernel,
        grid=(t // tm, n // tn),
        in_specs=[pl.BlockSpec((tm, d), lambda i, j: (i, 0)),
                  pl.BlockSpec((1, 1, d), lambda i, j: (row(i), 0, sc_idx)),
                  pl.BlockSpec((1, 1, d), lambda i, j: (row(i), 0, sh_idx)),
                  pl.BlockSpec((d, tn), lambda i, j: (0, j))],
        out_specs=pl.BlockSpec((tm, tn), lambda i, j: (i, j)),
        out_shape=jax.ShapeDtypeStruct((t, n), out_dtype),
        scratch_shapes=[pltpu.VMEM((tm, d), BF16)],
        compiler_params=_cparams(("arbitrary", "arbitrary")),
        name="modproj",
    )(x, mod, mod, w)


def _normproj_kernel(x_ref, g_ref, w_ref, o_ref, h_ref):
    @pl.when(pl.program_id(1) == 0)
    def _():
        h_ref[...] = _rms(x_ref[...], g_ref[...], 1e-6).astype(h_ref.dtype)

    o_ref[...] = _dot(h_ref[...], w_ref[...]).astype(o_ref.dtype)


def _normproj(x, col_block, g, w, row_off_blocks, n_rows, tm=1024, tn=2048):
    k = w.shape[0]
    n = w.shape[1]
    return pl.pallas_call(
        _normproj_kernel,
        grid=(n_rows // tm, n // tn),
        in_specs=[pl.BlockSpec((tm, k), lambda i, j: (i + row_off_blocks, col_block)),
                  pl.BlockSpec((1, k), lambda i, j: (0, 0)),
                  pl.BlockSpec((k, tn), lambda i, j: (0, j))],
        out_specs=pl.BlockSpec((tm, tn), lambda i, j: (i, j)),
        out_shape=jax.ShapeDtypeStruct((n_rows, n), BF16),
        scratch_shapes=[pltpu.VMEM((tm, k), BF16)],
        compiler_params=_cparams(("arbitrary", "arbitrary")),
        name="normproj",
    )(x, g.reshape(1, k), w)


def _diff_lambda(lam_ref, lam_init):
    lp = lam_ref[...]
    s01 = jnp.sum(lp[0:1] * lp[1:2], axis=-1, keepdims=True)
    s23 = jnp.sum(lp[2:3] * lp[3:4], axis=-1, keepdims=True)
    return jnp.exp(s01) - jnp.exp(s23) + lam_init


def _diff_logits(q, k):
    lane = lax.broadcasted_iota(jnp.int32, q.shape, 1)
    qs = q * (DIFF_QK_DIM ** -0.5 * LOG2E)
    q1 = jnp.where(lane < DIFF_QK_DIM, qs, 0.0).astype(BF16)
    q2 = jnp.where(lane >= DIFF_QK_DIM, qs, 0.0).astype(BF16)
    return _dot_nt(q1, k), _dot_nt(q2, k)


def _diff_combine(s1, s2, v, lam, g, lam_init):
    e1, r1 = _exp2_rows(s1)
    e2, r2 = _exp2_rows(s2)
    a = (e1 - e2 * (lam * r2 / r1)).astype(BF16)
    return _rms(_dot(a, v) * r1, g, 1e-5) * (1.0 - lam_init)


def _head_cols(j):
    return slice(j * LANES, (j + 1) * LANES)


def _diff_attn_kernel(q_ref, kc_ref, kl_ref, vc_ref, vl_ref, cos_ref, su_ref, sd_ref, lam_ref, g_ref,
                      o_ref, k_scr, v_scr, *, n_ctx, tq, shift, lam_init):
    i = pl.program_id(2)
    heads = range(HEADS_PER_STEP)

    @pl.when(i == 0)
    def _prep():
        for j in heads:
            k_scr[j, 0:n_ctx, :] = kc_ref[:, _head_cols(j)]
            kl = kl_ref[:, _head_cols(j)].astype(F32)
            k_scr[j, n_ctx:, :] = _rope(kl, cos_ref[...], su_ref[...], sd_ref[...], shift).astype(BF16)
            v_scr[j, 0:n_ctx, :] = vc_ref[:, _head_cols(j)]
            v_scr[j, n_ctx:, :] = vl_ref[:, _head_cols(j)]

    lam = _diff_lambda(lam_ref, lam_init)
    g = g_ref[...]

    def attend(qs, n_keys):
        logits = [_diff_logits(qs[j], k_scr[j, 0:n_keys, :]) for j in heads]
        for j in heads:
            o = _diff_combine(*logits[j], v_scr[j, 0:n_keys, :], lam, g, lam_init)
            o_ref[:, _head_cols(j)] = o.astype(o_ref.dtype)

    @pl.when(i == 0)
    def _ctx_queries():
        attend([q_ref[:, _head_cols(j)].astype(F32) for j in heads], n_ctx)

    @pl.when(i > 0)
    def _lat_queries():
        r0 = pl.multiple_of((i - 1) * tq, tq)
        cos, su, sd = cos_ref[pl.ds(r0, tq), :], su_ref[pl.ds(r0, tq), :], sd_ref[pl.ds(r0, tq), :]
        attend([_rope(q_ref[:, _head_cols(j)].astype(F32), cos, su, sd, shift) for j in heads],
               k_scr.shape[1])


def _q_block(b, i, bsz, n_ctx, seq, tq):
    return jnp.where(i == 0, b * (n_ctx // tq), (bsz * n_ctx + b * seq) // tq + i - 1)


def _diff_attn(proj, tables, lam_params, subln_g, lam_init, bsz, n_ctx, seq, n_heads, tq=256):
    t = proj.shape[0]
    cos, su, sd, shift = tables
    hp = HEADS_PER_STEP
    wide = hp * LANES
    n_pairs = n_heads // hp
    qb = functools.partial(_q_block, bsz=bsz, n_ctx=n_ctx, seq=seq, tq=tq)
    lat0 = bsz * n_ctx // seq
    kern = functools.partial(_diff_attn_kernel, n_ctx=n_ctx, tq=tq, shift=shift, lam_init=lam_init)
    tab = pl.BlockSpec((seq, LANES), lambda b, h, i: (0, 0))
    return pl.pallas_call(
        kern,
        grid=(bsz, n_pairs, 1 + seq // tq),
        in_specs=[pl.BlockSpec((tq, wide), lambda b, h, i: (qb(b, i), h)),
                  pl.BlockSpec((n_ctx, wide), lambda b, h, i: (b, n_pairs + h)),
                  pl.BlockSpec((seq, wide), lambda b, h, i: (lat0 + b, n_pairs + h)),
                  pl.BlockSpec((n_ctx, wide), lambda b, h, i: (b, 2 * n_pairs + h)),
                  pl.BlockSpec((seq, wide), lambda b, h, i: (lat0 + b, 2 * n_pairs + h)),
                  tab, tab, tab,
                  pl.BlockSpec((4, DIFF_QK_DIM), lambda b, h, i: (0, 0)),
                  pl.BlockSpec((1, LANES), lambda b, h, i: (0, 0))],
        out_specs=pl.BlockSpec((tq, wide), lambda b, h, i: (qb(b, i), h)),
        out_shape=jax.ShapeDtypeStruct((t, n_heads * LANES), BF16),
        scratch_shapes=[pltpu.VMEM((hp, n_ctx + seq, LANES), BF16), pltpu.VMEM((hp, n_ctx + seq, LANES), BF16)],
        compiler_params=_cparams(("arbitrary", "arbitrary", "arbitrary")),
        name="diff_attn",
    )(proj, proj, proj, proj, proj, cos, su, sd, lam_params, subln_g.reshape(1, LANES))


def _softmax_pv(s, v):
    e, r = _exp2_rows(s)
    return _dot(e.astype(BF16), v) * r


def _gqa_attn_kernel(q_ref, kc_ref, kl_ref, vc_ref, vl_ref, cos_ref, su_ref, sd_ref, qg_ref, kg_ref,
                     o_ref, k_scr, v_scr, *, n_ctx, tq, shift):
    g = pl.program_id(2)
    i = pl.program_id(3)
    heads = range(HEADS_PER_STEP)

    @pl.when((g == 0) & (i == 0))
    def _prep():
        kg = kg_ref[...]
        k_scr[0:n_ctx, :] = _rms(kc_ref[...].astype(F32), kg, 1e-6).astype(BF16)
        kl = _rms(kl_ref[...].astype(F32), kg, 1e-6)
        k_scr[n_ctx:, :] = _rope(kl, cos_ref[...], su_ref[...], sd_ref[...], shift).astype(BF16)
        v_scr[0:n_ctx, :] = vc_ref[...]
        v_scr[n_ctx:, :] = vl_ref[...]

    scale = HEAD_DIM ** -0.5 * LOG2E

    def attend(qs, n_keys):
        logits = [_dot_nt((qs[j] * scale).astype(BF16), k_scr[0:n_keys, :]) for j in heads]
        for j in heads:
            o_ref[:, _head_cols(j)] = _softmax_pv(logits[j], v_scr[0:n_keys, :]).astype(o_ref.dtype)

    @pl.when(i == 0)
    def _ctx_queries():
        attend([_rms(q_ref[:, _head_cols(j)].astype(F32), qg_ref[...], 1e-6) for j in heads], n_ctx)

    @pl.when(i > 0)
    def _lat_queries():
        r0 = pl.multiple_of((i - 1) * tq, tq)
        cos, su, sd = cos_ref[pl.ds(r0, tq), :], su_ref[pl.ds(r0, tq), :], sd_ref[pl.ds(r0, tq), :]
        attend([_rope(_rms(q_ref[:, _head_cols(j)].astype(F32), qg_ref[...], 1e-6), cos, su, sd, shift)
                for j in heads], k_scr.shape[0])


def _gqa_attn(proj, tables, q_norm_g, k_norm_g, col0, bsz, n_ctx, seq, n_heads, n_kv, tq=256):
    t = proj.shape[0]
    cos, su, sd, shift = tables
    hp = HEADS_PER_STEP
    wide = hp * LANES
    pairs = n_heads // n_kv // hp
    assert col0 % hp == 0
    qb = functools.partial(_q_block, bsz=bsz, n_ctx=n_ctx, seq=seq, tq=tq)
    lat0 = bsz * n_ctx // seq
    kcol = col0 + n_heads
    vcol = kcol + n_kv
    kern = functools.partial(_gqa_attn_kernel, n_ctx=n_ctx, tq=tq, shift=shift)
    tab = pl.BlockSpec((seq, LANES), lambda b, h, g, i: (0, 0))
    vec = pl.BlockSpec((1, LANES), lambda b, h, g, i: (0, 0))
    return pl.pallas_call(
        kern,
        grid=(bsz, n_kv, pairs, 1 + seq // tq),
        in_specs=[pl.BlockSpec((tq, wide), lambda b, h, g, i: (qb(b, i), col0 // hp + h * pairs + g)),
                  pl.BlockSpec((n_ctx, LANES), lambda b, h, g, i: (b, kcol + h)),
                  pl.BlockSpec((seq, LANES), lambda b, h, g, i: (lat0 + b, kcol + h)),
                  pl.BlockSpec((n_ctx, LANES), lambda b, h, g, i: (b, vcol + h)),
                  pl.BlockSpec((seq, LANES), lambda b, h, g, i: (lat0 + b, vcol + h)),
                  tab, tab, tab, vec, vec],
        out_specs=pl.BlockSpec((tq, wide), lambda b, h, g, i: (qb(b, i), h * pairs + g)),
        out_shape=jax.ShapeDtypeStruct((t, n_heads * LANES), BF16),
        scratch_shapes=[pltpu.VMEM((n_ctx + seq, LANES), BF16), pltpu.VMEM((n_ctx + seq, LANES), BF16)],
        compiler_params=_cparams(("arbitrary", "arbitrary", "arbitrary", "arbitrary")),
        name="gqa_attn",
    )(proj, proj, proj, proj, proj, cos, su, sd, q_norm_g.reshape(1, LANES), k_norm_g.reshape(1, LANES))


def _mla_attn_kernel(qn_ref, qp_ref, kvc_ref, kvl_ref, kpc_ref, kpl_ref, cos_ref, su_ref, sd_ref,
                     o_ref, k_scr, v_scr, *, n_ctx, tq, shift):
    i = pl.program_id(2)

    @pl.when(i == 0)
    def _prep():
        kpc = kpc_ref[...].astype(BF16)
        kpl = _rope(kpl_ref[...], cos_ref[...], su_ref[...], sd_ref[...], shift).astype(BF16)
        for j in range(HEADS_PER_STEP):
            c0 = 2 * j * LANES
            k_scr[j, 0:n_ctx, 0:LANES] = kvc_ref[:, c0:c0 + LANES]
            k_scr[j, n_ctx:, 0:LANES] = kvl_ref[:, c0:c0 + LANES]
            k_scr[j, 0:n_ctx, LANES:] = kpc
            k_scr[j, n_ctx:, LANES:] = kpl
            v_scr[j, 0:n_ctx, :] = kvc_ref[:, c0 + LANES:c0 + 2 * LANES]
            v_scr[j, n_ctx:, :] = kvl_ref[:, c0 + LANES:c0 + 2 * LANES]

    scale = (MLA_NOPE_DIM + MLA_ROPE_DIM) ** -0.5 * LOG2E
    r0 = pl.multiple_of(i * tq, tq)
    cos, su, sd = cos_ref[pl.ds(r0, tq), :], su_ref[pl.ds(r0, tq), :], sd_ref[pl.ds(r0, tq), :]
    logits = []
    for j in range(HEADS_PER_STEP):
        qp = _rope(qp_ref[:, _head_cols(j)].astype(F32), cos, su, sd, shift)
        q = jnp.concatenate([(qn_ref[:, _head_cols(j)].astype(F32) * scale).astype(BF16), (qp * scale).astype(BF16)],
                            axis=-1)
        logits.append(_dot_nt(q, k_scr[j]))
    for j in range(HEADS_PER_STEP):
        o_ref[:, _head_cols(j)] = _softmax_pv(logits[j], v_scr[j]).astype(o_ref.dtype)


def _mla_attn(q, kv, down, tables, bsz, n_ctx, seq, n_heads, kpe_col, tq=256):
    cos, su, sd, shift = tables
    lat0 = bsz * n_ctx // seq
    nq = seq // tq
    hp = HEADS_PER_STEP
    kern = functools.partial(_mla_attn_kernel, n_ctx=n_ctx, tq=tq, shift=shift)
    tab = pl.BlockSpec((seq, LANES), lambda b, h, i: (0, 0))
    return pl.pallas_call(
        kern,
        grid=(bsz, n_heads // hp, nq),
        in_specs=[pl.BlockSpec((tq, hp * LANES), lambda b, h, i: (b * nq + i, h)),
                  pl.BlockSpec((tq, hp * LANES), lambda b, h, i: (b * nq + i, n_heads // hp + h)),
                  pl.BlockSpec((n_ctx, 2 * hp * LANES), lambda b, h, i: (b, h)),
                  pl.BlockSpec((seq, 2 * hp * LANES), lambda b, h, i: (lat0 + b, h)),
                  pl.BlockSpec((n_ctx, LANES), lambda b, h, i: (b, kpe_col)),
                  pl.BlockSpec((seq, LANES), lambda b, h, i: (lat0 + b, kpe_col)),
                  tab, tab, tab],
        out_specs=pl.BlockSpec((tq, hp * LANES), lambda b, h, i: (b * nq + i, h)),
        out_shape=jax.ShapeDtypeStruct((bsz * seq, n_heads * LANES), BF16),
        scratch_shapes=[pltpu.VMEM((hp, n_ctx + seq, 2 * LANES), BF16), pltpu.VMEM((hp, n_ctx + seq, LANES), BF16)],
        compiler_params=_cparams(("arbitrary", "arbitrary", "arbitrary")),
        name="mla_attn",
    )(q, q, kv, kv, down, down, cos, su, sd)


def _layer_norm(z, g, b):
    zc = z - jnp.mean(z, axis=-1, keepdims=True)
    var = jnp.mean(zc * zc, axis=-1, keepdims=True)
    return zc * lax.rsqrt(var + 1e-5) * g + b


def _resln_kernel(*refs, n_act, alpha):
    x_ref, gate_ref = refs[0], refs[1]
    acts = refs[2:2 + n_act]
    ws = refs[2 + n_act:2 + 2 * n_act]
    lng_ref, lnb_ref, o_ref = refs[2 + 2 * n_act:]
    y = _dot(acts[0][...], ws[0][...])
    for a_ref, w_ref in zip(acts[1:], ws[1:]):
        y = y + _dot(a_ref[...], w_ref[...])
    z = alpha * x_ref[...] + gate_ref[0] * y
    o_ref[...] = _layer_norm(z, lng_ref[...], lnb_ref[...])


def _resln(x, mod, gate_idx, acts, ws, ln_g, ln_b, alpha, geom, row_off_blocks, n_rows, act_off_blocks, tm=512):
    d = x.shape[1]
    row = functools.partial(_mod_row, tm=tm, **geom)
    n_act = len(acts)
    in_specs = [pl.BlockSpec((tm, d), lambda i: (i + row_off_blocks, 0)),
                pl.BlockSpec((1, 1, d), lambda i: (row(i + row_off_blocks), 0, gate_idx))]
    in_specs += [pl.BlockSpec((tm, a.shape[1]), lambda i: (i + act_off_blocks, 0)) for a in acts]
    in_specs += [pl.BlockSpec(w.shape, lambda i: (0, 0)) for w in ws]
    in_specs += [pl.BlockSpec((1, d), lambda i: (0, 0))] * 2
    return pl.pallas_call(
        functools.partial(_resln_kernel, n_act=n_act, alpha=alpha),
        grid=(n_rows // tm,),
        in_specs=in_specs,
        out_specs=pl.BlockSpec((tm, d), lambda i: (i, 0)),
        out_shape=jax.ShapeDtypeStruct((n_rows, d), F32),
        compiler_params=_cparams(("arbitrary",)),
        name="proj_resln",
    )(x, mod, *acts, *ws, ln_g.reshape(1, d), ln_b.reshape(1, d))


def _pack_halves(x):
    n = x.shape[1] // 2
    bits = lax.bitcast_convert_type(x.astype(BF16).astype(F32), jnp.uint32)
    return (bits[:, :n] >> 16) | (bits[:, n:] & jnp.uint32(0xFFFF0000))


def _unpack_halves(w):
    lo = lax.bitcast_convert_type(w << 16, F32)
    hi = lax.bitcast_convert_type(w & jnp.uint32(0xFFFF0000), F32)
    return lo, hi


ROW_TILE = 8


def _store_token_tiles(ref, packed):
    m = packed.shape[0]
    for s in range(ROW_TILE):
        ref[pl.ds(s, m, stride=ROW_TILE), :] = packed[:, s * LANES:(s + 1) * LANES]


def _load_token_tiles(ref, m):
    return jnp.concatenate([ref[pl.ds(s, m, stride=ROW_TILE), :] for s in range(ROW_TILE)], axis=-1)


def _token_tile(ref, tok):
    return ref.at[pl.ds(pl.multiple_of(tok * ROW_TILE, ROW_TILE), ROW_TILE)]


def _pick_first_max(cur, idx, axes, sentinel):
    m = cur
    for ax in axes:
        m = jnp.max(m, axis=ax, keepdims=True)
    first = jnp.where(cur == m, idx, sentinel)
    for ax in axes:
        first = jnp.min(first, axis=ax, keepdims=True)
    return m, first


def _route_kernel(x_ref, sc_ref, sh_ref, rwt_ref, bias_ref, tri_ref, h_ref, e_ref, g_ref, r_ref, cnt_ref, run_ref):
    tm = x_ref.shape[0]
    n_members = N_EXPERTS // N_GROUPS

    @pl.when(pl.program_id(0) == 0)
    def _():
        run_ref[...] = jnp.zeros_like(run_ref)

    h = x_ref[...] * (1.0 + sc_ref[0]) + sh_ref[0]
    _store_token_tiles(h_ref, _pack_halves(h))
    h_hi = h.astype(BF16)
    h_lo = (h - h_hi.astype(F32)).astype(BF16)
    w = rwt_ref[...]
    w_hi = w.astype(BF16)
    w_lo = (w - w_hi.astype(F32)).astype(BF16)
    logits = _dot_nt(w_hi, h_hi) + (_dot_nt(w_hi, h_lo) + _dot_nt(w_lo, h_hi))
    scores = _sigmoid(logits).reshape(N_GROUPS, n_members, tm)
    biased = scores + bias_ref[...]
    neg = -jnp.inf
    member = lax.broadcasted_iota(jnp.int32, biased.shape, 1).astype(F32)
    group = lax.broadcasted_iota(jnp.int32, biased.shape, 0).astype(F32)
    expert = group * n_members + member

    m1, first = _pick_first_max(biased, member, (1,), float(n_members))
    m2 = jnp.max(jnp.where(member == first, neg, biased), axis=1, keepdims=True)
    gscore = m1 + m2
    gidx = lax.broadcasted_iota(jnp.int32, gscore.shape, 0).astype(F32)
    group_ok = jnp.zeros(gscore.shape, jnp.bool_)
    for _ in range(TOPK_GROUPS):
        _, first = _pick_first_max(gscore, gidx, (0,), float(N_GROUPS))
        pick = gidx == first
        group_ok = group_ok | pick
        gscore = jnp.where(pick, neg, gscore)

    cur = jnp.where(group_ok, biased, neg)
    chosen = jnp.zeros(biased.shape, jnp.bool_)
    top_e, gates = [], []
    for _ in range(TOP_K):
        _, first = _pick_first_max(cur, expert, (0, 1), float(N_EXPERTS))
        pick = expert == first
        chosen = chosen | pick
        cur = jnp.where(pick, neg, cur)
        top_e.append(first)
        gates.append(jnp.sum(jnp.sum(jnp.where(pick, scores, 0.0), axis=0, keepdims=True), axis=1, keepdims=True))
    gsum = gates[0]
    for gk in gates[1:]:
        gsum = gsum + gk
    norm = ROUTE_SCALE / gsum

    chosen2d = jnp.where(chosen, 1.0, 0.0).reshape(N_EXPERTS, tm)
    before = _dot(chosen2d.astype(BF16), tri_ref[...])
    rank = (run_ref[...] + before).reshape(N_GROUPS, n_members, tm)
    run_ref[...] = run_ref[...] + jnp.sum(chosen2d, axis=1, keepdims=True)
    cnt_ref[...] = run_ref[...].astype(jnp.int32)
    for k in range(TOP_K):
        pick = expert == top_e[k]
        rk = jnp.sum(jnp.sum(jnp.where(pick, rank, 0.0), axis=0, keepdims=True), axis=1, keepdims=True)
        e_ref[k:k + 1, :] = top_e[k].reshape(1, tm).astype(jnp.int32)
        g_ref[k:k + 1, :] = (gates[k] * norm).reshape(1, tm)
        r_ref[k:k + 1, :] = rk.reshape(1, tm).astype(jnp.int32)


def _route(x, mod, sc_idx, sh_idx, router_w, router_bias, geom, tm=512):
    t, d = x.shape
    row = functools.partial(_mod_row, tm=tm, **geom)
    tri = (np.arange(tm)[:, None] < np.arange(tm)[None, :]).astype(np.float32)
    kt = pl.BlockSpec((TOP_K, tm), lambda i: (0, i))
    return pl.pallas_call(
        _route_kernel,
        grid=(t // tm,),
        in_specs=[pl.BlockSpec((tm, d), lambda i: (i, 0)),
                  pl.BlockSpec((1, 1, d), lambda i: (row(i), 0, sc_idx)),
                  pl.BlockSpec((1, 1, d), lambda i: (row(i), 0, sh_idx)),
                  pl.BlockSpec((N_EXPERTS, d), lambda i: (0, 0)),
                  pl.BlockSpec((N_GROUPS, N_EXPERTS // N_GROUPS, 1), lambda i: (0, 0, 0)),
                  pl.BlockSpec((tm, tm), lambda i: (0, 0))],
        out_specs=[pl.BlockSpec((tm * ROW_TILE, LANES), lambda i: (i, 0)), kt, kt, kt,
                   pl.BlockSpec((N_EXPERTS, 1), lambda i: (0, 0))],
        out_shape=[jax.ShapeDtypeStruct((t * ROW_TILE, LANES), jnp.uint32),
                   jax.ShapeDtypeStruct((TOP_K, t), jnp.int32),
                   jax.ShapeDtypeStruct((TOP_K, t), F32),
                   jax.ShapeDtypeStruct((TOP_K, t), jnp.int32),
                   jax.ShapeDtypeStruct((N_EXPERTS, 1), jnp.int32)],
        scratch_shapes=[pltpu.VMEM((N_EXPERTS, 1), F32)],
        compiler_params=_cparams(("arbitrary",)),
        name="moe_route",
    )(x, mod, mod, router_w.T, router_bias.astype(F32).reshape(N_GROUPS, N_EXPERTS // N_GROUPS, 1),
      jnp.asarray(tri, BF16))


def _dispatch_plan(top_e, rank, counts, n_blocks):
    counts = counts.reshape(N_EXPERTS)
    padded = (counts + MOE_BLOCK - 1) // MOE_BLOCK * MOE_BLOCK
    pad_ends = jnp.cumsum(padded)
    pad_starts = pad_ends - padded
    onehot = top_e[:, :, None] == jnp.arange(N_EXPERTS, dtype=jnp.int32)
    dest = jnp.sum(jnp.where(onehot, pad_starts, 0), axis=-1) + rank
    block_start = jnp.arange(n_blocks, dtype=jnp.int32) * MOE_BLOCK
    block_e = jnp.minimum(jnp.sum(pad_ends[None, :] <= block_start[:, None], axis=1), N_EXPERTS - 1)
    n_used = (pad_ends[-1] // MOE_BLOCK).astype(jnp.int32).reshape(1)
    return dest, pad_ends.astype(jnp.int32), block_e.astype(jnp.int32), n_used


def _dest_tiles(dest, tm):
    k, t = dest.shape
    return dest.reshape(k, t // tm, tm).transpose(1, 0, 2).reshape(t // tm, 1, k * tm)


def _dispatch_kernel(pe_ref, nu_ref, dest_ref, h_ref, xs_ref, dest_smem, zero_ref, sems, *, tm, n_blocks):
    i = pl.program_id(0)

    def zero_fill(slot0):
        rows = MOE_BLOCK * ROW_TILE
        row0 = slot0 * ROW_TILE if isinstance(slot0, int) else pl.multiple_of(slot0 * ROW_TILE, rows)
        return pltpu.make_async_copy(zero_ref, xs_ref.at[pl.ds(row0, rows)], sems.at[1])

    @pl.when(i == 0)
    def _():
        zero_ref[...] = jnp.zeros_like(zero_ref)
        for e in range(N_EXPERTS):
            start = pe_ref[e - 1] if e else 0

            @pl.when(pe_ref[e] > start)
            def _():
                zero_fill(pe_ref[e] - MOE_BLOCK).start()

        def start_unused(b, c):
            zero_fill(b * MOE_BLOCK).start()
            return c

        lax.fori_loop(nu_ref[0], n_blocks, start_unused, 0)
        for e in range(N_EXPERTS):
            start = pe_ref[e - 1] if e else 0

            @pl.when(pe_ref[e] > start)
            def _():
                zero_fill(0).wait()

        def wait_unused(b, c):
            zero_fill(0).wait()
            return c

        lax.fori_loop(nu_ref[0], n_blocks, wait_unused, 0)

    load = pltpu.make_async_copy(dest_ref.at[0, 0], dest_smem, sems.at[0])
    load.start()
    load.wait()

    def issue(t, c):
        for k in range(TOP_K):
            d = dest_smem[k * tm + t]
            pltpu.make_async_copy(_token_tile(h_ref, t), _token_tile(xs_ref, d), sems.at[2]).start(priority=k % 2)
        return c

    lax.fori_loop(0, tm, issue, 0)
    for k in range(TOP_K):
        pltpu.make_async_copy(h_ref, xs_ref.at[pl.ds(0, tm * ROW_TILE)], sems.at[2]).wait()


def _dispatch(h, dest, pad_ends, n_used, n_blocks, tm=512):
    t = h.shape[0] // ROW_TILE
    grid_spec = pltpu.PrefetchScalarGridSpec(
        num_scalar_prefetch=2,
        grid=(t // tm,),
        in_specs=[pl.BlockSpec((1, 1, TOP_K * tm), lambda i, pe, nu: (i, 0, 0)),
                  pl.BlockSpec((tm * ROW_TILE, LANES), lambda i, pe, nu: (i, 0))],
        out_specs=pl.BlockSpec(memory_space=pl.ANY),
        scratch_shapes=[pltpu.SMEM((TOP_K * tm,), jnp.int32),
                        pltpu.VMEM((MOE_BLOCK * ROW_TILE, LANES), jnp.uint32),
                        pltpu.SemaphoreType.DMA((3,))],
    )
    return pl.pallas_call(
        functools.partial(_dispatch_kernel, tm=tm, n_blocks=n_blocks),
        grid_spec=grid_spec,
        out_shape=jax.ShapeDtypeStruct((n_blocks * MOE_BLOCK * ROW_TILE, LANES), jnp.uint32),
        compiler_params=_cparams(("arbitrary",)),
        name="moe_dispatch",
    )(pad_ends, n_used, _dest_tiles(dest, tm), h)


def _experts_kernel(be_ref, nu_ref, first_ref, slot_ref, next_ref, x_ref, wi_ref, wo_ref, o_ref,
                    gbuf, ubuf, obuf, wg_scr, wu_scr, wo_scr, sems, *, layer):
    blk = pl.program_id(0)
    half = ROW_TILE * LANES
    f = wg_scr.shape[1]

    def weight_copies(e, slot):
        return (pltpu.make_async_copy(wi_ref.at[layer, e, :, pl.ds(0, f)], gbuf.at[slot], sems.at[slot, 0]),
                pltpu.make_async_copy(wi_ref.at[layer, e, :, pl.ds(f, f)], ubuf.at[slot], sems.at[slot, 1]),
                pltpu.make_async_copy(wo_ref.at[layer, e], obuf.at[slot], sems.at[slot, 2]))

    @pl.when(blk == 0)
    def _():
        for c in weight_copies(be_ref[0], 0):
            c.start()

    @pl.when(first_ref[blk] == 1)
    def _():
        slot = slot_ref[blk]
        for c in weight_copies(be_ref[blk], slot):
            c.wait()
        wg_scr[...] = gbuf[slot].astype(BF16)
        wu_scr[...] = ubuf[slot].astype(BF16)
        wo_scr[...] = obuf[slot].astype(BF16)

        @pl.when(next_ref[blk] >= 0)
        def _():
            for c in weight_copies(next_ref[blk], 1 - slot):
                c.start()

    @pl.when(blk < nu_ref[0])
    def _():
        lo, hi = _unpack_halves(_load_token_tiles(x_ref, MOE_BLOCK))
        lo, hi = lo.astype(BF16), hi.astype(BF16)
        gate = _dot(lo, wg_scr[0:half, :]) + _dot(hi, wg_scr[half:, :])
        up = _dot(lo, wu_scr[0:half, :]) + _dot(hi, wu_scr[half:, :])
        a = (gate * _sigmoid(gate) * up).astype(BF16)
        _store_token_tiles(o_ref, _pack_halves(_dot(a, wo_scr[...])))

    @pl.when(blk >= nu_ref[0])
    def _():
        o_ref[...] = jnp.zeros_like(o_ref)


def _experts(xs, block_e, n_used, w_in, w_out, layer):
    d = w_in.shape[2]
    f = w_out.shape[2]
    assert d == 2 * ROW_TILE * LANES
    n_blocks = xs.shape[0] // (MOE_BLOCK * ROW_TILE)
    first = jnp.concatenate([jnp.ones((1,), jnp.int32), (block_e[1:] != block_e[:-1]).astype(jnp.int32)])
    slot = (jnp.cumsum(first) - 1) % 2
    later = jnp.where(block_e[None, :] > block_e[:, None], block_e[None, :], N_EXPERTS)
    nxt = jnp.min(later, axis=1)
    nxt = jnp.where(nxt >= N_EXPERTS, -1, nxt)
    grid_spec = pltpu.PrefetchScalarGridSpec(
        num_scalar_prefetch=5,
        grid=(n_blocks,),
        in_specs=[pl.BlockSpec((MOE_BLOCK * ROW_TILE, LANES), lambda i, *_: (i, 0)),
                  pl.BlockSpec(memory_space=pl.ANY),
                  pl.BlockSpec(memory_space=pl.ANY)],
        out_specs=pl.BlockSpec((MOE_BLOCK * ROW_TILE, LANES), lambda i, *_: (i, 0)),
        scratch_shapes=[pltpu.VMEM((2, d, f), F32), pltpu.VMEM((2, d, f), F32), pltpu.VMEM((2, f, d), F32),
                        pltpu.VMEM((d, f), BF16), pltpu.VMEM((d, f), BF16), pltpu.VMEM((f, d), BF16),
                        pltpu.SemaphoreType.DMA((2, 3))],
    )
    return pl.pallas_call(
        functools.partial(_experts_kernel, layer=layer),
        grid_spec=grid_spec,
        out_shape=jax.ShapeDtypeStruct(xs.shape, jnp.uint32),
        compiler_params=_cparams(("arbitrary",)),
        name="moe_experts",
    )(block_e, n_used, first, slot.astype(jnp.int32), nxt.astype(jnp.int32), xs, w_in, w_out)


def _ffn_out_kernel(dest_ref, dest_next_ref, x_ref, gate_ref, h_ref, rg_ref, wg_ref, wu_ref, wo_ref, lng_ref,
                    lnb_ref, ys_ref, o_ref, dest_smem, rows_ref, sems, *, alpha, tm, n_steps):
    i = pl.program_id(0)
    slot = i % 2
    half = ROW_TILE * LANES

    def gather(tile_ref, into):
        load = pltpu.make_async_copy(tile_ref.at[0, 0], dest_smem, sems.at[0])
        load.start()
        load.wait()

        def issue(t, c):
            for k in range(TOP_K):
                d = dest_smem[k * tm + t]
                pltpu.make_async_copy(_token_tile(ys_ref, d), _token_tile(rows_ref.at[into, k], t),
                                      sems.at[1 + into]).start(priority=k % 2)
            return c

        lax.fori_loop(0, tm, issue, 0)

    @pl.when(i == 0)
    def _():
        gather(dest_ref, 0)

    @pl.when(i + 1 < n_steps)
    def _():
        gather(dest_next_ref, 1 - slot)

    lo, hi = _unpack_halves(_load_token_tiles(h_ref, tm))
    lo, hi = lo.astype(BF16), hi.astype(BF16)
    gate = _dot(lo, wg_ref[0:half, :]) + _dot(hi, wg_ref[half:, :])
    up = _dot(lo, wu_ref[0:half, :]) + _dot(hi, wu_ref[half:, :])
    a = (gate * _sigmoid(gate) * up).astype(BF16)
    y = _dot(a, wo_ref[...])
    y_lo, y_hi = y[:, :half], y[:, half:]

    rg = rg_ref[...]
    for k in range(TOP_K):
        pltpu.make_async_copy(ys_ref.at[pl.ds(0, tm * ROW_TILE)], rows_ref.at[slot, k], sems.at[1 + slot]).wait()
    for k in range(TOP_K):
        r_lo, r_hi = _unpack_halves(_load_token_tiles(rows_ref.at[slot, k], tm))
        y_lo = y_lo + r_lo * rg[:, k:k + 1]
        y_hi = y_hi + r_hi * rg[:, k:k + 1]
    z = alpha * x_ref[...] + gate_ref[0] * jnp.concatenate([y_lo, y_hi], axis=-1)
    o_ref[...] = _layer_norm(z, lng_ref[...], lnb_ref[...])


def _ffn_out(x, mod, gate_idx, h, ys, dest, route_gate, sh_in, sh_out, ln_g, ln_b, alpha, geom, tm=256):
    t, d = x.shape
    f = sh_out.shape[0]
    assert d == 2 * ROW_TILE * LANES
    n_steps = t // tm
    row = functools.partial(_mod_row, tm=tm, **geom)
    tiles = _dest_tiles(dest, tm)
    return pl.pallas_call(
        functools.partial(_ffn_out_kernel, alpha=alpha, tm=tm, n_steps=n_steps),
        grid=(n_steps,),
        in_specs=[pl.BlockSpec((1, 1, TOP_K * tm), lambda i: (i, 0, 0)),
                  pl.BlockSpec((1, 1, TOP_K * tm), lambda i: (jnp.minimum(i + 1, n_steps - 1), 0, 0)),
                  pl.BlockSpec((tm, d), lambda i: (i, 0)),
                  pl.BlockSpec((1, 1, d), lambda i: (row(i), 0, gate_idx)),
                  pl.BlockSpec((tm * ROW_TILE, LANES), lambda i: (i, 0)),
                  pl.BlockSpec((tm, TOP_K), lambda i: (i, 0)),
                  pl.BlockSpec((d, f), lambda i: (0, 0)),
                  pl.BlockSpec((d, f), lambda i: (0, 1)),
                  pl.BlockSpec((f, d), lambda i: (0, 0)),
                  pl.BlockSpec((1, d), lambda i: (0, 0)),
                  pl.BlockSpec((1, d), lambda i: (0, 0)),
                  pl.BlockSpec(memory_space=pl.ANY)],
        out_specs=pl.BlockSpec((tm, d), lambda i: (i, 0)),
        out_shape=jax.ShapeDtypeStruct((t, d), F32),
        scratch_shapes=[pltpu.SMEM((TOP_K * tm,), jnp.int32),
                        pltpu.VMEM((2, TOP_K, tm * ROW_TILE, LANES), jnp.uint32),
                        pltpu.SemaphoreType.DMA((3,))],
        compiler_params=_cparams(("arbitrary",)),
        name="ffn_out",
    )(tiles, tiles, x, mod, h, route_gate, sh_in, sh_in, sh_out, ln_g.reshape(1, d), ln_b.reshape(1, d), ys)


def _moe_sublayer(x, mod, geom, layer, router_w, router_bias, w_in, w_out, sh_in, sh_out, ln_g, ln_b, alpha):
    h, top_e, gate, rank, counts = _route(x, mod, 4, 3, router_w, router_bias, geom)
    n_blocks = x.shape[0] * TOP_K // MOE_BLOCK + N_EXPERTS
    dest, pad_ends, block_e, n_used = _dispatch_plan(top_e, rank, counts, n_blocks)
    xs = _dispatch(h, dest, pad_ends, n_used, n_blocks)
    ys = _experts(xs, block_e, n_used, w_in, w_out, layer)
    return _ffn_out(x, mod, 5, h, ys, dest, gate.T, sh_in.astype(BF16), sh_out.astype(BF16), ln_g, ln_b, alpha,
                    geom)


def kernel(x, c, ctx, c_ctx, ada_w, ada_b, ln_mix_g, ln_mix_b, ln_ffn_g, ln_ffn_b, ab_w_in, ab_w_out, diff_lambda, diff_subln_g, gqa_q_norm_g, gqa_k_norm_g, mla_w_down, mla_q_norm_g, mla_w_uq, mla_kv_norm_g, mla_w_ukv, mla_w_o, router_w, router_bias, expert_w_in, expert_w_out, shared_w_in, shared_w_out):
    bsz, seq, d = x.shape
    n_ctx = ctx.shape[1]
    depth = ada_w.shape[0]
    assert depth == 2 and seq % GRID_W == 0 and bsz + 1 <= MOD_ROWS
    alpha = (2 * depth) ** 0.25
    n_ctx_rows = bsz * n_ctx
    n_lat_rows = bsz * seq
    geom_all = dict(n_ctx_rows=n_ctx_rows, seq=seq, ctx_row=bsz)
    geom_lat = dict(n_ctx_rows=0, seq=seq, ctx_row=bsz)

    tables64 = _rope_tables(seq // GRID_W, 64)
    tables128 = _rope_tables(seq // GRID_W, HEAD_DIM)

    cvec = jnp.zeros((MOD_ROWS, d), F32).at[:bsz].set(c).at[bsz].set(c_ctx)
    mods = _ada_mod(cvec, ada_w, ada_b)
    mods = mods.reshape(depth, MOD_ROWS, 1, 6 * d)

    xt = jnp.concatenate([ctx.reshape(n_ctx_rows, d), x.reshape(n_lat_rows, d)], axis=0)

    mod = mods[0]
    n_diff = d // (2 * HEAD_DIM)
    n_gqa = d // (2 * HEAD_DIM)
    n_gqa_kv = n_gqa // 4
    proj = _modproj(xt, mod, 1, 0, ab_w_in[0].astype(BF16), BF16, geom_all)
    lam_init = 0.8 - 0.6 * math.exp(-0.3 * 0)
    oa = _diff_attn(proj, tables64, diff_lambda[0], diff_subln_g[0], lam_init, bsz, n_ctx, seq, n_diff)
    ob = _gqa_attn(proj, tables128, gqa_q_norm_g[0], gqa_k_norm_g[0], 3 * n_diff, bsz, n_ctx, seq, n_gqa, n_gqa_kv)
    w_out = ab_w_out[0].astype(BF16)
    wa, wb = w_out[:n_diff * HEAD_DIM], w_out[n_diff * HEAD_DIM:]
    t_all = n_ctx_rows + n_lat_rows
    xt = _resln(xt, mod, 2, [oa, ob], [wa, wb], ln_mix_g[0], ln_mix_b[0], alpha, geom_all, 0, t_all, 0)
    xt = _moe_sublayer(xt, mod, geom_all, 0, router_w[0], router_bias[0], expert_w_in, expert_w_out,
                       shared_w_in[0], shared_w_out[0], ln_ffn_g[0], ln_ffn_b[0], alpha)

    mod = mods[1]
    n_mla = d // HEAD_DIM
    qk = MLA_NOPE_DIM + MLA_ROPE_DIM
    w_down = jnp.pad(mla_w_down[0].astype(BF16), ((0, 0), (0, LANES - MLA_ROPE_DIM)))
    down = _modproj(xt, mod, 1, 0, w_down, F32, geom_all)
    w_uq = mla_w_uq[0].reshape(MLA_Q_RANK, n_mla, qk)
    w_uq_rope = jnp.pad(w_uq[:, :, MLA_NOPE_DIM:], ((0, 0), (0, 0), (0, LANES - MLA_ROPE_DIM)))
    w_uq = jnp.concatenate([w_uq[:, :, :MLA_NOPE_DIM].reshape(MLA_Q_RANK, -1),
                            w_uq_rope.reshape(MLA_Q_RANK, -1)], axis=1).astype(BF16)
    lat_blocks = n_ctx_rows // 1024
    q = _normproj(down, 0, mla_q_norm_g[0], w_uq, lat_blocks, n_lat_rows)
    kv = _normproj(down, 1, mla_kv_norm_g[0], mla_w_ukv[0].astype(BF16), 0, t_all)
    kpe_col = (MLA_Q_RANK + MLA_KV_RANK) // LANES
    o = _mla_attn(q, kv, down, tables64, bsz, n_ctx, seq, n_mla, kpe_col)
    xl = _resln(xt, mod, 2, [o], [mla_w_o[0].astype(BF16)], ln_mix_g[1], ln_mix_b[1], alpha, geom_all,
                n_ctx_rows // 512, n_lat_rows, 0)
    xl = _moe_sublayer(xl, mod, geom_lat, 1, router_w[1], router_bias[1], expert_w_in, expert_w_out,
                       shared_w_in[1], shared_w_out[1], ln_ffn_g[1], ln_ffn_b[1], alpha)
    return xl.reshape(bsz, seq, d)
```
